```python
import math
import jax, jax.numpy as jnp
from jax import lax
import numpy as np

D_MODEL = 2048
BATCH = 1
SEQ = 8192
DEPTH = 1
DEC_BATCH = 8
DEC_SEQ = 64
PAST_LEN = 4096

CHUNK = 64
QBLOCK = 128
MIX_WIDTH = D_MODEL
HG_WIDTH = MIX_WIDTH // 2
ATT_WIDTH = MIX_WIDTH - HG_WIDTH
HG_EXPAND = 128
HG_HEADS = HG_WIDTH // HG_EXPAND
HG_DK = HG_EXPAND
HG_DV = HG_WIDTH // HG_HEADS
ATT_HEADS = 8
ATT_DV = ATT_WIDTH // ATT_HEADS
ATT_DH = ATT_DV // 2
D_FF = 4 * D_MODEL
IN_WIDTH = 4 * HG_WIDTH + 3 * ATT_WIDTH
EPS = 1e-6
NEG_INF = -1e30

kernel_name = "hymba_hgrn2_diffattn_streaming_step"


def rmsnorm(x, gain, out_dtype):
    xf = x.astype(jnp.float32)
    y = xf * lax.rsqrt(jnp.mean(xf * xf, axis=-1, keepdims=True) + EPS)
    return (y * gain.astype(jnp.float32)).astype(out_dtype)


def lambda_init_fn(layer_idx):
    return 0.8 - 0.6 * math.exp(-0.3 * layer_idx)


def gla_chunkwise(q, k, v, logf, s0):
    B, L, H, DK = q.shape
    DV = v.shape[-1]
    n = -(-L // CHUNK)
    pad = n * CHUNK - L

    def prep(a):
        a = jnp.pad(a.astype(jnp.float32), ((0, 0), (0, pad), (0, 0), (0, 0)))
        return a.reshape(B, n, CHUNK, H, a.shape[-1]).transpose(1, 0, 2, 3, 4)

    qs, ks, vs, gs = prep(q), prep(k), prep(v), prep(logf)
    causal = jnp.tril(jnp.ones((CHUNK, CHUNK), dtype=bool))[None, :, :, None, None]

    def step(S, xs):
        qc, kc, vc, gc = xs
        b = jnp.cumsum(gc, axis=1)
        o_inter = jnp.einsum('bthk,bhkv->bthv', qc * jnp.exp(b), S)
        diff = b[:, :, None] - b[:, None, :]
        decay = jnp.where(causal, jnp.exp(jnp.minimum(diff, 0.0)), 0.0)
        A = jnp.einsum('bthk,bshk,btshk->bhts', qc, kc, decay)
        o_intra = jnp.einsum('bhts,bshv->bthv', A, vc)
        b_last = b[:, -1]
        k_dec = kc * jnp.exp(b_last[:, None] - b)
        S_new = jnp.exp(b_last)[..., None] * S + jnp.einsum('bshk,bshv->bhkv', k_dec, vc)
        return S_new, o_inter + o_intra

    S_fin, o = lax.scan(step, s0.astype(jnp.float32), (qs, ks, vs, gs))
    o = o.transpose(1, 0, 2, 3, 4).reshape(B, n * CHUNK, H, DV)[:, :L]
    return o, S_fin


def diff_attn_block(q, k, v, q_pos, k_pos, lam):
    s = jnp.einsum('bqhcd,bkhcd->bhcqk', q.astype(jnp.float32), k.astype(jnp.float32)) * (ATT_DH ** -0.5)
    mask = (k_pos // CHUNK)[None, :] <= (q_pos // CHUNK)[:, None]
    s = jnp.where(mask, s, NEG_INF)
    p = jax.nn.softmax(s, axis=-1)
    w = p[:, :, 0] - lam * p[:, :, 1]
    return jnp.einsum('bhqk,bkhv->bqhv', w, v.astype(jnp.float32))


def trunk_layer(x, l, s0, past_k, past_v, norm_attn, w_in, lower_bounds, hg_norm,
                lambda_q1, lambda_k1, lambda_q2, lambda_k2, subln, w_out, norm_mlp, w_up, w_down):
    B, L, _ = x.shape
    dt = x.dtype
    h = rmsnorm(x, norm_attn[l], dt)
    p = h @ w_in[l]
    o0 = 0
    hq = p[..., o0:o0 + HG_WIDTH]; o0 += HG_WIDTH
    hf = p[..., o0:o0 + HG_WIDTH]; o0 += HG_WIDTH
    hi = p[..., o0:o0 + HG_WIDTH]; o0 += HG_WIDTH
    hg = p[..., o0:o0 + HG_WIDTH]; o0 += HG_WIDTH
    aq = p[..., o0:o0 + ATT_WIDTH]; o0 += ATT_WIDTH
    ak = p[..., o0:o0 + ATT_WIDTH]; o0 += ATT_WIDTH
    av = p[..., o0:o0 + ATT_WIDTH]

    lb = jnp.cumsum(jax.nn.softmax(lower_bounds.astype(jnp.float32), axis=0), axis=0)[l]
    f = lb + (1.0 - lb) * jax.nn.sigmoid(hf.astype(jnp.float32))
    logf = jnp.log(f)
    kg = 1.0 - f
    qg = jax.nn.silu(hq)
    heads_k = lambda a: a.reshape(B, L, HG_HEADS, HG_DK)
    o_hg, s_new = gla_chunkwise(heads_k(qg), heads_k(kg), hi.reshape(B, L, HG_HEADS, HG_DV),
                                heads_k(logf), s0)
    o_hg = rmsnorm(o_hg, hg_norm[l], jnp.float32) * jax.nn.silu(
        hg.reshape(B, L, HG_HEADS, HG_DV).astype(jnp.float32))

    lam_init = lambda_init_fn(l)
    lam = (jnp.exp(jnp.sum(lambda_q1[l].astype(jnp.float32) * lambda_k1[l].astype(jnp.float32)))
           - jnp.exp(jnp.sum(lambda_q2[l].astype(jnp.float32) * lambda_k2[l].astype(jnp.float32)))
           + lam_init)
    q5 = aq.reshape(B, L, ATT_HEADS, 2, ATT_DH)
    k5 = ak.reshape(B, L, ATT_HEADS, 2, ATT_DH)
    v4 = av.reshape(B, L, ATT_HEADS, ATT_DV)
    if past_k is None:
        k_pos = jnp.arange(L)

        def blk(i):
            qb = lax.dynamic_slice_in_dim(q5, i * QBLOCK, QBLOCK, axis=1)
            q_pos = i * QBLOCK + jnp.arange(QBLOCK)
            return diff_attn_block(qb, k5, v4, q_pos, k_pos, lam)

        o_at = lax.map(blk, jnp.arange(L // QBLOCK))
        o_at = o_at.transpose(1, 0, 2, 3, 4).reshape(B, L, ATT_HEADS, ATT_DV)
    else:
        P = past_k.shape[1]
        k_all = jnp.concatenate([past_k.reshape(B, P, ATT_HEADS, 2, ATT_DH).astype(k5.dtype), k5], axis=1)
        v_all = jnp.concatenate([past_v.astype(v4.dtype), v4], axis=1)
        q_pos = P + jnp.arange(L)
        k_pos = jnp.arange(P + L)
        o_at = diff_attn_block(q5, k_all, v_all, q_pos, k_pos, lam)
    o_at = rmsnorm(o_at, subln[l], jnp.float32) * (1.0 - lam_init)

    mix = jnp.concatenate([o_hg.reshape(B, L, HG_WIDTH), o_at.reshape(B, L, ATT_WIDTH)], axis=-1).astype(dt)
    x = x + mix @ w_out[l]
    hm = rmsnorm(x, norm_mlp[l], dt)
    x = x + jnp.square(jax.nn.relu(hm @ w_up[l])) @ w_down[l]
    return x, ak.reshape(B, L, ATT_HEADS, 2 * ATT_DH), v4, s_new


def setup_inputs(seed: int = 0) -> dict:
    key = jax.random.key(seed)
    ks = jax.random.split(key, 24)
    nrm = lambda k, shape, scale: jax.random.normal(k, shape, jnp.float32) * scale
    return {
        "x_prompt": nrm(ks[0], (BATCH, SEQ, D_MODEL), 1.0),
        "x_sample": nrm(ks[1], (DEC_BATCH, DEC_SEQ, D_MODEL), 1.0),
        "cache_k": nrm(ks[2], (DEPTH, DEC_BATCH, PAST_LEN, ATT_HEADS, 2 * ATT_DH), 1.0),
        "cache_v": nrm(ks[3], (DEPTH, DEC_BATCH, PAST_LEN, ATT_HEADS, ATT_DV), 1.0),
        "state_hgrn": nrm(ks[4], (DEPTH, DEC_BATCH, HG_HEADS, HG_DK, HG_DV), 0.1),
        "norm_attn": 1.0 + nrm(ks[5], (DEPTH, D_MODEL), 0.02),
        "w_in": nrm(ks[6], (DEPTH, D_MODEL, IN_WIDTH), D_MODEL ** -0.5),
        "lower_bounds": nrm(ks[7], (DEPTH + 1, HG_WIDTH), 0.1),
        "hg_norm": 1.0 + nrm(ks[8], (DEPTH, HG_DV), 0.02),
        "lambda_q1": nrm(ks[9], (DEPTH, ATT_DH), 0.1),
        "lambda_k1": nrm(ks[10], (DEPTH, ATT_DH), 0.1),
        "lambda_q2": nrm(ks[11], (DEPTH, ATT_DH), 0.1),
        "lambda_k2": nrm(ks[12], (DEPTH, ATT_DH), 0.1),
        "subln": 1.0 + nrm(ks[13], (DEPTH, ATT_DV), 0.02),
        "w_out": nrm(ks[14], (DEPTH, MIX_WIDTH, D_MODEL), MIX_WIDTH ** -0.5),
        "norm_mlp": 1.0 + nrm(ks[15], (DEPTH, D_MODEL), 0.02),
        "w_up": nrm(ks[16], (DEPTH, D_MODEL, D_FF), D_MODEL ** -0.5),
        "w_down": nrm(ks[17], (DEPTH, D_FF, D_MODEL), D_FF ** -0.5),
        "norm_final": 1.0 + nrm(ks[18], (D_MODEL,), 0.02),
    }


def reference(x_prompt, x_sample, cache_k, cache_v, state_hgrn, norm_attn, w_in, lower_bounds,
              hg_norm, lambda_q1, lambda_k1, lambda_q2, lambda_k2, subln, w_out, norm_mlp,
              w_up, w_down, norm_final):
    xp, xs = x_prompt, x_sample
    kp_list, vp_list, sp_list, ksl, vsl, ssl = [], [], [], [], [], []
    s0_prompt = jnp.zeros((xp.shape[0], HG_HEADS, HG_DK, HG_DV), jnp.float32)
    for l in range(DEPTH):
        xp, kp, vp, sp = trunk_layer(xp, l, s0_prompt, None, None, norm_attn, w_in, lower_bounds,
                                     hg_norm, lambda_q1, lambda_k1, lambda_q2, lambda_k2, subln,
                                     w_out, norm_mlp, w_up, w_down)
        xs, k_s, v_s, s_s = trunk_layer(xs, l, state_hgrn[l], cache_k[l], cache_v[l], norm_attn, w_in,
                                        lower_bounds, hg_norm, lambda_q1, lambda_k1, lambda_q2,
                                        lambda_k2, subln, w_out, norm_mlp, w_up, w_down)
        kp_list.append(kp); vp_list.append(vp); sp_list.append(sp)
        ksl.append(k_s); vsl.append(v_s); ssl.append(s_s)
    y_prompt = rmsnorm(xp, norm_final, xp.dtype)
    y_sample = rmsnorm(xs, norm_final, xs.dtype)
    new_k_prompt = jnp.stack(kp_list, axis=0)
    new_v_prompt = jnp.stack(vp_list, axis=0)
    new_state_prompt = jnp.stack(sp_list, axis=0)
    new_k_sample = jnp.stack(ksl, axis=0)
    new_v_sample = jnp.stack(vsl, axis=0)
    new_state_sample = jnp.stack(ssl, axis=0)
    return (y_prompt, y_sample, new_k_prompt, new_v_prompt, new_state_prompt,
            new_k_sample, new_v_sample, new_state_sample)
```

```python
import functools
import math

import jax
import jax.numpy as jnp
from jax import lax
from jax.experimental import pallas as pl
from jax.experimental.pallas import tpu as pltpu

F32 = jnp.float32
BF16 = jnp.bfloat16

D_MODEL = 2048
HG_WIDTH = 1024
ATT_WIDTH = 1024
N_HEADS = 8
HEAD_W = 128
ATT_DH = 64
CHUNK = 64
SUB = 16
N_SUB = CHUNK // SUB
D_FF = 4 * D_MODEL
N_SEG = 7
EPS = 1e-6
NEG_INF = -1e30
ATT_SCALE = ATT_DH ** -0.5
LAM_INIT = 0.8 - 0.6 * math.exp(-0.3 * 0)

VMEM_LIMIT = 56 * 1024 * 1024


def _rmsnorm_rows(x, gain):
    return x * lax.rsqrt(jnp.mean(x * x, axis=-1, keepdims=True) + EPS) * gain


def _sigmoid(x):
    return 1.0 / (1.0 + jnp.exp(-x))


def _inproj_kernel(x_ref, gain_ref, w_ref, gates_ref, q_ref, kf_ref, vf_ref, kb_ref, vb_ref, h_ref):
    j = pl.program_id(1)

    @pl.when(j == 0)
    def _():
        h_ref[...] = _rmsnorm_rows(x_ref[...], gain_ref[...]).astype(BF16)

    acc = jnp.dot(h_ref[...], w_ref[...], preferred_element_type=F32)

    @pl.when(j < 4)
    def _():
        gates_ref[0] = acc

    @pl.when(j == 4)
    def _():
        q_ref[...] = (acc * ATT_SCALE).astype(BF16)

    @pl.when(j == 5)
    def _():
        kf_ref[...] = acc
        kb_ref[...] = acc.astype(BF16)

    @pl.when(j == 6)
    def _():
        vf_ref[...] = acc
        vb_ref[...] = acc.astype(BF16)


def _inproj(x, gain, w_bf, tm):
    m = x.shape[0]
    seg = HG_WIDTH
    row = lambda i, j: (i, 0)
    return pl.pallas_call(
        _inproj_kernel,
        grid=(m // tm, N_SEG),
        in_specs=[
            pl.BlockSpec((tm, D_MODEL), row),
            pl.BlockSpec((1, D_MODEL), lambda i, j: (0, 0)),
            pl.BlockSpec((D_MODEL, seg), lambda i, j: (0, j)),
        ],
        out_specs=[
            pl.BlockSpec((1, tm, seg), lambda i, j: (jnp.minimum(j, 3), i, 0)),
            pl.BlockSpec((tm, seg), row),
            pl.BlockSpec((tm, seg), row),
            pl.BlockSpec((tm, seg), row),
            pl.BlockSpec((tm, seg), row),
            pl.BlockSpec((tm, seg), row),
        ],
        out_shape=[
            jax.ShapeDtypeStruct((4, m, seg), F32),
            jax.ShapeDtypeStruct((m, seg), BF16),
            jax.ShapeDtypeStruct((m, seg), F32),
            jax.ShapeDtypeStruct((m, seg), F32),
            jax.ShapeDtypeStruct((m, seg), BF16),
            jax.ShapeDtypeStruct((m, seg), BF16),
        ],
        scratch_shapes=[pltpu.VMEM((tm, D_MODEL), BF16)],
        compiler_params=pltpu.CompilerParams(
            dimension_semantics=("arbitrary", "arbitrary"), vmem_limit_bytes=VMEM_LIMIT),
        name="inproj",
    )(x, gain, w_bf)


def _gla_kernel(hq_ref, hf_ref, hi_ref, hg_ref, lbnd_ref, gn_ref, s0_ref, o_ref, sout_ref, st_ref,
                *, n_chunks):
    r = pl.program_id(2)

    @pl.when(r == 0)
    def _():
        st_ref[...] = s0_ref[0, 0].T

    lbs = lbnd_ref[...]
    e = jnp.exp(lbs - jnp.max(lbs, axis=0, keepdims=True))
    lb = e[0:1, :] / jnp.sum(e, axis=0, keepdims=True)
    gn = gn_ref[...]

    ri = lax.broadcasted_iota(jnp.int32, (CHUNK, CHUNK), 0)
    ci = lax.broadcasted_iota(jnp.int32, (CHUNK, CHUNK), 1)
    tril = (ri >= ci).astype(F32)
    t3 = lax.broadcasted_iota(jnp.int32, (SUB, SUB, HEAD_W), 0)
    s3 = lax.broadcasted_iota(jnp.int32, (SUB, SUB, HEAD_W), 1)
    causal3 = s3 <= t3
    ones_bf = jnp.ones((HEAD_W, HEAD_W), BF16)
    gr = lax.broadcasted_iota(jnp.int32, (SUB, SUB * SUB), 0)
    gc = lax.broadcasted_iota(jnp.int32, (SUB, SUB * SUB), 1)
    seg_sum = ((gc >> 4) == gr).astype(BF16)
    n_off = (N_SUB - 1) * SUB
    n_key = SUB * (N_SUB * (N_SUB - 1) // 2)
    orow = lax.broadcasted_iota(jnp.int32, (n_off, n_key), 0) >> 4
    ocol = lax.broadcasted_iota(jnp.int32, (n_off, n_key), 1)
    ocol_seg = jnp.where(ocol < SUB, 0, jnp.where(ocol < 3 * SUB, 1, 2))
    off_mask = orow == ocol_seg

    def chunk(c, carry):
        r0 = pl.multiple_of(c * CHUNK, CHUNK)
        hq = hq_ref[0, pl.ds(r0, CHUNK), :]
        hf = hf_ref[0, pl.ds(r0, CHUNK), :]
        v = hi_ref[0, pl.ds(r0, CHUNK), :]
        hg = hg_ref[0, pl.ds(r0, CHUNK), :]

        f = lb + (1.0 - lb) * _sigmoid(hf)
        g = jnp.log(f)
        kk = 1.0 - f
        q = hq * _sigmoid(hq)
        b = jnp.dot(tril, g, precision=lax.Precision.HIGHEST, preferred_element_type=F32)

        st = st_ref[...]
        b_last = b[CHUNK - 1:CHUNK, :]
        q_in = (q * jnp.exp(b)).astype(BF16)
        o_inter = lax.dot_general(q_in, st.astype(BF16), (((1,), (1,)), ((), ())),
                                  preferred_element_type=F32)
        k_dec = (kk * jnp.exp(b_last - b)).astype(BF16)
        upd = lax.dot_general(v.astype(BF16), k_dec, (((0,), (0,)), ((), ())),
                              preferred_element_type=F32)
        st_ref[...] = jnp.exp(b_last) * st + upd

        qt, kh, vh = [], [], []
        for i in range(1, N_SUB):
            lo = i * SUB
            b_ref = b[lo - 1:lo, :]
            qt.append(q[lo:lo + SUB] * jnp.exp(b[lo:lo + SUB] - b_ref))
            kh.append(kk[:lo] * jnp.exp(b_ref - b[:lo]))
            vh.append(v[:lo])
        qt = jnp.concatenate(qt, axis=0).astype(BF16)
        kh = jnp.concatenate(kh, axis=0).astype(BF16)
        vh = jnp.concatenate(vh, axis=0).astype(BF16)
        a_off = lax.dot_general(qt, kh, (((1,), (1,)), ((), ())), preferred_element_type=F32)
        a_off = jnp.where(off_mask, a_off, 0.0).astype(BF16)
        o_off = jnp.dot(a_off, vh, preferred_element_type=F32)

        o_rows = []
        for i in range(N_SUB):
            lo = i * SUB
            bb = b[lo:lo + SUB]
            p = (q[lo:lo + SUB][:, None, :] * kk[lo:lo + SUB][None, :, :]) * jnp.exp(
                bb[:, None, :] - bb[None, :, :])
            p = jnp.where(causal3, p, 0.0).reshape(SUB * SUB, HEAD_W).astype(BF16)
            a_rep = jnp.dot(p, ones_bf, preferred_element_type=F32)
            w = a_rep.reshape(SUB, SUB, HEAD_W) * v[lo:lo + SUB][None, :, :]
            o_d = jnp.dot(seg_sum, w.reshape(SUB * SUB, HEAD_W).astype(BF16),
                          preferred_element_type=F32)
            if i > 0:
                o_d = o_d + o_off[lo - SUB:lo]
            o_rows.append(o_d)
        o = o_inter + jnp.concatenate(o_rows, axis=0)

        y = _rmsnorm_rows(o, gn) * (hg * _sigmoid(hg))
        o_ref[pl.ds(r0, CHUNK), :] = y.astype(BF16)
        return carry

    lax.fori_loop(0, n_chunks, chunk, 0)

    @pl.when(r == pl.num_programs(2) - 1)
    def _():
        sout_ref[0, 0] = st_ref[...].T


def _gla(gates, lower_bounds, hg_norm, s0, n_seq, seq_len, rows):
    m = gates.shape[1]
    nr = seq_len // rows
    seg = lambda s: pl.BlockSpec((1, rows, HEAD_W), lambda b, h, r: (s, b * nr + r, h))
    return pl.pallas_call(
        functools.partial(_gla_kernel, n_chunks=rows // CHUNK),
        grid=(n_seq, N_HEADS, nr),
        in_specs=[
            seg(0), seg(1), seg(2), seg(3),
            pl.BlockSpec((lower_bounds.shape[0], HEAD_W), lambda b, h, r: (0, h)),
            pl.BlockSpec((1, HEAD_W), lambda b, h, r: (0, 0)),
            pl.BlockSpec((1, 1, HEAD_W, HEAD_W), lambda b, h, r: (b, h, 0, 0)),
        ],
        out_specs=[
            pl.BlockSpec((rows, HEAD_W), lambda b, h, r: (b * nr + r, h)),
            pl.BlockSpec((1, 1, HEAD_W, HEAD_W), lambda b, h, r: (b, h, 0, 0)),
        ],
        out_shape=[
            jax.ShapeDtypeStruct((m, HG_WIDTH), BF16),
            jax.ShapeDtypeStruct((n_seq, N_HEADS, HEAD_W, HEAD_W), F32),
        ],
        scratch_shapes=[pltpu.VMEM((HEAD_W, HEAD_W), F32)],
        compiler_params=pltpu.CompilerParams(
            dimension_semantics=("arbitrary", "arbitrary", "arbitrary"), vmem_limit_bytes=VMEM_LIMIT),
        name="hgrn2",
    )(gates, gates, gates, gates, lower_bounds, hg_norm, s0)


def _stacked_query(q):
    lane = lax.broadcasted_iota(jnp.int32, q.shape, 1)
    qbig = jnp.concatenate([jnp.where(lane < ATT_DH, q, 0.0), jnp.where(lane >= ATT_DH, q, 0.0)], axis=0)
    return qbig.T.astype(BF16)


def _softmax_update(s, v_bf, m_ref, l_ref, acc_ref):
    m_prev = m_ref[...]
    m_new = jnp.maximum(m_prev, jnp.max(s, axis=0, keepdims=True))
    alpha = jnp.exp(m_prev - m_new)
    p = jnp.exp(s - m_new)
    l_ref[...] = alpha * l_ref[...] + jnp.sum(p, axis=0, keepdims=True)
    pv = lax.dot_general(v_bf, p.astype(BF16), (((0,), (0,)), ((), ())), preferred_element_type=F32)
    acc_ref[...] = alpha * acc_ref[...] + pv
    m_ref[...] = m_new


def _lambda(lq1_ref, lk1_ref, lq2_ref, lk2_ref):
    s1 = jnp.sum(lq1_ref[...] * lk1_ref[...], axis=-1, keepdims=True)
    s2 = jnp.sum(lq2_ref[...] * lk2_ref[...], axis=-1, keepdims=True)
    return jnp.exp(s1) - jnp.exp(s2) + LAM_INIT


def _attn_finish(acc, l, lam, sub, n):
    o_both = (acc * (1.0 / l)).T
    o = o_both[:n] - lam * o_both[n:]
    return _rmsnorm_rows(o, sub) * (1.0 - LAM_INIT)


def _attn_prompt_kernel(q_ref, k_ref, v_ref, lq1_ref, lk1_ref, lq2_ref, lk2_ref, sub_ref, o_ref,
                        qq_ref, m_ref, l_ref, acc_ref, *, tq, tk):
    qi = pl.program_id(1)
    qq_ref[...] = _stacked_query(q_ref[...].astype(F32))
    m_ref[...] = jnp.full(m_ref.shape, NEG_INF, F32)
    l_ref[...] = jnp.zeros(l_ref.shape, F32)
    acc_ref[...] = jnp.zeros(acc_ref.shape, F32)

    def step(kv, masked):
        k0 = pl.multiple_of(kv * tk, tk)
        s = jnp.dot(k_ref[pl.ds(k0, tk), :], qq_ref[...], preferred_element_type=F32)
        if masked:
            kpos = k0 + lax.broadcasted_iota(jnp.int32, s.shape, 0)
            col = lax.broadcasted_iota(jnp.int32, s.shape, 1)
            qpos = qi * tq + jnp.where(col >= tq, col - tq, col)
            s = jnp.where((kpos >> 6) <= (qpos >> 6), s, NEG_INF)
        _softmax_update(s, v_ref[pl.ds(k0, tk), :], m_ref, l_ref, acc_ref)

    n_full = qi * (tq // tk)

    def body(kv, carry):
        step(kv, False)
        return carry

    lax.fori_loop(0, n_full, body, 0)
    for d in range(tq // tk):
        step(n_full + d, True)

    lam = _lambda(lq1_ref, lk1_ref, lq2_ref, lk2_ref)
    o_ref[...] = _attn_finish(acc_ref[...], l_ref[...], lam, sub_ref[...], tq).astype(BF16)


def _attn_prompt(q_bf, k_bf, v_bf, lams, subln, tq, tk):
    n = q_bf.shape[0]
    small = lambda shape: pl.BlockSpec(shape, lambda h, i: (0, 0))
    return pl.pallas_call(
        functools.partial(_attn_prompt_kernel, tq=tq, tk=tk),
        grid=(N_HEADS, n // tq),
        in_specs=[
            pl.BlockSpec((tq, HEAD_W), lambda h, i: (i, h)),
            pl.BlockSpec((n, HEAD_W), lambda h, i: (0, h)),
            pl.BlockSpec((n, HEAD_W), lambda h, i: (0, h)),
            small((1, ATT_DH)), small((1, ATT_DH)), small((1, ATT_DH)), small((1, ATT_DH)),
            small((1, HEAD_W)),
        ],
        out_specs=pl.BlockSpec((tq, HEAD_W), lambda h, i: (i, h)),
        out_shape=jax.ShapeDtypeStruct((n, ATT_WIDTH), BF16),
        scratch_shapes=[
            pltpu.VMEM((HEAD_W, 2 * tq), BF16),
            pltpu.VMEM((1, 2 * tq), F32),
            pltpu.VMEM((1, 2 * tq), F32),
            pltpu.VMEM((HEAD_W, 2 * tq), F32),
        ],
        compiler_params=pltpu.CompilerParams(
            dimension_semantics=("arbitrary", "arbitrary"), vmem_limit_bytes=VMEM_LIMIT),
        name="attn_prompt",
    )(q_bf, k_bf, v_bf, *lams, subln)


def _attn_sample_kernel(q_ref, kc_ref, vc_ref, kn_ref, vn_ref, lq1_ref, lk1_ref, lq2_ref, lk2_ref,
                        sub_ref, o_ref, qq_ref, m_ref, l_ref, acc_ref, *, n_q):
    t = pl.program_id(1)

    @pl.when(t == 0)
    def _():
        for h in range(N_HEADS):
            qq_ref[h] = _stacked_query(q_ref[:, h * HEAD_W:(h + 1) * HEAD_W].astype(F32))
        m_ref[...] = jnp.full(m_ref.shape, NEG_INF, F32)
        l_ref[...] = jnp.zeros(l_ref.shape, F32)
        acc_ref[...] = jnp.zeros(acc_ref.shape, F32)

    def update(h, k_bf, v_bf):
        s = jnp.dot(k_bf, qq_ref[h], preferred_element_type=F32)
        _softmax_update(s, v_bf, m_ref.at[h], l_ref.at[h], acc_ref.at[h])

    for h in range(N_HEADS):
        hs = slice(h * HEAD_W, (h + 1) * HEAD_W)
        update(h, kc_ref[0, :, hs].astype(BF16), vc_ref[0, :, hs].astype(BF16))

    @pl.when(t == pl.num_programs(1) - 1)
    def _():
        lam = _lambda(lq1_ref, lk1_ref, lq2_ref, lk2_ref)
        for h in range(N_HEADS):
            hs = slice(h * HEAD_W, (h + 1) * HEAD_W)
            update(h, kn_ref[:, hs], vn_ref[:, hs])
            o_ref[:, hs] = _attn_finish(acc_ref[h], l_ref[h], lam, sub_ref[...], n_q).astype(BF16)


def _attn_sample(q_bf, cache_k, cache_v, kn_bf, vn_bf, lams, subln, n_q, tk):
    n_b, past = cache_k.shape[0], cache_k.shape[1]
    assert past % CHUNK == 0 and n_q <= CHUNK and past % tk == 0
    small = lambda shape: pl.BlockSpec(shape, lambda b, t: (0, 0))
    rows = pl.BlockSpec((n_q, ATT_WIDTH), lambda b, t: (b, 0))
    cache = pl.BlockSpec((1, tk, ATT_WIDTH), lambda b, t: (b, t, 0))
    return pl.pallas_call(
        functools.partial(_attn_sample_kernel, n_q=n_q),
        grid=(n_b, past // tk),
        in_specs=[
            rows, cache, cache, rows, rows,
            small((1, ATT_DH)), small((1, ATT_DH)), small((1, ATT_DH)), small((1, ATT_DH)),
            small((1, HEAD_W)),
        ],
        out_specs=rows,
        out_shape=jax.ShapeDtypeStruct((n_b * n_q, ATT_WIDTH), BF16),
        scratch_shapes=[
            pltpu.VMEM((N_HEADS, HEAD_W, 2 * n_q), BF16),
            pltpu.VMEM((N_HEADS, 1, 2 * n_q), F32),
            pltpu.VMEM((N_HEADS, 1, 2 * n_q), F32),
            pltpu.VMEM((N_HEADS, HEAD_W, 2 * n_q), F32),
        ],
        compiler_params=pltpu.CompilerParams(
            dimension_semantics=("arbitrary", "arbitrary"), vmem_limit_bytes=VMEM_LIMIT),
        name="attn_sample",
    )(q_bf, cache_k, cache_v, kn_bf, vn_bf, *lams, subln)


def _outproj_kernel(x_ref, a_ref, b_ref, wa_ref, wb_ref, o_ref):
    o_ref[...] = (x_ref[...]
                  + jnp.dot(a_ref[...], wa_ref[...], preferred_element_type=F32)
                  + jnp.dot(b_ref[...], wb_ref[...], preferred_element_type=F32))


def _outproj(x, mix_hg, mix_at, w_bf, tm, tn):
    m = x.shape[0]
    return pl.pallas_call(
        _outproj_kernel,
        grid=(m // tm, D_MODEL // tn),
        in_specs=[
            pl.BlockSpec((tm, tn), lambda i, j: (i, j)),
            pl.BlockSpec((tm, HG_WIDTH), lambda i, j: (i, 0)),
            pl.BlockSpec((tm, ATT_WIDTH), lambda i, j: (i, 0)),
            pl.BlockSpec((HG_WIDTH, tn), lambda i, j: (0, j)),
            pl.BlockSpec((ATT_WIDTH, tn), lambda i, j: (1, j)),
        ],
        out_specs=pl.BlockSpec((tm, tn), lambda i, j: (i, j)),
        out_shape=jax.ShapeDtypeStruct((m, D_MODEL), F32),
        compiler_params=pltpu.CompilerParams(
            dimension_semantics=("arbitrary", "arbitrary"), vmem_limit_bytes=VMEM_LIMIT),
        name="outproj",
    )(x, mix_hg, mix_at, w_bf, w_bf)


def _mlp_kernel(x_ref, gain_ref, wu_ref, wd_ref, gfin_ref, o_ref, h_ref):
    j = pl.program_id(1)

    @pl.when(j == 0)
    def _():
        x = x_ref[...]
        h_ref[...] = _rmsnorm_rows(x, gain_ref[...]).astype(BF16)
        o_ref[...] = x

    u = jnp.dot(h_ref[...], wu_ref[...], preferred_element_type=F32)
    u = jnp.square(jnp.maximum(u, 0.0)).astype(BF16)
    o_ref[...] += jnp.dot(u, wd_ref[...], preferred_element_type=F32)

    @pl.when(j == pl.num_programs(1) - 1)
    def _():
        o_ref[...] = _rmsnorm_rows(o_ref[...], gfin_ref[...])


def _mlp(x, gain, wu_bf, wd_bf, gfin, tm, tf):
    m = x.shape[0]
    return pl.pallas_call(
        _mlp_kernel,
        grid=(m // tm, D_FF // tf),
        in_specs=[
            pl.BlockSpec((tm, D_MODEL), lambda i, j: (i, 0)),
            pl.BlockSpec((1, D_MODEL), lambda i, j: (0, 0)),
            pl.BlockSpec((D_MODEL, tf), lambda i, j: (0, j)),
            pl.BlockSpec((tf, D_MODEL), lambda i, j: (j, 0)),
            pl.BlockSpec((1, D_MODEL), lambda i, j: (0, 0)),
        ],
        out_specs=pl.BlockSpec((tm, D_MODEL), lambda i, j: (i, 0)),
        out_shape=jax.ShapeDtypeStruct((m, D_MODEL), F32),
        scratch_shapes=[pltpu.VMEM((tm, D_MODEL), BF16)],
        compiler_params=pltpu.CompilerParams(
            dimension_semantics=("arbitrary", "arbitrary"), vmem_limit_bytes=VMEM_LIMIT),
        name="mlp",
    )(x, gain, wu_bf, wd_bf, gfin)


def kernel(x_prompt, x_sample, cache_k, cache_v, state_hgrn, norm_attn, w_in, lower_bounds, hg_norm,
           lambda_q1, lambda_k1, lambda_q2, lambda_k2, subln, w_out, norm_mlp, w_up, w_down, norm_final):
    depth = w_in.shape[0]
    assert depth == 1
    n_pb, n_p, _ = x_prompt.shape
    n_sb, n_s, _ = x_sample.shape
    assert n_pb == 1
    past = cache_k.shape[2]

    w_in_bf = w_in[0].astype(BF16)
    w_out_bf = w_out[0].astype(BF16)
    w_up_bf = w_up[0].astype(BF16)
    w_down_bf = w_down[0].astype(BF16)
    lams = (lambda_q1, lambda_k1, lambda_q2, lambda_k2)
    gfin = norm_final.reshape(1, D_MODEL)

    xp = x_prompt.reshape(n_p, D_MODEL)
    xs = x_sample.reshape(n_sb * n_s, D_MODEL)

    gates_p, q_p, kf_p, vf_p, kb_p, vb_p = _inproj(xp, norm_attn, w_in_bf, tm=512)
    gates_s, q_s, kf_s, vf_s, kb_s, vb_s = _inproj(xs, norm_attn, w_in_bf, tm=512)

    s0_p = jnp.zeros((1, N_HEADS, HEAD_W, HEAD_W), F32)
    ohg_p, st_p = _gla(gates_p, lower_bounds, hg_norm, s0_p, n_seq=1, seq_len=n_p, rows=1024)
    ohg_s, st_s = _gla(gates_s, lower_bounds, hg_norm, state_hgrn[0], n_seq=n_sb, seq_len=n_s, rows=n_s)

    oat_p = _attn_prompt(q_p, kb_p, vb_p, lams, subln, tq=512, tk=512)
    ck = cache_k[0].reshape(n_sb, past, ATT_WIDTH)
    cv = cache_v[0].reshape(n_sb, past, ATT_WIDTH)
    oat_s = _attn_sample(q_s, ck, cv, kb_s, vb_s, lams, subln, n_q=n_s, tk=1024)

    x1_p = _outproj(xp, ohg_p, oat_p, w_out_bf, tm=1024, tn=1024)
    x1_s = _outproj(xs, ohg_s, oat_s, w_out_bf, tm=512, tn=1024)

    y_p = _mlp(x1_p, norm_mlp, w_up_bf, w_down_bf, gfin, tm=1024, tf=512)
    y_s = _mlp(x1_s, norm_mlp, w_up_bf, w_down_bf, gfin, tm=512, tf=512)

    return (
        y_p.reshape(n_pb, n_p, D_MODEL),
        y_s.reshape(n_sb, n_s, D_MODEL),
        kf_p.reshape(1, n_pb, n_p, N_HEADS, HEAD_W),
        vf_p.reshape(1, n_pb, n_p, N_HEADS, HEAD_W),
        st_p.reshape(1, n_pb, N_HEADS, HEAD_W, HEAD_W),
        kf_s.reshape(1, n_sb, n_s, N_HEADS, HEAD_W),
        vf_s.reshape(1, n_sb, n_s, N_HEADS, HEAD_W),
        st_s.reshape(1, n_sb, N_HEADS, HEAD_W, HEAD_W),
    )
```

```python
import functools
import math

import jax
import jax.numpy as jnp
from jax import lax
from jax.experimental import pallas as pl
from jax.experimental.pallas import tpu as pltpu

F32 = jnp.float32
BF16 = jnp.bfloat16

D_MODEL = 2048
HG_WIDTH = 1024
ATT_WIDTH = 1024
N_HEADS = 8
HEAD_W = 128
ATT_DH = 64
CHUNK = 64
SUB = 16
N_SUB = CHUNK // SUB
D_FF = 4 * D_MODEL
N_SEG = 7
EPS = 1e-6
NEG_INF = -1e30
LOG2E = 1.4426950408889634
Q_SCALE = ATT_DH ** -0.5 * LOG2E
LAM_INIT = 0.8 - 0.6 * math.exp(-0.3 * 0)

VMEM_LIMIT = 56 * 1024 * 1024


def _rmsnorm_rows(x, gain):
    return x * lax.rsqrt(jnp.mean(x * x, axis=-1, keepdims=True) + EPS) * gain


def _sigmoid(x):
    return 1.0 / (1.0 + jnp.exp(-x))


def _head(h):
    return slice(h * HEAD_W, (h + 1) * HEAD_W)


def _inproj_kernel(x_ref, gain_ref, w_ref, gates_ref, q_ref, kf_ref, vf_ref, kb_ref, vb_ref, h_ref):
    j = pl.program_id(1)

    @pl.when(j == 0)
    def _():
        h_ref[...] = _rmsnorm_rows(x_ref[...], gain_ref[...]).astype(BF16)

    acc = jnp.dot(h_ref[...], w_ref[...], preferred_element_type=F32)

    @pl.when(j < 4)
    def _():
        gates_ref[0] = acc

    @pl.when(j == 4)
    def _():
        q_ref[...] = (acc * Q_SCALE).astype(BF16)

    @pl.when(j == 5)
    def _():
        kf_ref[...] = acc
        kb_ref[...] = acc.astype(BF16)

    @pl.when(j == 6)
    def _():
        vf_ref[...] = acc
        vb_ref[...] = acc.astype(BF16)


def _inproj(x, gain, w_bf, tm):
    m = x.shape[0]
    seg = HG_WIDTH
    row = lambda i, j: (i, 0)
    return pl.pallas_call(
        _inproj_kernel,
        grid=(m // tm, N_SEG),
        in_specs=[
            pl.BlockSpec((tm, D_MODEL), row),
            pl.BlockSpec((1, D_MODEL), lambda i, j: (0, 0)),
            pl.BlockSpec((D_MODEL, seg), lambda i, j: (0, j)),
        ],
        out_specs=[
            pl.BlockSpec((1, tm, seg), lambda i, j: (jnp.minimum(j, 3), i, 0)),
            pl.BlockSpec((tm, seg), row),
            pl.BlockSpec((tm, seg), row),
            pl.BlockSpec((tm, seg), row),
            pl.BlockSpec((tm, seg), row),
            pl.BlockSpec((tm, seg), row),
        ],
        out_shape=[
            jax.ShapeDtypeStruct((4, m, seg), F32),
            jax.ShapeDtypeStruct((m, seg), BF16),
            jax.ShapeDtypeStruct((m, seg), F32),
            jax.ShapeDtypeStruct((m, seg), F32),
            jax.ShapeDtypeStruct((m, seg), BF16),
            jax.ShapeDtypeStruct((m, seg), BF16),
        ],
        scratch_shapes=[pltpu.VMEM((tm, D_MODEL), BF16)],
        compiler_params=pltpu.CompilerParams(
            dimension_semantics=("arbitrary", "arbitrary"), vmem_limit_bytes=VMEM_LIMIT),
        name="inproj",
    )(x, gain, w_bf)


def _gla_kernel(hq_ref, hf_ref, hi_ref, hg_ref, lbnd_ref, gn_ref, s0_ref, o_ref, sout_ref, st_ref,
                q_ref, b_ref, *, n_chunks):
    r = pl.program_id(1)

    @pl.when(r == 0)
    def _():
        st_ref[...] = jnp.concatenate([s0_ref[0, h].T for h in range(N_HEADS)], axis=1)

    lbs = lbnd_ref[...]
    e = jnp.exp(lbs - jnp.max(lbs, axis=0, keepdims=True))
    lb = e[0:1, :] / jnp.sum(e, axis=0, keepdims=True)
    gn = jnp.concatenate([gn_ref[...]] * N_HEADS, axis=1)

    ri = lax.broadcasted_iota(jnp.int32, (CHUNK, CHUNK), 0)
    ci = lax.broadcasted_iota(jnp.int32, (CHUNK, CHUNK), 1)
    tril = (ri >= ci).astype(F32)
    pr = lax.broadcasted_iota(jnp.int32, (2 * HEAD_W, 2 * HEAD_W), 0)
    pc = lax.broadcasted_iota(jnp.int32, (2 * HEAD_W, 2 * HEAD_W), 1)
    pair_ones = ((pr >> 7) == (pc >> 7)).astype(BF16)
    gr = lax.broadcasted_iota(jnp.int32, (SUB, SUB * SUB), 0)
    gc = lax.broadcasted_iota(jnp.int32, (SUB, SUB * SUB), 1)
    seg_sum = (((gc >> 4) == gr) & ((gc & (SUB - 1)) <= gr)).astype(BF16)
    n_off = (N_SUB - 1) * SUB
    n_key = SUB * (N_SUB * (N_SUB - 1) // 2)
    orow = lax.broadcasted_iota(jnp.int32, (n_off, n_key), 0) >> 4
    ocol = lax.broadcasted_iota(jnp.int32, (n_off, n_key), 1)
    ocol_seg = jnp.where(ocol < SUB, 0, jnp.where(ocol < 3 * SUB, 1, 2))
    off_mask = orow == ocol_seg
    nt = (((1,), (1,)), ((), ()))
    tn = (((0,), (0,)), ((), ()))

    def chunk(c, carry):
        r0 = pl.multiple_of(c * CHUNK, CHUNK)
        hq = hq_ref[0, pl.ds(r0, CHUNK), :]
        hf = hf_ref[0, pl.ds(r0, CHUNK), :]
        v = hi_ref[0, pl.ds(r0, CHUNK), :]
        hg = hg_ref[0, pl.ds(r0, CHUNK), :]

        f = lb + (1.0 - lb) * _sigmoid(hf)
        g = jnp.log(f) * LOG2E
        kk = 1.0 - f
        q = hq * _sigmoid(hq)
        b = jnp.dot(tril, g, precision=lax.Precision.HIGHEST, preferred_element_type=F32)
        v_bf = v.astype(BF16)
        q_ref[...] = q
        b_ref[...] = b

        st = st_ref[...]
        st_bf = st.astype(BF16)
        b_last = b[CHUNK - 1:CHUNK, :]
        q_in = (q * jnp.exp2(b)).astype(BF16)
        k_dec = (kk * jnp.exp2(b_last - b)).astype(BF16)
        o_inter = jnp.concatenate(
            [lax.dot_general(q_in[:, _head(h)], st_bf[:, _head(h)], nt, preferred_element_type=F32)
             for h in range(N_HEADS)], axis=1)
        upd = jnp.concatenate(
            [lax.dot_general(v_bf[:, _head(h)], k_dec[:, _head(h)], tn, preferred_element_type=F32)
             for h in range(N_HEADS)], axis=1)
        st_ref[...] = jnp.exp2(b_last) * st + upd

        qt, kh, vh = [], [], []
        for i in range(1, N_SUB):
            lo = i * SUB
            b_start = b[lo - 1:lo, :]
            qt.append(q[lo:lo + SUB] * jnp.exp2(b[lo:lo + SUB] - b_start))
            kh.append(kk[:lo] * jnp.exp2(b_start - b[:lo]))
            vh.append(v_bf[:lo])
        qt = jnp.concatenate(qt, axis=0).astype(BF16)
        kh = jnp.concatenate(kh, axis=0).astype(BF16)
        vh = jnp.concatenate(vh, axis=0)
        o_off = []
        for h in range(N_HEADS):
            a = lax.dot_general(qt[:, _head(h)], kh[:, _head(h)], nt, preferred_element_type=F32)
            a = jnp.where(off_mask, a, 0.0).astype(BF16)
            o_off.append(jnp.dot(a, vh[:, _head(h)], preferred_element_type=F32))
        o_off = jnp.concatenate(o_off, axis=1)

        o_rows = []
        for i in range(N_SUB):
            lo = i * SUB
            bb = b[lo:lo + SUB]
            kb = kk[lo:lo + SUB]
            p = jnp.concatenate(
                [(q_ref[lo + t:lo + t + 1, :] * kb) * jnp.exp2(jnp.minimum(b_ref[lo + t:lo + t + 1, :] - bb, 0.0))
                 for t in range(SUB)], axis=0).astype(BF16)
            a_rep = jnp.concatenate(
                [jnp.dot(p[:, 2 * HEAD_W * j:2 * HEAD_W * (j + 1)], pair_ones, preferred_element_type=F32)
                 for j in range(N_HEADS // 2)], axis=1)
            w = a_rep.reshape(SUB, SUB, HG_WIDTH) * v[lo:lo + SUB][None, :, :]
            o_d = jnp.dot(seg_sum, w.reshape(SUB * SUB, HG_WIDTH).astype(BF16),
                          preferred_element_type=F32)
            if i > 0:
                o_d = o_d + o_off[lo - SUB:lo]
            o_rows.append(o_d)
        o = o_inter + jnp.concatenate(o_rows, axis=0)

        y = jnp.concatenate(
            [o[:, _head(h)] * lax.rsqrt(jnp.mean(o[:, _head(h)] * o[:, _head(h)], axis=-1, keepdims=True) + EPS)
             for h in range(N_HEADS)], axis=1)
        y = y * gn * (hg * _sigmoid(hg))
        o_ref[pl.ds(r0, CHUNK), :] = y.astype(BF16)
        return carry

    lax.fori_loop(0, n_chunks, chunk, 0)

    @pl.when(r == pl.num_programs(1) - 1)
    def _():
        st = st_ref[...]
        for h in range(N_HEADS):
            sout_ref[0, h] = st[:, _head(h)].T


def _gla(gates, lower_bounds, hg_norm, s0, n_seq, seq_len, rows):
    m = gates.shape[1]
    nr = seq_len // rows
    seg = lambda s: pl.BlockSpec((1, rows, HG_WIDTH), lambda b, r: (s, b * nr + r, 0))
    state = pl.BlockSpec((1, N_HEADS, HEAD_W, HEAD_W), lambda b, r: (b, 0, 0, 0))
    return pl.pallas_call(
        functools.partial(_gla_kernel, n_chunks=rows // CHUNK),
        grid=(n_seq, nr),
        in_specs=[
            seg(0), seg(1), seg(2), seg(3),
            pl.BlockSpec((lower_bounds.shape[0], HG_WIDTH), lambda b, r: (0, 0)),
            pl.BlockSpec((1, HEAD_W), lambda b, r: (0, 0)),
            state,
        ],
        out_specs=[pl.BlockSpec((rows, HG_WIDTH), lambda b, r: (b * nr + r, 0)), state],
        out_shape=[
            jax.ShapeDtypeStruct((m, HG_WIDTH), BF16),
            jax.ShapeDtypeStruct((n_seq, N_HEADS, HEAD_W, HEAD_W), F32),
        ],
        scratch_shapes=[pltpu.VMEM((HEAD_W, HG_WIDTH), F32),
                        pltpu.VMEM((CHUNK, HG_WIDTH), F32),
                        pltpu.VMEM((CHUNK, HG_WIDTH), F32)],
        compiler_params=pltpu.CompilerParams(
            dimension_semantics=("arbitrary", "arbitrary"), vmem_limit_bytes=VMEM_LIMIT),
        name="hgrn2",
    )(gates, gates, gates, gates, lower_bounds, hg_norm, s0)


def _stacked_query(q):
    lane = lax.broadcasted_iota(jnp.int32, q.shape, 1)
    qbig = jnp.concatenate([jnp.where(lane < ATT_DH, q, 0.0), jnp.where(lane >= ATT_DH, q, 0.0)], axis=0)
    return qbig.T.astype(BF16)


def _softmax_update(s, v_bf, m_ref, l_ref, acc_ref):
    m_prev = m_ref[...]
    m_new = jnp.maximum(m_prev, jnp.max(s, axis=0, keepdims=True))
    alpha = jnp.exp2(m_prev - m_new)
    p = jnp.exp2(s - m_new)
    l_ref[...] = alpha * l_ref[...] + jnp.sum(p, axis=0, keepdims=True)
    pv = lax.dot_general(v_bf, p.astype(BF16), (((0,), (0,)), ((), ())), preferred_element_type=F32)
    acc_ref[...] = alpha * acc_ref[...] + pv
    m_ref[...] = m_new


def _lambda(lq1_ref, lk1_ref, lq2_ref, lk2_ref):
    s1 = jnp.sum(lq1_ref[...] * lk1_ref[...], axis=-1, keepdims=True)
    s2 = jnp.sum(lq2_ref[...] * lk2_ref[...], axis=-1, keepdims=True)
    return jnp.exp(s1) - jnp.exp(s2) + LAM_INIT


def _attn_finish(acc, l, lam, sub, n):
    o_both = (acc * (1.0 / l)).T
    o = o_both[:n] - lam * o_both[n:]
    return _rmsnorm_rows(o, sub) * (1.0 - LAM_INIT)


def _attn_prompt_kernel(q_ref, k_ref, v_ref, lq1_ref, lk1_ref, lq2_ref, lk2_ref, sub_ref, o_ref,
                        qq_ref, s_ref, p_ref, a_ref, m_ref, l_ref, acc_ref, *, t):
    qi = pl.program_id(1)
    qq_ref[...] = _stacked_query(q_ref[...].astype(F32))
    m_ref[...] = jnp.full(m_ref.shape, NEG_INF, F32)
    l_ref[...] = jnp.zeros(l_ref.shape, F32)
    acc_ref[...] = jnp.zeros(acc_ref.shape, F32)

    def scores(kv):
        k0 = pl.multiple_of(kv * t, t)
        return jnp.dot(k_ref[pl.ds(k0, t), :], qq_ref[...], preferred_element_type=F32)

    def weighted_values(kv, p):
        k0 = pl.multiple_of(kv * t, t)
        return lax.dot_general(v_ref[pl.ds(k0, t), :], p, (((0,), (0,)), ((), ())),
                               preferred_element_type=F32)

    def softmax(s):
        m_prev = m_ref[...]
        m_new = jnp.maximum(m_prev, jnp.max(s, axis=0, keepdims=True))
        alpha = jnp.exp2(m_prev - m_new)
        p = jnp.exp2(s - m_new)
        l_ref[...] = alpha * l_ref[...] + jnp.sum(p, axis=0, keepdims=True)
        m_ref[...] = m_new
        return p.astype(BF16), alpha

    def stage(kv, cur, oth):
        s_ref[oth] = scores(kv + 1)
        acc_ref[...] = a_ref[oth] * acc_ref[...] + weighted_values(jnp.maximum(kv - 1, 0), p_ref[oth])
        p_ref[cur], a_ref[cur] = softmax(s_ref[cur])

    def finish(cur, oth):
        acc = a_ref[oth] * acc_ref[...] + weighted_values(jnp.maximum(qi - 1, 0), p_ref[oth])
        s = s_ref[cur]
        kpos = qi * t + lax.broadcasted_iota(jnp.int32, s.shape, 0)
        col = lax.broadcasted_iota(jnp.int32, s.shape, 1)
        qpos = qi * t + jnp.where(col >= t, col - t, col)
        p, alpha = softmax(jnp.where((kpos >> 6) <= (qpos >> 6), s, NEG_INF))
        acc = alpha * acc + weighted_values(qi, p)
        lam = _lambda(lq1_ref, lk1_ref, lq2_ref, lk2_ref)
        o_ref[...] = _attn_finish(acc, l_ref[...], lam, sub_ref[...], t).astype(BF16)

    s_ref[0] = scores(0)
    p_ref[1] = jnp.zeros(p_ref.shape[1:], BF16)
    a_ref[1] = jnp.ones(a_ref.shape[1:], F32)

    def pair(j, carry):
        stage(2 * j, 0, 1)
        stage(2 * j + 1, 1, 0)
        return carry

    lax.fori_loop(0, qi // 2, pair, 0)

    @pl.when(qi % 2 == 1)
    def _():
        stage(qi - 1, 0, 1)
        finish(1, 0)

    @pl.when(qi % 2 == 0)
    def _():
        finish(0, 1)


def _attn_prompt(q_bf, k_bf, v_bf, lams, subln, t):
    n = q_bf.shape[0]
    small = lambda shape: pl.BlockSpec(shape, lambda h, i: (0, 0))
    return pl.pallas_call(
        functools.partial(_attn_prompt_kernel, t=t),
        grid=(N_HEADS, n // t),
        in_specs=[
            pl.BlockSpec((t, HEAD_W), lambda h, i: (i, h)),
            pl.BlockSpec((n, HEAD_W), lambda h, i: (0, h)),
            pl.BlockSpec((n, HEAD_W), lambda h, i: (0, h)),
            small((1, ATT_DH)), small((1, ATT_DH)), small((1, ATT_DH)), small((1, ATT_DH)),
            small((1, HEAD_W)),
        ],
        out_specs=pl.BlockSpec((t, HEAD_W), lambda h, i: (i, h)),
        out_shape=jax.ShapeDtypeStruct((n, ATT_WIDTH), BF16),
        scratch_shapes=[
            pltpu.VMEM((HEAD_W, 2 * t), BF16),
            pltpu.VMEM((2, t, 2 * t), F32),
            pltpu.VMEM((2, t, 2 * t), BF16),
            pltpu.VMEM((2, 1, 2 * t), F32),
            pltpu.VMEM((1, 2 * t), F32),
            pltpu.VMEM((1, 2 * t), F32),
            pltpu.VMEM((HEAD_W, 2 * t), F32),
        ],
        compiler_params=pltpu.CompilerParams(
            dimension_semantics=("arbitrary", "arbitrary"), vmem_limit_bytes=VMEM_LIMIT),
        name="attn_prompt",
    )(q_bf, k_bf, v_bf, *lams, subln)


def _attn_sample_kernel(q_ref, kc_ref, vc_ref, kn_ref, vn_ref, lq1_ref, lk1_ref, lq2_ref, lk2_ref,
                        sub_ref, o_ref, qq_ref, m_ref, l_ref, acc_ref, *, n_q, tk):
    t = pl.program_id(1)

    @pl.when(t == 0)
    def _():
        for h in range(N_HEADS):
            qq_ref[h] = _stacked_query(q_ref[:, _head(h)].astype(F32))
        m_ref[...] = jnp.full(m_ref.shape, NEG_INF, F32)
        l_ref[...] = jnp.zeros(l_ref.shape, F32)
        acc_ref[...] = jnp.zeros(acc_ref.shape, F32)

    def update(h, k_bf, v_bf):
        s = jnp.dot(k_bf, qq_ref[h], preferred_element_type=F32)
        _softmax_update(s, v_bf, m_ref.at[h], l_ref.at[h], acc_ref.at[h])

    for h in range(N_HEADS):
        rows = pl.ds(h, tk, stride=N_HEADS)
        update(h, kc_ref[0, rows, :].astype(BF16), vc_ref[0, rows, :].astype(BF16))

    @pl.when(t == pl.num_programs(1) - 1)
    def _():
        lam = _lambda(lq1_ref, lk1_ref, lq2_ref, lk2_ref)
        for h in range(N_HEADS):
            update(h, kn_ref[:, _head(h)], vn_ref[:, _head(h)])
            o_ref[:, _head(h)] = _attn_finish(acc_ref[h], l_ref[h], lam, sub_ref[...], n_q).astype(BF16)


def _attn_sample(q_bf, cache_k, cache_v, kn_bf, vn_bf, lams, subln, n_q, tk):
    n_b = cache_k.shape[0]
    past = cache_k.shape[1] // N_HEADS
    assert past % CHUNK == 0 and n_q <= CHUNK and past % tk == 0
    small = lambda shape: pl.BlockSpec(shape, lambda b, t: (0, 0))
    rows = pl.BlockSpec((n_q, ATT_WIDTH), lambda b, t: (b, 0))
    cache = pl.BlockSpec((1, tk * N_HEADS, HEAD_W), lambda b, t: (b, t, 0))
    return pl.pallas_call(
        functools.partial(_attn_sample_kernel, n_q=n_q, tk=tk),
        grid=(n_b, past // tk),
        in_specs=[
            rows, cache, cache, rows, rows,
            small((1, ATT_DH)), small((1, ATT_DH)), small((1, ATT_DH)), small((1, ATT_DH)),
            small((1, HEAD_W)),
        ],
        out_specs=rows,
        out_shape=jax.ShapeDtypeStruct((n_b * n_q, ATT_WIDTH), BF16),
        scratch_shapes=[
            pltpu.VMEM((N_HEADS, HEAD_W, 2 * n_q), BF16),
            pltpu.VMEM((N_HEADS, 1, 2 * n_q), F32),
            pltpu.VMEM((N_HEADS, 1, 2 * n_q), F32),
            pltpu.VMEM((N_HEADS, HEAD_W, 2 * n_q), F32),
        ],
        compiler_params=pltpu.CompilerParams(
            dimension_semantics=("arbitrary", "arbitrary"), vmem_limit_bytes=VMEM_LIMIT),
        name="attn_sample",
    )(q_bf, cache_k, cache_v, kn_bf, vn_bf, *lams, subln)


def _outproj_kernel(x_ref, a_ref, b_ref, wa_ref, wb_ref, o_ref):
    o_ref[...] = (x_ref[...]
                  + jnp.dot(a_ref[...], wa_ref[...], preferred_element_type=F32)
                  + jnp.dot(b_ref[...], wb_ref[...], preferred_element_type=F32))


def _outproj(x, mix_hg, mix_at, w_bf, tm, tn):
    m = x.shape[0]
    return pl.pallas_call(
        _outproj_kernel,
        grid=(m // tm, D_MODEL // tn),
        in_specs=[
            pl.BlockSpec((tm, tn), lambda i, j: (i, j)),
            pl.BlockSpec((tm, HG_WIDTH), lambda i, j: (i, 0)),
            pl.BlockSpec((tm, ATT_WIDTH), lambda i, j: (i, 0)),
            pl.BlockSpec((HG_WIDTH, tn), lambda i, j: (0, j)),
            pl.BlockSpec((ATT_WIDTH, tn), lambda i, j: (1, j)),
        ],
        out_specs=pl.BlockSpec((tm, tn), lambda i, j: (i, j)),
        out_shape=jax.ShapeDtypeStruct((m, D_MODEL), F32),
        compiler_params=pltpu.CompilerParams(
            dimension_semantics=("arbitrary", "arbitrary"), vmem_limit_bytes=VMEM_LIMIT),
        name="outproj",
    )(x, mix_hg, mix_at, w_bf, w_bf)


def _mlp_kernel(x_ref, gain_ref, wu_ref, wd_ref, gfin_ref, o_ref, h_ref):
    j = pl.program_id(1)

    @pl.when(j == 0)
    def _():
        x = x_ref[...]
        h_ref[...] = _rmsnorm_rows(x, gain_ref[...]).astype(BF16)
        o_ref[...] = x

    u = jnp.dot(h_ref[...], wu_ref[...], preferred_element_type=F32)
    u = jnp.square(jnp.maximum(u, 0.0)).astype(BF16)
    o_ref[...] += jnp.dot(u, wd_ref[...], preferred_element_type=F32)

    @pl.when(j == pl.num_programs(1) - 1)
    def _():
        o_ref[...] = _rmsnorm_rows(o_ref[...], gfin_ref[...])


def _mlp(x, gain, wu_bf, wd_bf, gfin, tm, tf):
    m = x.shape[0]
    return pl.pallas_call(
        _mlp_kernel,
        grid=(m // tm, D_FF // tf),
        in_specs=[
            pl.BlockSpec((tm, D_MODEL), lambda i, j: (i, 0)),
            pl.BlockSpec((1, D_MODEL), lambda i, j: (0, 0)),
            pl.BlockSpec((D_MODEL, tf), lambda i, j: (0, j)),
            pl.BlockSpec((tf, D_MODEL), lambda i, j: (j, 0)),
            pl.BlockSpec((1, D_MODEL), lambda i, j: (0, 0)),
        ],
        out_specs=pl.BlockSpec((tm, D_MODEL), lambda i, j: (i, 0)),
        out_shape=jax.ShapeDtypeStruct((m, D_MODEL), F32),
        scratch_shapes=[pltpu.VMEM((tm, D_MODEL), BF16)],
        compiler_params=pltpu.CompilerParams(
            dimension_semantics=("arbitrary", "arbitrary"), vmem_limit_bytes=VMEM_LIMIT),
        name="mlp",
    )(x, gain, wu_bf, wd_bf, gfin)


def kernel(x_prompt, x_sample, cache_k, cache_v, state_hgrn, norm_attn, w_in, lower_bounds, hg_norm,
           lambda_q1, lambda_k1, lambda_q2, lambda_k2, subln, w_out, norm_mlp, w_up, w_down, norm_final):
    depth = w_in.shape[0]
    assert depth == 1
    n_pb, n_p, _ = x_prompt.shape
    n_sb, n_s, _ = x_sample.shape
    assert n_pb == 1
    past = cache_k.shape[2]

    w_in_bf = w_in[0].astype(BF16)
    w_out_bf = w_out[0].astype(BF16)
    w_up_bf = w_up[0].astype(BF16)
    w_down_bf = w_down[0].astype(BF16)
    lams = (lambda_q1, lambda_k1, lambda_q2, lambda_k2)
    gfin = norm_final.reshape(1, D_MODEL)

    xp = x_prompt.reshape(n_p, D_MODEL)
    xs = x_sample.reshape(n_sb * n_s, D_MODEL)

    gates_p, q_p, kf_p, vf_p, kb_p, vb_p = _inproj(xp, norm_attn, w_in_bf, tm=512)
    gates_s, q_s, kf_s, vf_s, kb_s, vb_s = _inproj(xs, norm_attn, w_in_bf, tm=512)

    s0_p = jnp.zeros((1, N_HEADS, HEAD_W, HEAD_W), F32)
    ohg_p, st_p = _gla(gates_p, lower_bounds, hg_norm, s0_p, n_seq=1, seq_len=n_p, rows=512)
    ohg_s, st_s = _gla(gates_s, lower_bounds, hg_norm, state_hgrn[0], n_seq=n_sb, seq_len=n_s, rows=n_s)

    oat_p = _attn_prompt(q_p, kb_p, vb_p, lams, subln, t=512)
    ck = cache_k[0].reshape(n_sb, past * N_HEADS, HEAD_W)
    cv = cache_v[0].reshape(n_sb, past * N_HEADS, HEAD_W)
    oat_s = _attn_sample(q_s, ck, cv, kb_s, vb_s, lams, subln, n_q=n_s, tk=1024)

    x1_p = _outproj(xp, ohg_p, oat_p, w_out_bf, tm=1024, tn=1024)
    x1_s = _outproj(xs, ohg_s, oat_s, w_out_bf, tm=512, tn=1024)

    y_p = _mlp(x1_p, norm_mlp, w_up_bf, w_down_bf, gfin, tm=1024, tf=512)
    y_s = _mlp(x1_s, norm_mlp, w_up_bf, w_down_bf, gfin, tm=512, tf=512)

    return (
        y_p.reshape(n_pb, n_p, D_MODEL),
        y_s.reshape(n_sb, n_s, D_MODEL),
        kf_p.reshape(1, n_pb, n_p, N_HEADS, HEAD_W),
        vf_p.reshape(1, n_pb, n_p, N_HEADS, HEAD_W),
        st_p.reshape(1, n_pb, N_HEADS, HEAD_W, HEAD_W),
        kf_s.reshape(1, n_sb, n_s, N_HEADS, HEAD_W),
        vf_s.reshape(1, n_sb, n_s, N_HEADS, HEAD_W),
        st_s.reshape(1, n_sb, N_HEADS, HEAD_W, HEAD_W),
    )
```

```python
import functools
import math

import jax
import jax.numpy as jnp
from jax import lax
from jax.experimental import pallas as pl
from jax.experimental.pallas import tpu as pltpu

F32 = jnp.float32
BF16 = jnp.bfloat16

D_MODEL = 2048
HG_WIDTH = 1024
ATT_WIDTH = 1024
N_HEADS = 8
HEAD_W = 128
ATT_DH = 64
CHUNK = 64
SUB = 8
N_SUB = CHUNK // SUB
D_FF = 4 * D_MODEL
N_SEG = 7
EPS = 1e-6
NEG_INF = -1e30
LOG2E = 1.4426950408889634
Q_SCALE = ATT_DH ** -0.5 * LOG2E
LAM_INIT = 0.8 - 0.6 * math.exp(-0.3 * 0)

VMEM_LIMIT = 56 * 1024 * 1024


def _rmsnorm_rows(x, gain):
    return x * lax.rsqrt(jnp.mean(x * x, axis=-1, keepdims=True) + EPS) * gain


def _sigmoid(x):
    return 1.0 / (1.0 + jnp.exp(-x))


def _head(h):
    return slice(h * HEAD_W, (h + 1) * HEAD_W)


def _inproj_kernel(x_ref, gain_ref, w_ref, gates_ref, q_ref, kf_ref, vf_ref, kb_ref, vb_ref, h_ref,
                   *, v_transposed):
    j = pl.program_id(1)

    @pl.when(j == 0)
    def _():
        h_ref[...] = _rmsnorm_rows(x_ref[...], gain_ref[...]).astype(BF16)

    acc = jnp.dot(h_ref[...], w_ref[...], preferred_element_type=F32)

    @pl.when(j < 4)
    def _():
        gates_ref[0] = acc

    @pl.when(j == 4)
    def _():
        q_ref[...] = (acc * Q_SCALE).astype(BF16)

    @pl.when(j == 5)
    def _():
        kf_ref[...] = acc
        kb_ref[...] = acc.astype(BF16)

    @pl.when(j == 6)
    def _():
        vf_ref[...] = acc
        vb_ref[...] = (acc.T if v_transposed else acc).astype(BF16)


def _inproj(x, gain, w_bf, tm, v_transposed):
    m = x.shape[0]
    seg = HG_WIDTH
    row = lambda i, j: (i, 0)
    vb_spec = pl.BlockSpec((seg, tm), lambda i, j: (0, i)) if v_transposed else pl.BlockSpec((tm, seg), row)
    return pl.pallas_call(
        functools.partial(_inproj_kernel, v_transposed=v_transposed),
        grid=(m // tm, N_SEG),
        in_specs=[
            pl.BlockSpec((tm, D_MODEL), row),
            pl.BlockSpec((1, D_MODEL), lambda i, j: (0, 0)),
            pl.BlockSpec((D_MODEL, seg), lambda i, j: (0, j)),
        ],
        out_specs=[
            pl.BlockSpec((1, tm, seg), lambda i, j: (jnp.minimum(j, 3), i, 0)),
            pl.BlockSpec((tm, seg), row),
            pl.BlockSpec((tm, seg), row),
            pl.BlockSpec((tm, seg), row),
            pl.BlockSpec((tm, seg), row),
            vb_spec,
        ],
        out_shape=[
            jax.ShapeDtypeStruct((4, m, seg), F32),
            jax.ShapeDtypeStruct((m, seg), BF16),
            jax.ShapeDtypeStruct((m, seg), F32),
            jax.ShapeDtypeStruct((m, seg), F32),
            jax.ShapeDtypeStruct((m, seg), BF16),
            jax.ShapeDtypeStruct((seg, m) if v_transposed else (m, seg), BF16),
        ],
        scratch_shapes=[pltpu.VMEM((tm, D_MODEL), BF16)],
        compiler_params=pltpu.CompilerParams(
            dimension_semantics=("arbitrary", "arbitrary"), vmem_limit_bytes=VMEM_LIMIT),
        name="inproj",
    )(x, gain, w_bf)


def _gla_kernel(hq_ref, hf_ref, hi_ref, hg_ref, lbnd_ref, gn_ref, s0_ref, o_ref, sout_ref, st_ref,
                q_ref, b_ref, *, n_chunks):
    r = pl.program_id(1)

    @pl.when(r == 0)
    def _():
        st_ref[...] = jnp.concatenate([s0_ref[0, h].T for h in range(N_HEADS)], axis=1)

    lbs = lbnd_ref[...]
    e = jnp.exp(lbs - jnp.max(lbs, axis=0, keepdims=True))
    lb = e[0:1, :] / jnp.sum(e, axis=0, keepdims=True)
    gn = jnp.concatenate([gn_ref[...]] * N_HEADS, axis=1)

    ri = lax.broadcasted_iota(jnp.int32, (CHUNK, CHUNK), 0)
    ci = lax.broadcasted_iota(jnp.int32, (CHUNK, CHUNK), 1)
    tril = (ri >= ci).astype(F32)
    pr = lax.broadcasted_iota(jnp.int32, (2 * HEAD_W, 2 * HEAD_W), 0)
    pc = lax.broadcasted_iota(jnp.int32, (2 * HEAD_W, 2 * HEAD_W), 1)
    pair_ones = ((pr >> 7) == (pc >> 7)).astype(BF16)
    gr = lax.broadcasted_iota(jnp.int32, (CHUNK, CHUNK * SUB), 0)
    gc = lax.broadcasted_iota(jnp.int32, (CHUNK, CHUNK * SUB), 1)
    seg_sum = (((gc >> 3) == gr) & ((gc & (SUB - 1)) <= (gr & (SUB - 1)))).astype(BF16)
    n_off = (N_SUB - 1) * SUB
    n_key = SUB * (N_SUB * (N_SUB - 1) // 2)
    orow = lax.broadcasted_iota(jnp.int32, (n_off, n_key), 0) >> 3
    ocol = lax.broadcasted_iota(jnp.int32, (n_off, n_key), 1)
    ocol_seg = sum((ocol >= 4 * i * (i - 1)).astype(jnp.int32) for i in range(2, N_SUB))
    off_mask = orow == ocol_seg
    nt = (((1,), (1,)), ((), ()))
    tn = (((0,), (0,)), ((), ()))

    def chunk(c, carry):
        r0 = pl.multiple_of(c * CHUNK, CHUNK)
        hq = hq_ref[0, pl.ds(r0, CHUNK), :]
        hf = hf_ref[0, pl.ds(r0, CHUNK), :]
        v = hi_ref[0, pl.ds(r0, CHUNK), :]
        hg = hg_ref[0, pl.ds(r0, CHUNK), :]

        f = lb + (1.0 - lb) * _sigmoid(hf)
        g = jnp.log(f) * LOG2E
        kk = 1.0 - f
        q = hq * _sigmoid(hq)
        b = jnp.dot(tril, g, precision=lax.Precision.HIGHEST, preferred_element_type=F32)
        v_bf = v.astype(BF16)
        for h in range(N_HEADS):
            q_ref[h] = q[:, _head(h)]
            b_ref[h] = b[:, _head(h)]

        st = st_ref[...]
        st_bf = st.astype(BF16)
        b_last = b[CHUNK - 1:CHUNK, :]
        q_in = (q * jnp.exp2(b)).astype(BF16)
        k_dec = (kk * jnp.exp2(b_last - b)).astype(BF16)
        o_inter = jnp.concatenate(
            [lax.dot_general(q_in[:, _head(h)], st_bf[:, _head(h)], nt, preferred_element_type=F32)
             for h in range(N_HEADS)], axis=1)
        upd = jnp.concatenate(
            [lax.dot_general(v_bf[:, _head(h)], k_dec[:, _head(h)], tn, preferred_element_type=F32)
             for h in range(N_HEADS)], axis=1)
        st_ref[...] = jnp.exp2(b_last) * st + upd

        qt, kh, vh = [], [], []
        for i in range(1, N_SUB):
            lo = i * SUB
            b_start = b[lo - 1:lo, :]
            qt.append(q[lo:lo + SUB] * jnp.exp2(b[lo:lo + SUB] - b_start))
            kh.append(kk[:lo] * jnp.exp2(b_start - b[:lo]))
            vh.append(v[:lo])
        qt = jnp.concatenate(qt, axis=0).astype(BF16)
        kh = jnp.concatenate(kh, axis=0).astype(BF16)
        vh = jnp.concatenate(vh, axis=0).astype(BF16)
        a_off = [lax.dot_general(qt[:, _head(h)], kh[:, _head(h)], nt, preferred_element_type=F32)
                 for h in range(N_HEADS)]

        a_rep = []
        for j in range(N_HEADS // 2):
            p_pair = []
            for h in (2 * j, 2 * j + 1):
                kk_h = kk[:, _head(h)]
                b_h = b[:, _head(h)]
                rows = []
                for r in range(CHUNK):
                    lo = r - r % SUB
                    q_row = q_ref[h, pl.ds(r, SUB, stride=0), :]
                    b_row = b_ref[h, pl.ds(r, SUB, stride=0), :]
                    rows.append((q_row * kk_h[lo:lo + SUB])
                                * jnp.exp2(jnp.minimum(b_row - b_h[lo:lo + SUB], 0.0)))
                p_pair.append(jnp.concatenate(rows, axis=0).astype(BF16))
            a_rep.append(jnp.dot(jnp.concatenate(p_pair, axis=1), pair_ones,
                                 preferred_element_type=F32))

        o_off = jnp.concatenate(
            [jnp.dot(jnp.where(off_mask, a_off[h], 0.0).astype(BF16), vh[:, _head(h)],
                     preferred_element_type=F32) for h in range(N_HEADS)], axis=1)
        o_diag = []
        for j in range(N_HEADS // 2):
            pair = slice(2 * HEAD_W * j, 2 * HEAD_W * (j + 1))
            v_rep = jnp.broadcast_to(v[:, pair].reshape(N_SUB, 1, SUB, 2 * HEAD_W),
                                     (N_SUB, SUB, SUB, 2 * HEAD_W))
            w = (a_rep[j].reshape(N_SUB, SUB, SUB, 2 * HEAD_W) * v_rep).reshape(CHUNK * SUB, 2 * HEAD_W)
            o_diag.append(jnp.dot(seg_sum, w.astype(BF16), preferred_element_type=F32))
        o_intra = jnp.concatenate(o_diag, axis=1) + jnp.concatenate(
            [jnp.zeros((SUB, HG_WIDTH), F32), o_off], axis=0)
        o = o_inter + o_intra

        y = jnp.concatenate(
            [o[:, _head(h)] * lax.rsqrt(jnp.mean(o[:, _head(h)] * o[:, _head(h)], axis=-1, keepdims=True) + EPS)
             for h in range(N_HEADS)], axis=1)
        y = y * gn * (hg * _sigmoid(hg))
        o_ref[pl.ds(r0, CHUNK), :] = y.astype(BF16)
        return carry

    lax.fori_loop(0, n_chunks, chunk, 0)

    @pl.when(r == pl.num_programs(1) - 1)
    def _():
        st = st_ref[...]
        for h in range(N_HEADS):
            sout_ref[0, h] = st[:, _head(h)].T


def _gla(gates, lower_bounds, hg_norm, s0, n_seq, seq_len, rows):
    m = gates.shape[1]
    nr = seq_len // rows
    seg = lambda s: pl.BlockSpec((1, rows, HG_WIDTH), lambda b, r: (s, b * nr + r, 0))
    state = pl.BlockSpec((1, N_HEADS, HEAD_W, HEAD_W), lambda b, r: (b, 0, 0, 0))
    return pl.pallas_call(
        functools.partial(_gla_kernel, n_chunks=rows // CHUNK),
        grid=(n_seq, nr),
        in_specs=[
            seg(0), seg(1), seg(2), seg(3),
            pl.BlockSpec((lower_bounds.shape[0], HG_WIDTH), lambda b, r: (0, 0)),
            pl.BlockSpec((1, HEAD_W), lambda b, r: (0, 0)),
            state,
        ],
        out_specs=[pl.BlockSpec((rows, HG_WIDTH), lambda b, r: (b * nr + r, 0)), state],
        out_shape=[
            jax.ShapeDtypeStruct((m, HG_WIDTH), BF16),
            jax.ShapeDtypeStruct((n_seq, N_HEADS, HEAD_W, HEAD_W), F32),
        ],
        scratch_shapes=[pltpu.VMEM((HEAD_W, HG_WIDTH), F32),
                        pltpu.VMEM((N_HEADS, CHUNK, HEAD_W), F32),
                        pltpu.VMEM((N_HEADS, CHUNK, HEAD_W), F32)],
        compiler_params=pltpu.CompilerParams(
            dimension_semantics=("arbitrary", "arbitrary"), vmem_limit_bytes=VMEM_LIMIT),
        name="hgrn2",
    )(gates, gates, gates, gates, lower_bounds, hg_norm, s0)


def _stacked_query(q):
    lane = lax.broadcasted_iota(jnp.int32, q.shape, 1)
    qbig = jnp.concatenate([jnp.where(lane < ATT_DH, q, 0.0), jnp.where(lane >= ATT_DH, q, 0.0)], axis=0)
    return qbig.T.astype(BF16)


def _softmax_update(s, v_bf, m_ref, l_ref, acc_ref):
    m_prev = m_ref[...]
    m_new = jnp.maximum(m_prev, jnp.max(s, axis=0, keepdims=True))
    alpha = jnp.exp2(m_prev - m_new)
    p = jnp.exp2(s - m_new)
    l_ref[...] = alpha * l_ref[...] + jnp.sum(p, axis=0, keepdims=True)
    pv = lax.dot_general(v_bf, p.astype(BF16), (((0,), (0,)), ((), ())), preferred_element_type=F32)
    acc_ref[...] = alpha * acc_ref[...] + pv
    m_ref[...] = m_new


def _lambda(lq1_ref, lk1_ref, lq2_ref, lk2_ref):
    s1 = jnp.sum(lq1_ref[...] * lk1_ref[...], axis=-1, keepdims=True)
    s2 = jnp.sum(lq2_ref[...] * lk2_ref[...], axis=-1, keepdims=True)
    return jnp.exp(s1) - jnp.exp(s2) + LAM_INIT


def _attn_finish(acc, l, lam, sub, n):
    o_both = (acc * (1.0 / l)).T
    o = o_both[:n] - lam * o_both[n:]
    return _rmsnorm_rows(o, sub) * (1.0 - LAM_INIT)


def _attn_prompt_kernel(q_ref, k_ref, vt_ref, lq1_ref, lk1_ref, lq2_ref, lk2_ref, sub_ref, o_ref,
                        qq_ref, s_ref, p_ref, a_ref, m_ref, l_ref, acc_ref, *, t):
    qi = pl.program_id(1)
    qq_ref[...] = _stacked_query(q_ref[...].astype(F32))
    m_ref[...] = jnp.full(m_ref.shape, NEG_INF, F32)
    l_ref[...] = jnp.zeros(l_ref.shape, F32)
    acc_ref[...] = jnp.zeros(acc_ref.shape, F32)

    def scores(kv):
        k0 = pl.multiple_of(kv * t, t)
        return jnp.dot(k_ref[pl.ds(k0, t), :], qq_ref[...], preferred_element_type=F32)

    def weighted_values(kv, p):
        k0 = pl.multiple_of(kv * t, t)
        return jnp.dot(vt_ref[:, pl.ds(k0, t)], p, preferred_element_type=F32)

    def softmax(s):
        m_prev = m_ref[...]
        m_new = jnp.maximum(m_prev, jnp.max(s, axis=0, keepdims=True))
        alpha = jnp.exp2(m_prev - m_new)
        p = jnp.exp2(s - m_new)
        l_ref[...] = alpha * l_ref[...] + jnp.sum(p, axis=0, keepdims=True)
        m_ref[...] = m_new
        return p.astype(BF16), alpha

    def stage(kv, cur, oth):
        s_ref[oth] = scores(kv + 1)
        acc_ref[...] = a_ref[oth] * acc_ref[...] + weighted_values(jnp.maximum(kv - 1, 0), p_ref[oth])
        p_ref[cur], a_ref[cur] = softmax(s_ref[cur])

    def finish(cur, oth):
        acc = a_ref[oth] * acc_ref[...] + weighted_values(jnp.maximum(qi - 1, 0), p_ref[oth])
        s = s_ref[cur]
        kpos = qi * t + lax.broadcasted_iota(jnp.int32, s.shape, 0)
        col = lax.broadcasted_iota(jnp.int32, s.shape, 1)
        qpos = qi * t + jnp.where(col >= t, col - t, col)
        p, alpha = softmax(jnp.where((kpos >> 6) <= (qpos >> 6), s, NEG_INF))
        acc = alpha * acc + weighted_values(qi, p)
        lam = _lambda(lq1_ref, lk1_ref, lq2_ref, lk2_ref)
        o_ref[...] = _attn_finish(acc, l_ref[...], lam, sub_ref[...], t).astype(BF16)

    s_ref[0] = scores(0)
    p_ref[1] = jnp.zeros(p_ref.shape[1:], BF16)
    a_ref[1] = jnp.ones(a_ref.shape[1:], F32)

    def pair(j, carry):
        stage(2 * j, 0, 1)
        stage(2 * j + 1, 1, 0)
        return carry

    lax.fori_loop(0, qi // 2, pair, 0)

    @pl.when(qi % 2 == 1)
    def _():
        stage(qi - 1, 0, 1)
        finish(1, 0)

    @pl.when(qi % 2 == 0)
    def _():
        finish(0, 1)


def _attn_prompt(q_bf, k_bf, vt_bf, lams, subln, t):
    n = q_bf.shape[0]
    small = lambda shape: pl.BlockSpec(shape, lambda h, i: (0, 0))
    return pl.pallas_call(
        functools.partial(_attn_prompt_kernel, t=t),
        grid=(N_HEADS, n // t),
        in_specs=[
            pl.BlockSpec((t, HEAD_W), lambda h, i: (i, h)),
            pl.BlockSpec((n, HEAD_W), lambda h, i: (0, h)),
            pl.BlockSpec((HEAD_W, n), lambda h, i: (h, 0)),
            small((1, ATT_DH)), small((1, ATT_DH)), small((1, ATT_DH)), small((1, ATT_DH)),
            small((1, HEAD_W)),
        ],
        out_specs=pl.BlockSpec((t, HEAD_W), lambda h, i: (i, h)),
        out_shape=jax.ShapeDtypeStruct((n, ATT_WIDTH), BF16),
        scratch_shapes=[
            pltpu.VMEM((HEAD_W, 2 * t), BF16),
            pltpu.VMEM((2, t, 2 * t), F32),
            pltpu.VMEM((2, t, 2 * t), BF16),
            pltpu.VMEM((2, 1, 2 * t), F32),
            pltpu.VMEM((1, 2 * t), F32),
            pltpu.VMEM((1, 2 * t), F32),
            pltpu.VMEM((HEAD_W, 2 * t), F32),
        ],
        compiler_params=pltpu.CompilerParams(
            dimension_semantics=("arbitrary", "arbitrary"), vmem_limit_bytes=VMEM_LIMIT),
        name="attn_prompt",
    )(q_bf, k_bf, vt_bf, *lams, subln)


def _attn_sample_kernel(q_ref, kc_ref, vc_ref, kn_ref, vn_ref, lq1_ref, lk1_ref, lq2_ref, lk2_ref,
                        sub_ref, o_ref, qq_ref, m_ref, l_ref, acc_ref, *, n_q, tk):
    t = pl.program_id(1)

    @pl.when(t == 0)
    def _():
        for h in range(N_HEADS):
            qq_ref[h] = _stacked_query(q_ref[:, _head(h)].astype(F32))
        m_ref[...] = jnp.full(m_ref.shape, NEG_INF, F32)
        l_ref[...] = jnp.zeros(l_ref.shape, F32)
        acc_ref[...] = jnp.zeros(acc_ref.shape, F32)

    def update(h, k_bf, v_bf):
        s = jnp.dot(k_bf, qq_ref[h], preferred_element_type=F32)
        _softmax_update(s, v_bf, m_ref.at[h], l_ref.at[h], acc_ref.at[h])

    for h in range(N_HEADS):
        rows = pl.ds(h, tk, stride=N_HEADS)
        update(h, kc_ref[0, rows, :].astype(BF16), vc_ref[0, rows, :].astype(BF16))

    @pl.when(t == pl.num_programs(1) - 1)
    def _():
        lam = _lambda(lq1_ref, lk1_ref, lq2_ref, lk2_ref)
        for h in range(N_HEADS):
            update(h, kn_ref[:, _head(h)], vn_ref[:, _head(h)])
            o_ref[:, _head(h)] = _attn_finish(acc_ref[h], l_ref[h], lam, sub_ref[...], n_q).astype(BF16)


def _attn_sample(q_bf, cache_k, cache_v, kn_bf, vn_bf, lams, subln, n_q, tk):
    n_b = cache_k.shape[0]
    past = cache_k.shape[1] // N_HEADS
    assert past % CHUNK == 0 and n_q <= CHUNK and past % tk == 0
    small = lambda shape: pl.BlockSpec(shape, lambda b, t: (0, 0))
    rows = pl.BlockSpec((n_q, ATT_WIDTH), lambda b, t: (b, 0))
    cache = pl.BlockSpec((1, tk * N_HEADS, HEAD_W), lambda b, t: (b, t, 0))
    return pl.pallas_call(
        functools.partial(_attn_sample_kernel, n_q=n_q, tk=tk),
        grid=(n_b, past // tk),
        in_specs=[
            rows, cache, cache, rows, rows,
            small((1, ATT_DH)), small((1, ATT_DH)), small((1, ATT_DH)), small((1, ATT_DH)),
            small((1, HEAD_W)),
        ],
        out_specs=rows,
        out_shape=jax.ShapeDtypeStruct((n_b * n_q, ATT_WIDTH), BF16),
        scratch_shapes=[
            pltpu.VMEM((N_HEADS, HEAD_W, 2 * n_q), BF16),
            pltpu.VMEM((N_HEADS, 1, 2 * n_q), F32),
            pltpu.VMEM((N_HEADS, 1, 2 * n_q), F32),
            pltpu.VMEM((N_HEADS, HEAD_W, 2 * n_q), F32),
        ],
        compiler_params=pltpu.CompilerParams(
            dimension_semantics=("arbitrary", "arbitrary"), vmem_limit_bytes=VMEM_LIMIT),
        name="attn_sample",
    )(q_bf, cache_k, cache_v, kn_bf, vn_bf, *lams, subln)


def _outproj_kernel(x_ref, a_ref, b_ref, wa_ref, wb_ref, o_ref):
    o_ref[...] = (x_ref[...]
                  + jnp.dot(a_ref[...], wa_ref[...], preferred_element_type=F32)
                  + jnp.dot(b_ref[...], wb_ref[...], preferred_element_type=F32))


def _outproj(x, mix_hg, mix_at, w_bf, tm, tn):
    m = x.shape[0]
    return pl.pallas_call(
        _outproj_kernel,
        grid=(m // tm, D_MODEL // tn),
        in_specs=[
            pl.BlockSpec((tm, tn), lambda i, j: (i, j)),
            pl.BlockSpec((tm, HG_WIDTH), lambda i, j: (i, 0)),
            pl.BlockSpec((tm, ATT_WIDTH), lambda i, j: (i, 0)),
            pl.BlockSpec((HG_WIDTH, tn), lambda i, j: (0, j)),
            pl.BlockSpec((ATT_WIDTH, tn), lambda i, j: (1, j)),
        ],
        out_specs=pl.BlockSpec((tm, tn), lambda i, j: (i, j)),
        out_shape=jax.ShapeDtypeStruct((m, D_MODEL), F32),
        compiler_params=pltpu.CompilerParams(
            dimension_semantics=("arbitrary", "arbitrary"), vmem_limit_bytes=VMEM_LIMIT),
        name="outproj",
    )(x, mix_hg, mix_at, w_bf, w_bf)


def _mlp_kernel(x_ref, gain_ref, wu_ref, wd_ref, gfin_ref, o_ref, h_ref):
    j = pl.program_id(1)

    @pl.when(j == 0)
    def _():
        x = x_ref[...]
        h_ref[...] = _rmsnorm_rows(x, gain_ref[...]).astype(BF16)
        o_ref[...] = x

    u = jnp.dot(h_ref[...], wu_ref[...], preferred_element_type=F32)
    u = jnp.square(jnp.maximum(u, 0.0)).astype(BF16)
    o_ref[...] += jnp.dot(u, wd_ref[...], preferred_element_type=F32)

    @pl.when(j == pl.num_programs(1) - 1)
    def _():
        o_ref[...] = _rmsnorm_rows(o_ref[...], gfin_ref[...])


def _mlp(x, gain, wu_bf, wd_bf, gfin, tm, tf):
    m = x.shape[0]
    return pl.pallas_call(
        _mlp_kernel,
        grid=(m // tm, D_FF // tf),
        in_specs=[
            pl.BlockSpec((tm, D_MODEL), lambda i, j: (i, 0)),
            pl.BlockSpec((1, D_MODEL), lambda i, j: (0, 0)),
            pl.BlockSpec((D_MODEL, tf), lambda i, j: (0, j)),
            pl.BlockSpec((tf, D_MODEL), lambda i, j: (j, 0)),
            pl.BlockSpec((1, D_MODEL), lambda i, j: (0, 0)),
        ],
        out_specs=pl.BlockSpec((tm, D_MODEL), lambda i, j: (i, 0)),
        out_shape=jax.ShapeDtypeStruct((m, D_MODEL), F32),
        scratch_shapes=[pltpu.VMEM((tm, D_MODEL), BF16)],
        compiler_params=pltpu.CompilerParams(
            dimension_semantics=("arbitrary", "arbitrary"), vmem_limit_bytes=VMEM_LIMIT),
        name="mlp",
    )(x, gain, wu_bf, wd_bf, gfin)


def kernel(x_prompt, x_sample, cache_k, cache_v, state_hgrn, norm_attn, w_in, lower_bounds, hg_norm,
           lambda_q1, lambda_k1, lambda_q2, lambda_k2, subln, w_out, norm_mlp, w_up, w_down, norm_final):
    depth = w_in.shape[0]
    assert depth == 1
    n_pb, n_p, _ = x_prompt.shape
    n_sb, n_s, _ = x_sample.shape
    assert n_pb == 1
    past = cache_k.shape[2]

    w_in_bf = w_in[0].astype(BF16)
    w_out_bf = w_out[0].astype(BF16)
    w_up_bf = w_up[0].astype(BF16)
    w_down_bf = w_down[0].astype(BF16)
    lams = (lambda_q1, lambda_k1, lambda_q2, lambda_k2)
    gfin = norm_final.reshape(1, D_MODEL)

    xp = x_prompt.reshape(n_p, D_MODEL)
    xs = x_sample.reshape(n_sb * n_s, D_MODEL)

    gates_p, q_p, kf_p, vf_p, kb_p, vtb_p = _inproj(xp, norm_attn, w_in_bf, tm=512, v_transposed=True)
    gates_s, q_s, kf_s, vf_s, kb_s, vb_s = _inproj(xs, norm_attn, w_in_bf, tm=512, v_transposed=False)

    s0_p = jnp.zeros((1, N_HEADS, HEAD_W, HEAD_W), F32)
    ohg_p, st_p = _gla(gates_p, lower_bounds, hg_norm, s0_p, n_seq=1, seq_len=n_p, rows=512)
    ohg_s, st_s = _gla(gates_s, lower_bounds, hg_norm, state_hgrn[0], n_seq=n_sb, seq_len=n_s, rows=n_s)

    oat_p = _attn_prompt(q_p, kb_p, vtb_p, lams, subln, t=512)
    ck = cache_k[0].reshape(n_sb, past * N_HEADS, HEAD_W)
    cv = cache_v[0].reshape(n_sb, past * N_HEADS, HEAD_W)
    oat_s = _attn_sample(q_s, ck, cv, kb_s, vb_s, lams, subln, n_q=n_s, tk=1024)

    x1_p = _outproj(xp, ohg_p, oat_p, w_out_bf, tm=1024, tn=1024)
    x1_s = _outproj(xs, ohg_s, oat_s, w_out_bf, tm=512, tn=1024)

    y_p = _mlp(x1_p, norm_mlp, w_up_bf, w_down_bf, gfin, tm=1024, tf=512)
    y_s = _mlp(x1_s, norm_mlp, w_up_bf, w_down_bf, gfin, tm=512, tf=512)

    return (
        y_p.reshape(n_pb, n_p, D_MODEL),
        y_s.reshape(n_sb, n_s, D_MODEL),
        kf_p.reshape(1, n_pb, n_p, N_HEADS, HEAD_W),
        vf_p.reshape(1, n_pb, n_p, N_HEADS, HEAD_W),
        st_p.reshape(1, n_pb, N_HEADS, HEAD_W, HEAD_W),
        kf_s.reshape(1, n_sb, n_s, N_HEADS, HEAD_W),
        vf_s.reshape(1, n_sb, n_s, N_HEADS, HEAD_W),
        st_s.reshape(1, n_sb, N_HEADS, HEAD_W, HEAD_W),
    )
```

```python
import functools
import math

import jax
import jax.numpy as jnp
from jax import lax
from jax.experimental import pallas as pl
from jax.experimental.pallas import tpu as pltpu

F32 = jnp.float32
BF16 = jnp.bfloat16

D_MODEL = 2048
HG_WIDTH = 1024
ATT_WIDTH = 1024
N_HEADS = 8
HEAD_W = 128
ATT_DH = 64
CHUNK = 64
SUB = 8
N_SUB = CHUNK // SUB
D_FF = 4 * D_MODEL
N_SEG = 7
EPS = 1e-6
NEG_INF = -1e30
LOG2E = 1.4426950408889634
Q_SCALE = ATT_DH ** -0.5 * LOG2E
LAM_INIT = 0.8 - 0.6 * math.exp(-0.3 * 0)
ONES_ROWS = 16

VMEM_LIMIT = 56 * 1024 * 1024


def _rmsnorm_rows(x, gain):
    return x * lax.rsqrt(jnp.mean(x * x, axis=-1, keepdims=True) + EPS) * gain


def _sigmoid(x):
    return 1.0 / (1.0 + jnp.exp(-x))


def _head(h):
    return slice(h * HEAD_W, (h + 1) * HEAD_W)


def _inproj_kernel(x_ref, gain_ref, w_ref, gates_ref, q_ref, kf_ref, vf_ref, kb_ref, vb_ref, *, v_transposed):
    h = _rmsnorm_rows(x_ref[...], gain_ref[...]).astype(BF16)

    def segment(s):
        return jnp.dot(h, w_ref[:, s * HG_WIDTH:(s + 1) * HG_WIDTH], preferred_element_type=F32)

    for s in range(4):
        gates_ref[s] = segment(s)
    q_ref[...] = (segment(4) * Q_SCALE).astype(BF16)
    k = segment(5)
    kf_ref[...] = k
    kb_ref[...] = k.astype(BF16)
    v = segment(6)
    vf_ref[...] = v
    vb_ref[...] = (v.T if v_transposed else v).astype(BF16)


def _inproj(x, gain, w_bf, tm, v_transposed):
    m = x.shape[0]
    seg = HG_WIDTH
    row = lambda i: (i, 0)
    vb_spec = pl.BlockSpec((seg, tm), lambda i: (0, i)) if v_transposed else pl.BlockSpec((tm, seg), row)
    return pl.pallas_call(
        functools.partial(_inproj_kernel, v_transposed=v_transposed),
        grid=(m // tm,),
        in_specs=[
            pl.BlockSpec((tm, D_MODEL), row),
            pl.BlockSpec((1, D_MODEL), lambda i: (0, 0)),
            pl.BlockSpec((D_MODEL, N_SEG * seg), lambda i: (0, 0), pipeline_mode=pl.Buffered(1)),
        ],
        out_specs=[
            pl.BlockSpec((4, tm, seg), lambda i: (0, i, 0)),
            pl.BlockSpec((tm, seg), row),
            pl.BlockSpec((tm, seg), row),
            pl.BlockSpec((tm, seg), row),
            pl.BlockSpec((tm, seg), row),
            vb_spec,
        ],
        out_shape=[
            jax.ShapeDtypeStruct((4, m, seg), F32),
            jax.ShapeDtypeStruct((m, seg), BF16),
            jax.ShapeDtypeStruct((m, seg), F32),
            jax.ShapeDtypeStruct((m, seg), F32),
            jax.ShapeDtypeStruct((m, seg), BF16),
            jax.ShapeDtypeStruct((seg, m) if v_transposed else (m, seg), BF16),
        ],
        compiler_params=pltpu.CompilerParams(
            dimension_semantics=("arbitrary",), vmem_limit_bytes=VMEM_LIMIT),
        name="inproj",
    )(x, gain, w_bf)


def _gla_kernel(hq_ref, hf_ref, hi_ref, hg_ref, lbnd_ref, gn_ref, s0_ref, o_ref, sout_ref, st_ref,
                q_ref, b_ref, *, n_chunks):
    r = pl.program_id(1)

    @pl.when(r == 0)
    def _():
        st_ref[...] = jnp.concatenate([s0_ref[0, h].T for h in range(N_HEADS)], axis=1)

    lbs = lbnd_ref[...]
    e = jnp.exp(lbs - jnp.max(lbs, axis=0, keepdims=True))
    lb = e[0:1, :] / jnp.sum(e, axis=0, keepdims=True)
    gn = jnp.concatenate([gn_ref[...]] * N_HEADS, axis=1)

    ri = lax.broadcasted_iota(jnp.int32, (CHUNK, CHUNK), 0)
    ci = lax.broadcasted_iota(jnp.int32, (CHUNK, CHUNK), 1)
    tril = (ri >= ci).astype(F32)
    pr = lax.broadcasted_iota(jnp.int32, (2 * HEAD_W, 2 * HEAD_W), 0)
    pc = lax.broadcasted_iota(jnp.int32, (2 * HEAD_W, 2 * HEAD_W), 1)
    pair_ones = ((pr >> 7) == (pc >> 7)).astype(BF16)
    gr = lax.broadcasted_iota(jnp.int32, (CHUNK, CHUNK * SUB), 0)
    gc = lax.broadcasted_iota(jnp.int32, (CHUNK, CHUNK * SUB), 1)
    seg_sum = (((gc >> 3) == gr) & ((gc & (SUB - 1)) <= (gr & (SUB - 1)))).astype(BF16)
    n_off = (N_SUB - 1) * SUB
    n_key = SUB * (N_SUB * (N_SUB - 1) // 2)
    orow = lax.broadcasted_iota(jnp.int32, (n_off, n_key), 0) >> 3
    ocol = lax.broadcasted_iota(jnp.int32, (n_off, n_key), 1)
    ocol_seg = sum((ocol >= 4 * i * (i - 1)).astype(jnp.int32) for i in range(2, N_SUB))
    off_mask = orow == ocol_seg
    nt = (((1,), (1,)), ((), ()))
    tn = (((0,), (0,)), ((), ()))

    def chunk(c, carry):
        r0 = pl.multiple_of(c * CHUNK, CHUNK)
        hq = hq_ref[0, pl.ds(r0, CHUNK), :]
        hf = hf_ref[0, pl.ds(r0, CHUNK), :]
        v = hi_ref[0, pl.ds(r0, CHUNK), :]
        hg = hg_ref[0, pl.ds(r0, CHUNK), :]

        f = lb + (1.0 - lb) * _sigmoid(hf)
        g = jnp.log(f) * LOG2E
        kk = 1.0 - f
        q = hq * _sigmoid(hq)
        b = jnp.dot(tril, g, precision=lax.Precision.HIGHEST, preferred_element_type=F32)
        v_bf = v.astype(BF16)
        for h in range(N_HEADS):
            q_ref[h] = q[:, _head(h)]
            b_ref[h] = b[:, _head(h)]

        st = st_ref[...]
        st_bf = st.astype(BF16)
        b_last = b[CHUNK - 1:CHUNK, :]
        q_in = (q * jnp.exp2(b)).astype(BF16)
        k_dec = (kk * jnp.exp2(b_last - b)).astype(BF16)
        o_inter = jnp.concatenate(
            [lax.dot_general(q_in[:, _head(h)], st_bf[:, _head(h)], nt, preferred_element_type=F32)
             for h in range(N_HEADS)], axis=1)
        upd = jnp.concatenate(
            [lax.dot_general(v_bf[:, _head(h)], k_dec[:, _head(h)], tn, preferred_element_type=F32)
             for h in range(N_HEADS)], axis=1)
        st_ref[...] = jnp.exp2(b_last) * st + upd

        qt, kh, vh = [], [], []
        for i in range(1, N_SUB):
            lo = i * SUB
            b_start = b[lo - 1:lo, :]
            qt.append(q[lo:lo + SUB] * jnp.exp2(b[lo:lo + SUB] - b_start))
            kh.append(kk[:lo] * jnp.exp2(b_start - b[:lo]))
            vh.append(v[:lo])
        qt = jnp.concatenate(qt, axis=0).astype(BF16)
        kh = jnp.concatenate(kh, axis=0).astype(BF16)
        vh = jnp.concatenate(vh, axis=0).astype(BF16)
        a_off = [lax.dot_general(qt[:, _head(h)], kh[:, _head(h)], nt, preferred_element_type=F32)
                 for h in range(N_HEADS)]

        a_rep = []
        for j in range(N_HEADS // 2):
            p_pair = []
            for h in (2 * j, 2 * j + 1):
                kk_h = kk[:, _head(h)]
                b_h = b[:, _head(h)]
                rows = []
                for r in range(CHUNK):
                    lo = r - r % SUB
                    q_row = q_ref[h, pl.ds(r, SUB, stride=0), :]
                    b_row = b_ref[h, pl.ds(r, SUB, stride=0), :]
                    rows.append((q_row * kk_h[lo:lo + SUB])
                                * jnp.exp2(jnp.minimum(b_row - b_h[lo:lo + SUB], 0.0)))
                p_pair.append(jnp.concatenate(rows, axis=0).astype(BF16))
            a_rep.append(jnp.dot(jnp.concatenate(p_pair, axis=1), pair_ones,
                                 preferred_element_type=F32))

        o_off = jnp.concatenate(
            [jnp.dot(jnp.where(off_mask, a_off[h], 0.0).astype(BF16), vh[:, _head(h)],
                     preferred_element_type=F32) for h in range(N_HEADS)], axis=1)
        o_diag = []
        for j in range(N_HEADS // 2):
            pair = slice(2 * HEAD_W * j, 2 * HEAD_W * (j + 1))
            v_rep = jnp.broadcast_to(v[:, pair].reshape(N_SUB, 1, SUB, 2 * HEAD_W),
                                     (N_SUB, SUB, SUB, 2 * HEAD_W))
            w = (a_rep[j].reshape(N_SUB, SUB, SUB, 2 * HEAD_W) * v_rep).reshape(CHUNK * SUB, 2 * HEAD_W)
            o_diag.append(jnp.dot(seg_sum, w.astype(BF16), preferred_element_type=F32))
        o_intra = jnp.concatenate(o_diag, axis=1) + jnp.concatenate(
            [jnp.zeros((SUB, HG_WIDTH), F32), o_off], axis=0)
        o = o_inter + o_intra

        y = jnp.concatenate(
            [o[:, _head(h)] * lax.rsqrt(jnp.mean(o[:, _head(h)] * o[:, _head(h)], axis=-1, keepdims=True) + EPS)
             for h in range(N_HEADS)], axis=1)
        y = y * gn * (hg * _sigmoid(hg))
        o_ref[pl.ds(r0, CHUNK), :] = y.astype(BF16)
        return carry

    lax.fori_loop(0, n_chunks, chunk, 0)

    @pl.when(r == pl.num_programs(1) - 1)
    def _():
        st = st_ref[...]
        for h in range(N_HEADS):
            sout_ref[0, h] = st[:, _head(h)].T


def _gla(gates, lower_bounds, hg_norm, s0, n_seq, seq_len, rows):
    m = gates.shape[1]
    nr = seq_len // rows
    seg = lambda s: pl.BlockSpec((1, rows, HG_WIDTH), lambda b, r: (s, b * nr + r, 0))
    state = pl.BlockSpec((1, N_HEADS, HEAD_W, HEAD_W), lambda b, r: (b, 0, 0, 0))
    return pl.pallas_call(
        functools.partial(_gla_kernel, n_chunks=rows // CHUNK),
        grid=(n_seq, nr),
        in_specs=[
            seg(0), seg(1), seg(2), seg(3),
            pl.BlockSpec((lower_bounds.shape[0], HG_WIDTH), lambda b, r: (0, 0)),
            pl.BlockSpec((1, HEAD_W), lambda b, r: (0, 0)),
            state,
        ],
        out_specs=[pl.BlockSpec((rows, HG_WIDTH), lambda b, r: (b * nr + r, 0)), state],
        out_shape=[
            jax.ShapeDtypeStruct((m, HG_WIDTH), BF16),
            jax.ShapeDtypeStruct((n_seq, N_HEADS, HEAD_W, HEAD_W), F32),
        ],
        scratch_shapes=[pltpu.VMEM((HEAD_W, HG_WIDTH), F32),
                        pltpu.VMEM((N_HEADS, CHUNK, HEAD_W), F32),
                        pltpu.VMEM((N_HEADS, CHUNK, HEAD_W), F32)],
        compiler_params=pltpu.CompilerParams(
            dimension_semantics=("arbitrary", "arbitrary"), vmem_limit_bytes=VMEM_LIMIT),
        name="hgrn2",
    )(gates, gates, gates, gates, lower_bounds, hg_norm, s0)


def _stacked_query(q):
    lane = lax.broadcasted_iota(jnp.int32, q.shape, 1)
    qbig = jnp.concatenate([jnp.where(lane < ATT_DH, q, 0.0), jnp.where(lane >= ATT_DH, q, 0.0)], axis=0)
    return qbig.T.astype(BF16)


def _softmax_update(s, v_bf, m_ref, l_ref, acc_ref):
    m_prev = m_ref[...]
    m_new = jnp.maximum(m_prev, jnp.max(s, axis=0, keepdims=True))
    alpha = jnp.exp2(m_prev - m_new)
    p = jnp.exp2(s - m_new)
    l_ref[...] = alpha * l_ref[...] + jnp.sum(p, axis=0, keepdims=True)
    pv = lax.dot_general(v_bf, p.astype(BF16), (((0,), (0,)), ((), ())), preferred_element_type=F32)
    acc_ref[...] = alpha * acc_ref[...] + pv
    m_ref[...] = m_new


def _lambda(lq1_ref, lk1_ref, lq2_ref, lk2_ref):
    s1 = jnp.sum(lq1_ref[...] * lk1_ref[...], axis=-1, keepdims=True)
    s2 = jnp.sum(lq2_ref[...] * lk2_ref[...], axis=-1, keepdims=True)
    return jnp.exp(s1) - jnp.exp(s2) + LAM_INIT


def _attn_finish(acc, l, lam, sub, n):
    o_both = (acc * (1.0 / l)).T
    o = o_both[:n] - lam * o_both[n:]
    return _rmsnorm_rows(o, sub) * (1.0 - LAM_INIT)


def _attn_prompt_kernel(q_ref, k_ref, vt_ref, lq1_ref, lk1_ref, lq2_ref, lk2_ref, sub_ref, o_ref,
                        qq_ref, s_ref, smax_ref, p_ref, a_ref, m_ref, acc_ref, *, t):
    qi = pl.program_id(1)
    qq_ref[...] = _stacked_query(q_ref[...].astype(F32))
    m_ref[...] = jnp.full(m_ref.shape, NEG_INF, F32)
    acc_ref[...] = jnp.zeros(acc_ref.shape, F32)
    ones_rows = jnp.ones((ONES_ROWS, t), BF16)

    def scores(kv):
        k0 = pl.multiple_of(kv * t, t)
        return jnp.dot(k_ref[pl.ds(k0, t), :], qq_ref[...], preferred_element_type=F32)

    def weighted_values(kv, p):
        k0 = pl.multiple_of(kv * t, t)
        vt = jnp.concatenate([vt_ref[:, pl.ds(k0, t)], ones_rows], axis=0)
        return jnp.dot(vt, p, preferred_element_type=F32)

    def softmax(s, s_max):
        m_prev = m_ref[...]
        m_new = jnp.maximum(m_prev, s_max)
        m_ref[...] = m_new
        return jnp.exp2(s - m_new).astype(BF16), jnp.exp2(m_prev - m_new)

    def stage(kv, cur, oth):
        s_next = scores(kv + 1)
        s_ref[oth] = s_next
        smax_ref[oth] = jnp.max(s_next, axis=0, keepdims=True)
        acc_ref[...] = a_ref[oth] * acc_ref[...] + weighted_values(jnp.maximum(kv - 1, 0), p_ref[oth])
        p_ref[cur], a_ref[cur] = softmax(s_ref[cur], smax_ref[cur])

    def finish(cur, oth):
        acc = a_ref[oth] * acc_ref[...] + weighted_values(jnp.maximum(qi - 1, 0), p_ref[oth])
        s = s_ref[cur]
        kpos = qi * t + lax.broadcasted_iota(jnp.int32, s.shape, 0)
        col = lax.broadcasted_iota(jnp.int32, s.shape, 1)
        qpos = qi * t + jnp.where(col >= t, col - t, col)
        s = jnp.where((kpos >> 6) <= (qpos >> 6), s, NEG_INF)
        p, alpha = softmax(s, jnp.max(s, axis=0, keepdims=True))
        acc = alpha * acc + weighted_values(qi, p)
        lam = _lambda(lq1_ref, lk1_ref, lq2_ref, lk2_ref)
        o_ref[...] = _attn_finish(acc[:HEAD_W], acc[HEAD_W:HEAD_W + 1], lam, sub_ref[...], t).astype(BF16)

    s_first = scores(0)
    s_ref[0] = s_first
    smax_ref[0] = jnp.max(s_first, axis=0, keepdims=True)
    p_ref[1] = jnp.zeros(p_ref.shape[1:], BF16)
    a_ref[1] = jnp.ones(a_ref.shape[1:], F32)

    def pair(j, carry):
        stage(2 * j, 0, 1)
        stage(2 * j + 1, 1, 0)
        return carry

    lax.fori_loop(0, qi // 2, pair, 0)

    @pl.when(qi % 2 == 1)
    def _():
        stage(qi - 1, 0, 1)
        finish(1, 0)

    @pl.when(qi % 2 == 0)
    def _():
        finish(0, 1)


def _attn_prompt(q_bf, k_bf, vt_bf, lams, subln, t):
    n = q_bf.shape[0]
    small = lambda shape: pl.BlockSpec(shape, lambda h, i: (0, 0))
    return pl.pallas_call(
        functools.partial(_attn_prompt_kernel, t=t),
        grid=(N_HEADS, n // t),
        in_specs=[
            pl.BlockSpec((t, HEAD_W), lambda h, i: (i, h)),
            pl.BlockSpec((n, HEAD_W), lambda h, i: (0, h)),
            pl.BlockSpec((HEAD_W, n), lambda h, i: (h, 0)),
            small((1, ATT_DH)), small((1, ATT_DH)), small((1, ATT_DH)), small((1, ATT_DH)),
            small((1, HEAD_W)),
        ],
        out_specs=pl.BlockSpec((t, HEAD_W), lambda h, i: (i, h)),
        out_shape=jax.ShapeDtypeStruct((n, ATT_WIDTH), BF16),
        scratch_shapes=[
            pltpu.VMEM((HEAD_W, 2 * t), BF16),
            pltpu.VMEM((2, t, 2 * t), F32),
            pltpu.VMEM((2, 1, 2 * t), F32),
            pltpu.VMEM((2, t, 2 * t), BF16),
            pltpu.VMEM((2, 1, 2 * t), F32),
            pltpu.VMEM((1, 2 * t), F32),
            pltpu.VMEM((HEAD_W + ONES_ROWS, 2 * t), F32),
        ],
        compiler_params=pltpu.CompilerParams(
            dimension_semantics=("arbitrary", "arbitrary"), vmem_limit_bytes=VMEM_LIMIT),
        name="attn_prompt",
    )(q_bf, k_bf, vt_bf, *lams, subln)


def _attn_sample_kernel(q_ref, kc_ref, vc_ref, kn_ref, vn_ref, lq1_ref, lk1_ref, lq2_ref, lk2_ref,
                        sub_ref, o_ref, qq_ref, m_ref, l_ref, acc_ref, *, n_q, tk):
    t = pl.program_id(1)

    @pl.when(t == 0)
    def _():
        for h in range(N_HEADS):
            qq_ref[h] = _stacked_query(q_ref[:, _head(h)].astype(F32))
        m_ref[...] = jnp.full(m_ref.shape, NEG_INF, F32)
        l_ref[...] = jnp.zeros(l_ref.shape, F32)
        acc_ref[...] = jnp.zeros(acc_ref.shape, F32)

    def update(h, k_bf, v_bf):
        s = jnp.dot(k_bf, qq_ref[h], preferred_element_type=F32)
        _softmax_update(s, v_bf, m_ref.at[h], l_ref.at[h], acc_ref.at[h])

    for h in range(N_HEADS):
        rows = pl.ds(h, tk, stride=N_HEADS)
        update(h, kc_ref[0, rows, :].astype(BF16), vc_ref[0, rows, :].astype(BF16))

    @pl.when(t == pl.num_programs(1) - 1)
    def _():
        lam = _lambda(lq1_ref, lk1_ref, lq2_ref, lk2_ref)
        for h in range(N_HEADS):
            update(h, kn_ref[:, _head(h)], vn_ref[:, _head(h)])
            o_ref[:, _head(h)] = _attn_finish(acc_ref[h], l_ref[h], lam, sub_ref[...], n_q).astype(BF16)


def _attn_sample(q_bf, cache_k, cache_v, kn_bf, vn_bf, lams, subln, n_q, tk):
    n_b = cache_k.shape[0]
    past = cache_k.shape[1] // N_HEADS
    assert past % CHUNK == 0 and n_q <= CHUNK and past % tk == 0
    small = lambda shape: pl.BlockSpec(shape, lambda b, t: (0, 0))
    rows = pl.BlockSpec((n_q, ATT_WIDTH), lambda b, t: (b, 0))
    cache = pl.BlockSpec((1, tk * N_HEADS, HEAD_W), lambda b, t: (b, t, 0))
    return pl.pallas_call(
        functools.partial(_attn_sample_kernel, n_q=n_q, tk=tk),
        grid=(n_b, past // tk),
        in_specs=[
            rows, cache, cache, rows, rows,
            small((1, ATT_DH)), small((1, ATT_DH)), small((1, ATT_DH)), small((1, ATT_DH)),
            small((1, HEAD_W)),
        ],
        out_specs=rows,
        out_shape=jax.ShapeDtypeStruct((n_b * n_q, ATT_WIDTH), BF16),
        scratch_shapes=[
            pltpu.VMEM((N_HEADS, HEAD_W, 2 * n_q), BF16),
            pltpu.VMEM((N_HEADS, 1, 2 * n_q), F32),
            pltpu.VMEM((N_HEADS, 1, 2 * n_q), F32),
            pltpu.VMEM((N_HEADS, HEAD_W, 2 * n_q), F32),
        ],
        compiler_params=pltpu.CompilerParams(
            dimension_semantics=("arbitrary", "arbitrary"), vmem_limit_bytes=VMEM_LIMIT),
        name="attn_sample",
    )(q_bf, cache_k, cache_v, kn_bf, vn_bf, *lams, subln)


def _outproj_kernel(x_ref, a_ref, b_ref, w_ref, o_ref):
    o_ref[...] = (x_ref[...]
                  + jnp.dot(a_ref[...], w_ref[:HG_WIDTH, :], preferred_element_type=F32)
                  + jnp.dot(b_ref[...], w_ref[HG_WIDTH:, :], preferred_element_type=F32))


def _outproj(x, mix_hg, mix_at, w_bf, tm):
    m = x.shape[0]
    row = lambda i: (i, 0)
    return pl.pallas_call(
        _outproj_kernel,
        grid=(m // tm,),
        in_specs=[
            pl.BlockSpec((tm, D_MODEL), row),
            pl.BlockSpec((tm, HG_WIDTH), row),
            pl.BlockSpec((tm, ATT_WIDTH), row),
            pl.BlockSpec((HG_WIDTH + ATT_WIDTH, D_MODEL), lambda i: (0, 0), pipeline_mode=pl.Buffered(1)),
        ],
        out_specs=pl.BlockSpec((tm, D_MODEL), row),
        out_shape=jax.ShapeDtypeStruct((m, D_MODEL), F32),
        compiler_params=pltpu.CompilerParams(
            dimension_semantics=("arbitrary",), vmem_limit_bytes=VMEM_LIMIT),
        name="outproj",
    )(x, mix_hg, mix_at, w_bf)


def _mlp_kernel(x_ref, gain_ref, wu_ref, wd_ref, gfin_ref, o_ref, h_ref):
    j = pl.program_id(1)

    @pl.when(j == 0)
    def _():
        x = x_ref[...]
        h_ref[...] = _rmsnorm_rows(x, gain_ref[...]).astype(BF16)
        o_ref[...] = x

    u = jnp.dot(h_ref[...], wu_ref[...], preferred_element_type=F32)
    u = jnp.square(jnp.maximum(u, 0.0)).astype(BF16)
    o_ref[...] += jnp.dot(u, wd_ref[...], preferred_element_type=F32)

    @pl.when(j == pl.num_programs(1) - 1)
    def _():
        o_ref[...] = _rmsnorm_rows(o_ref[...], gfin_ref[...])


def _mlp(x, gain, wu_bf, wd_bf, gfin, tm, tf):
    m = x.shape[0]
    return pl.pallas_call(
        _mlp_kernel,
        grid=(m // tm, D_FF // tf),
        in_specs=[
            pl.BlockSpec((tm, D_MODEL), lambda i, j: (i, 0)),
            pl.BlockSpec((1, D_MODEL), lambda i, j: (0, 0)),
            pl.BlockSpec((D_MODEL, tf), lambda i, j: (0, j)),
            pl.BlockSpec((tf, D_MODEL), lambda i, j: (j, 0)),
            pl.BlockSpec((1, D_MODEL), lambda i, j: (0, 0)),
        ],
        out_specs=pl.BlockSpec((tm, D_MODEL), lambda i, j: (i, 0)),
        out_shape=jax.ShapeDtypeStruct((m, D_MODEL), F32),
        scratch_shapes=[pltpu.VMEM((tm, D_MODEL), BF16)],
        compiler_params=pltpu.CompilerParams(
            dimension_semantics=("arbitrary", "arbitrary"), vmem_limit_bytes=VMEM_LIMIT),
        name="mlp",
    )(x, gain, wu_bf, wd_bf, gfin)


def kernel(x_prompt, x_sample, cache_k, cache_v, state_hgrn, norm_attn, w_in, lower_bounds, hg_norm,
           lambda_q1, lambda_k1, lambda_q2, lambda_k2, subln, w_out, norm_mlp, w_up, w_down, norm_final):
    depth = w_in.shape[0]
    assert depth == 1
    n_pb, n_p, _ = x_prompt.shape
    n_sb, n_s, _ = x_sample.shape
    assert n_pb == 1
    past = cache_k.shape[2]

    w_in_bf = w_in[0].astype(BF16)
    w_out_bf = w_out[0].astype(BF16)
    w_up_bf = w_up[0].astype(BF16)
    w_down_bf = w_down[0].astype(BF16)
    lams = (lambda_q1, lambda_k1, lambda_q2, lambda_k2)
    gfin = norm_final.reshape(1, D_MODEL)

    xp = x_prompt.reshape(n_p, D_MODEL)
    xs = x_sample.reshape(n_sb * n_s, D_MODEL)

    gates_p, q_p, kf_p, vf_p, kb_p, vtb_p = _inproj(xp, norm_attn, w_in_bf, tm=256, v_transposed=True)
    gates_s, q_s, kf_s, vf_s, kb_s, vb_s = _inproj(xs, norm_attn, w_in_bf, tm=256, v_transposed=False)

    s0_p = jnp.zeros((1, N_HEADS, HEAD_W, HEAD_W), F32)
    ohg_p, st_p = _gla(gates_p, lower_bounds, hg_norm, s0_p, n_seq=1, seq_len=n_p, rows=512)
    ohg_s, st_s = _gla(gates_s, lower_bounds, hg_norm, state_hgrn[0], n_seq=n_sb, seq_len=n_s, rows=n_s)

    oat_p = _attn_prompt(q_p, kb_p, vtb_p, lams, subln, t=512)
    ck = cache_k[0].reshape(n_sb, past * N_HEADS, HEAD_W)
    cv = cache_v[0].reshape(n_sb, past * N_HEADS, HEAD_W)
    oat_s = _attn_sample(q_s, ck, cv, kb_s, vb_s, lams, subln, n_q=n_s, tk=1024)

    x1_p = _outproj(xp, ohg_p, oat_p, w_out_bf, tm=512)
    x1_s = _outproj(xs, ohg_s, oat_s, w_out_bf, tm=512)

    y_p = _mlp(x1_p, norm_mlp, w_up_bf, w_down_bf, gfin, tm=1024, tf=512)
    y_s = _mlp(x1_s, norm_mlp, w_up_bf, w_down_bf, gfin, tm=512, tf=512)

    return (
        y_p.reshape(n_pb, n_p, D_MODEL),
        y_s.reshape(n_sb, n_s, D_MODEL),
        kf_p.reshape(1, n_pb, n_p, N_HEADS, HEAD_W),
        vf_p.reshape(1, n_pb, n_p, N_HEADS, HEAD_W),
        st_p.reshape(1, n_pb, N_HEADS, HEAD_W, HEAD_W),
        kf_s.reshape(1, n_sb, n_s, N_HEADS, HEAD_W),
        vf_s.reshape(1, n_sb, n_s, N_HEADS, HEAD_W),
        st_s.reshape(1, n_sb, N_HEADS, HEAD_W, HEAD_W),
    )
```

```python
import functools
import math

import jax
import jax.numpy as jnp
from jax import lax
from jax.experimental import pallas as pl
from jax.experimental.pallas import tpu as pltpu

F32 = jnp.float32
BF16 = jnp.bfloat16

D_MODEL = 2048
HG_WIDTH = 1024
ATT_WIDTH = 1024
N_HEADS = 8
HEAD_W = 128
ATT_DH = 64
CHUNK = 64
SUB = 8
N_SUB = CHUNK // SUB
D_FF = 4 * D_MODEL
N_SEG = 7
EPS = 1e-6
NEG_INF = -1e30
LOG2E = 1.4426950408889634
Q_SCALE = ATT_DH ** -0.5 * LOG2E
LAM_INIT = 0.8 - 0.6 * math.exp(-0.3 * 0)
ONES_ROWS = 16
FINISH_COLS = 256

VMEM_LIMIT = 56 * 1024 * 1024


def _rmsnorm_rows(x, gain):
    return x * lax.rsqrt(jnp.mean(x * x, axis=-1, keepdims=True) + EPS) * gain


def _sigmoid(x):
    return 1.0 / (1.0 + jnp.exp(-x))


def _head(h):
    return slice(h * HEAD_W, (h + 1) * HEAD_W)


def _inproj_kernel(x_ref, gain_ref, w_ref, gates_ref, q_ref, kf_ref, vf_ref, kb_ref, vb_ref, *, v_transposed):
    h = _rmsnorm_rows(x_ref[...], gain_ref[...]).astype(BF16)

    def segment(s):
        return jnp.dot(h, w_ref[:, s * HG_WIDTH:(s + 1) * HG_WIDTH], preferred_element_type=F32)

    for s in range(4):
        gates_ref[s] = segment(s)
    q_ref[...] = (segment(4) * Q_SCALE).astype(BF16)
    k = segment(5)
    kf_ref[...] = k
    kb_ref[...] = k.astype(BF16)
    v = segment(6)
    vf_ref[...] = v
    vb_ref[...] = (v.T if v_transposed else v).astype(BF16)


def _inproj(x, gain, w_bf, tm, v_transposed):
    m = x.shape[0]
    seg = HG_WIDTH
    row = lambda i: (i, 0)
    vb_spec = pl.BlockSpec((seg, tm), lambda i: (0, i)) if v_transposed else pl.BlockSpec((tm, seg), row)
    return pl.pallas_call(
        functools.partial(_inproj_kernel, v_transposed=v_transposed),
        grid=(m // tm,),
        in_specs=[
            pl.BlockSpec((tm, D_MODEL), row),
            pl.BlockSpec((1, D_MODEL), lambda i: (0, 0)),
            pl.BlockSpec((D_MODEL, N_SEG * seg), lambda i: (0, 0), pipeline_mode=pl.Buffered(1)),
        ],
        out_specs=[
            pl.BlockSpec((4, tm, seg), lambda i: (0, i, 0)),
            pl.BlockSpec((tm, seg), row),
            pl.BlockSpec((tm, seg), row),
            pl.BlockSpec((tm, seg), row),
            pl.BlockSpec((tm, seg), row),
            vb_spec,
        ],
        out_shape=[
            jax.ShapeDtypeStruct((4, m, seg), F32),
            jax.ShapeDtypeStruct((m, seg), BF16),
            jax.ShapeDtypeStruct((m, seg), F32),
            jax.ShapeDtypeStruct((m, seg), F32),
            jax.ShapeDtypeStruct((m, seg), BF16),
            jax.ShapeDtypeStruct((seg, m) if v_transposed else (m, seg), BF16),
        ],
        compiler_params=pltpu.CompilerParams(
            dimension_semantics=("arbitrary",), vmem_limit_bytes=VMEM_LIMIT),
        name="inproj",
    )(x, gain, w_bf)


def _gla_kernel(hq_ref, hf_ref, hi_ref, hg_ref, lbnd_ref, gn_ref, s0_ref, o_ref, sout_ref, st_ref,
                q_ref, b_ref, *, n_chunks):
    r = pl.program_id(1)

    @pl.when(r == 0)
    def _():
        st_ref[...] = jnp.concatenate([s0_ref[0, h].T for h in range(N_HEADS)], axis=1)

    lbs = lbnd_ref[...]
    e = jnp.exp(lbs - jnp.max(lbs, axis=0, keepdims=True))
    lb = e[0:1, :] / jnp.sum(e, axis=0, keepdims=True)
    gn = jnp.concatenate([gn_ref[...]] * N_HEADS, axis=1)

    ri = lax.broadcasted_iota(jnp.int32, (CHUNK, CHUNK), 0)
    ci = lax.broadcasted_iota(jnp.int32, (CHUNK, CHUNK), 1)
    tril = (ri >= ci).astype(F32)
    pr = lax.broadcasted_iota(jnp.int32, (2 * HEAD_W, 2 * HEAD_W), 0)
    pc = lax.broadcasted_iota(jnp.int32, (2 * HEAD_W, 2 * HEAD_W), 1)
    pair_ones = ((pr >> 7) == (pc >> 7)).astype(BF16)
    gr = lax.broadcasted_iota(jnp.int32, (CHUNK, CHUNK * SUB), 0)
    gc = lax.broadcasted_iota(jnp.int32, (CHUNK, CHUNK * SUB), 1)
    seg_sum = (((gc >> 3) == gr) & ((gc & (SUB - 1)) <= (gr & (SUB - 1)))).astype(BF16)
    n_off = (N_SUB - 1) * SUB
    n_key = SUB * (N_SUB * (N_SUB - 1) // 2)
    orow = lax.broadcasted_iota(jnp.int32, (n_off, n_key), 0) >> 3
    ocol = lax.broadcasted_iota(jnp.int32, (n_off, n_key), 1)
    ocol_seg = sum((ocol >= 4 * i * (i - 1)).astype(jnp.int32) for i in range(2, N_SUB))
    off_mask = orow == ocol_seg
    nt = (((1,), (1,)), ((), ()))
    tn = (((0,), (0,)), ((), ()))

    def chunk(c, carry):
        r0 = pl.multiple_of(c * CHUNK, CHUNK)
        hq = hq_ref[0, pl.ds(r0, CHUNK), :]
        hf = hf_ref[0, pl.ds(r0, CHUNK), :]
        v = hi_ref[0, pl.ds(r0, CHUNK), :]
        hg = hg_ref[0, pl.ds(r0, CHUNK), :]

        f = lb + (1.0 - lb) * _sigmoid(hf)
        g = jnp.log(f) * LOG2E
        kk = 1.0 - f
        q = hq * _sigmoid(hq)
        b = jnp.dot(tril, g, precision=lax.Precision.HIGHEST, preferred_element_type=F32)
        v_bf = v.astype(BF16)
        for h in range(N_HEADS):
            q_ref[h] = q[:, _head(h)]
            b_ref[h] = b[:, _head(h)]

        st = st_ref[...]
        st_bf = st.astype(BF16)
        b_last = b[CHUNK - 1:CHUNK, :]
        q_in = (q * jnp.exp2(b)).astype(BF16)
        k_dec = (kk * jnp.exp2(b_last - b)).astype(BF16)
        o_inter = jnp.concatenate(
            [lax.dot_general(q_in[:, _head(h)], st_bf[:, _head(h)], nt, preferred_element_type=F32)
             for h in range(N_HEADS)], axis=1)
        upd = jnp.concatenate(
            [lax.dot_general(v_bf[:, _head(h)], k_dec[:, _head(h)], tn, preferred_element_type=F32)
             for h in range(N_HEADS)], axis=1)
        st_ref[...] = jnp.exp2(b_last) * st + upd

        qt, kh, vh = [], [], []
        for i in range(1, N_SUB):
            lo = i * SUB
            b_start = b[lo - 1:lo, :]
            qt.append(q[lo:lo + SUB] * jnp.exp2(b[lo:lo + SUB] - b_start))
            kh.append(kk[:lo] * jnp.exp2(b_start - b[:lo]))
            vh.append(v[:lo])
        qt = jnp.concatenate(qt, axis=0).astype(BF16)
        kh = jnp.concatenate(kh, axis=0).astype(BF16)
        vh = jnp.concatenate(vh, axis=0).astype(BF16)
        a_off = [lax.dot_general(qt[:, _head(h)], kh[:, _head(h)], nt, preferred_element_type=F32)
                 for h in range(N_HEADS)]

        a_rep = []
        for j in range(N_HEADS // 2):
            p_pair = []
            for h in (2 * j, 2 * j + 1):
                kk_h = kk[:, _head(h)]
                b_h = b[:, _head(h)]
                rows = []
                for r in range(CHUNK):
                    lo = r - r % SUB
                    q_row = q_ref[h, pl.ds(r, SUB, stride=0), :]
                    b_row = b_ref[h, pl.ds(r, SUB, stride=0), :]
                    rows.append((q_row * kk_h[lo:lo + SUB])
                                * jnp.exp2(jnp.minimum(b_row - b_h[lo:lo + SUB], 0.0)))
                p_pair.append(jnp.concatenate(rows, axis=0).astype(BF16))
            a_rep.append(jnp.dot(jnp.concatenate(p_pair, axis=1), pair_ones,
                                 preferred_element_type=F32))

        o_off = jnp.concatenate(
            [jnp.dot(jnp.where(off_mask, a_off[h], 0.0).astype(BF16), vh[:, _head(h)],
                     preferred_element_type=F32) for h in range(N_HEADS)], axis=1)
        o_diag = []
        for j in range(N_HEADS // 2):
            pair = slice(2 * HEAD_W * j, 2 * HEAD_W * (j + 1))
            v_rep = jnp.broadcast_to(v[:, pair].reshape(N_SUB, 1, SUB, 2 * HEAD_W),
                                     (N_SUB, SUB, SUB, 2 * HEAD_W))
            w = (a_rep[j].reshape(N_SUB, SUB, SUB, 2 * HEAD_W) * v_rep).reshape(CHUNK * SUB, 2 * HEAD_W)
            o_diag.append(jnp.dot(seg_sum, w.astype(BF16), preferred_element_type=F32))
        o_intra = jnp.concatenate(o_diag, axis=1) + jnp.concatenate(
            [jnp.zeros((SUB, HG_WIDTH), F32), o_off], axis=0)
        o = o_inter + o_intra

        y = jnp.concatenate(
            [o[:, _head(h)] * lax.rsqrt(jnp.mean(o[:, _head(h)] * o[:, _head(h)], axis=-1, keepdims=True) + EPS)
             for h in range(N_HEADS)], axis=1)
        y = y * gn * (hg * _sigmoid(hg))
        o_ref[pl.ds(r0, CHUNK), :] = y.astype(BF16)
        return carry

    lax.fori_loop(0, n_chunks, chunk, 0)

    @pl.when(r == pl.num_programs(1) - 1)
    def _():
        st = st_ref[...]
        for h in range(N_HEADS):
            sout_ref[0, h] = st[:, _head(h)].T


def _gla(gates, lower_bounds, hg_norm, s0, n_seq, seq_len, rows):
    m = gates.shape[1]
    nr = seq_len // rows
    seg = lambda s: pl.BlockSpec((1, rows, HG_WIDTH), lambda b, r: (s, b * nr + r, 0))
    state = pl.BlockSpec((1, N_HEADS, HEAD_W, HEAD_W), lambda b, r: (b, 0, 0, 0))
    return pl.pallas_call(
        functools.partial(_gla_kernel, n_chunks=rows // CHUNK),
        grid=(n_seq, nr),
        in_specs=[
            seg(0), seg(1), seg(2), seg(3),
            pl.BlockSpec((lower_bounds.shape[0], HG_WIDTH), lambda b, r: (0, 0)),
            pl.BlockSpec((1, HEAD_W), lambda b, r: (0, 0)),
            state,
        ],
        out_specs=[pl.BlockSpec((rows, HG_WIDTH), lambda b, r: (b * nr + r, 0)), state],
        out_shape=[
            jax.ShapeDtypeStruct((m, HG_WIDTH), BF16),
            jax.ShapeDtypeStruct((n_seq, N_HEADS, HEAD_W, HEAD_W), F32),
        ],
        scratch_shapes=[pltpu.VMEM((HEAD_W, HG_WIDTH), F32),
                        pltpu.VMEM((N_HEADS, CHUNK, HEAD_W), F32),
                        pltpu.VMEM((N_HEADS, CHUNK, HEAD_W), F32)],
        compiler_params=pltpu.CompilerParams(
            dimension_semantics=("arbitrary", "arbitrary"), vmem_limit_bytes=VMEM_LIMIT),
        name="hgrn2",
    )(gates, gates, gates, gates, lower_bounds, hg_norm, s0)


def _stacked_query(q):
    lane = lax.broadcasted_iota(jnp.int32, q.shape, 1)
    qbig = jnp.concatenate([jnp.where(lane < ATT_DH, q, 0.0), jnp.where(lane >= ATT_DH, q, 0.0)], axis=0)
    return qbig.T.astype(BF16)


def _softmax_update(s, v_bf, m_ref, l_ref, acc_ref):
    m_prev = m_ref[...]
    m_new = jnp.maximum(m_prev, jnp.max(s, axis=0, keepdims=True))
    alpha = jnp.exp2(m_prev - m_new)
    p = jnp.exp2(s - m_new)
    l_ref[...] = alpha * l_ref[...] + jnp.sum(p, axis=0, keepdims=True)
    pv = lax.dot_general(v_bf, p.astype(BF16), (((0,), (0,)), ((), ())), preferred_element_type=F32)
    acc_ref[...] = alpha * acc_ref[...] + pv
    m_ref[...] = m_new


def _lambda(lq1_ref, lk1_ref, lq2_ref, lk2_ref):
    s1 = jnp.sum(lq1_ref[...] * lk1_ref[...], axis=-1, keepdims=True)
    s2 = jnp.sum(lq2_ref[...] * lk2_ref[...], axis=-1, keepdims=True)
    return jnp.exp(s1) - jnp.exp(s2) + LAM_INIT


def _attn_finish(acc, l, lam, sub, n):
    o_both = (acc * (1.0 / l)).T
    o = o_both[:n] - lam * o_both[n:]
    return _rmsnorm_rows(o, sub) * (1.0 - LAM_INIT)


def _attn_prompt_kernel(q_ref, k_ref, vt_ref, lq1_ref, lk1_ref, lq2_ref, lk2_ref, sub_ref, o_ref,
                        qq_ref, s_ref, smax_ref, mask_ref, m_ref, acc_ref, *, t):
    qi = pl.program_id(1)
    qq_ref[...] = _stacked_query(q_ref[...].astype(F32))
    m_ref[...] = jnp.full(m_ref.shape, NEG_INF, F32)
    acc_ref[...] = jnp.zeros(acc_ref.shape, F32)
    ones_rows = jnp.ones((ONES_ROWS, t), BF16)

    def scores(kv):
        k0 = pl.multiple_of(kv * t, t)
        return jnp.dot(k_ref[pl.ds(k0, t), :], qq_ref[...], preferred_element_type=F32)

    def weighted_values(kv, p):
        k0 = pl.multiple_of(kv * t, t)
        vt = jnp.concatenate([vt_ref[:, pl.ds(k0, t)], ones_rows], axis=0)
        return jnp.dot(vt, p, preferred_element_type=F32)

    def softmax(s, s_max):
        m_prev = m_ref[...]
        m_new = jnp.maximum(m_prev, s_max)
        m_ref[...] = m_new
        return jnp.exp2(s - m_new).astype(BF16), jnp.exp2(m_prev - m_new)

    def stage(kv, cur, oth):
        s_next = scores(kv + 1)
        s_ref[oth] = s_next
        smax_ref[oth] = jnp.max(s_next, axis=0, keepdims=True)
        p, alpha = softmax(s_ref[cur], smax_ref[cur])
        acc_ref[...] = alpha * acc_ref[...] + weighted_values(kv, p)

    @pl.when((pl.program_id(0) == 0) & (qi == 0))
    def _():
        kpos = lax.broadcasted_iota(jnp.int32, mask_ref.shape, 0)
        col = lax.broadcasted_iota(jnp.int32, mask_ref.shape, 1)
        qpos = jnp.where(col >= t, col - t, col)
        mask_ref[...] = jnp.where((kpos >> 6) <= (qpos >> 6), 0.0, NEG_INF)

    def finish(cur):
        k0 = pl.multiple_of(qi * t, t)
        vt = jnp.concatenate([vt_ref[:, pl.ds(k0, t)], ones_rows], axis=0)
        acc = []
        for c in range(0, 2 * t, FINISH_COLS):
            cols = slice(c, c + FINISH_COLS)
            s = s_ref[cur, :, cols] + mask_ref[:, cols]
            m_prev = m_ref[:, cols]
            m_new = jnp.maximum(m_prev, jnp.max(s, axis=0, keepdims=True))
            p = jnp.exp2(s - m_new).astype(BF16)
            acc.append(jnp.exp2(m_prev - m_new) * acc_ref[:, cols]
                       + jnp.dot(vt, p, preferred_element_type=F32))
        acc = jnp.concatenate(acc, axis=1)
        lam = _lambda(lq1_ref, lk1_ref, lq2_ref, lk2_ref)
        o_ref[...] = _attn_finish(acc[:HEAD_W], acc[HEAD_W:HEAD_W + 1], lam, sub_ref[...], t).astype(BF16)

    s_first = scores(0)
    s_ref[0] = s_first
    smax_ref[0] = jnp.max(s_first, axis=0, keepdims=True)

    def pair(j, carry):
        stage(2 * j, 0, 1)
        stage(2 * j + 1, 1, 0)
        return carry

    lax.fori_loop(0, qi // 2, pair, 0)

    @pl.when(qi % 2 == 1)
    def _():
        stage(qi - 1, 0, 1)
        finish(1)

    @pl.when(qi % 2 == 0)
    def _():
        finish(0)


def _attn_prompt(q_bf, k_bf, vt_bf, lams, subln, t):
    n = q_bf.shape[0]
    small = lambda shape: pl.BlockSpec(shape, lambda h, i: (0, 0))
    return pl.pallas_call(
        functools.partial(_attn_prompt_kernel, t=t),
        grid=(N_HEADS, n // t),
        in_specs=[
            pl.BlockSpec((t, HEAD_W), lambda h, i: (i, h)),
            pl.BlockSpec((n, HEAD_W), lambda h, i: (0, h)),
            pl.BlockSpec((HEAD_W, n), lambda h, i: (h, 0)),
            small((1, ATT_DH)), small((1, ATT_DH)), small((1, ATT_DH)), small((1, ATT_DH)),
            small((1, HEAD_W)),
        ],
        out_specs=pl.BlockSpec((t, HEAD_W), lambda h, i: (i, h)),
        out_shape=jax.ShapeDtypeStruct((n, ATT_WIDTH), BF16),
        scratch_shapes=[
            pltpu.VMEM((HEAD_W, 2 * t), BF16),
            pltpu.VMEM((2, t, 2 * t), F32),
            pltpu.VMEM((2, 1, 2 * t), F32),
            pltpu.VMEM((t, 2 * t), F32),
            pltpu.VMEM((1, 2 * t), F32),
            pltpu.VMEM((HEAD_W + ONES_ROWS, 2 * t), F32),
        ],
        compiler_params=pltpu.CompilerParams(
            dimension_semantics=("arbitrary", "arbitrary"), vmem_limit_bytes=VMEM_LIMIT),
        name="attn_prompt",
    )(q_bf, k_bf, vt_bf, *lams, subln)


def _attn_sample_kernel(q_ref, kc_ref, vc_ref, kn_ref, vn_ref, lq1_ref, lk1_ref, lq2_ref, lk2_ref,
                        sub_ref, o_ref, qq_ref, m_ref, l_ref, acc_ref, *, n_q, tk):
    t = pl.program_id(1)

    @pl.when(t == 0)
    def _():
        for h in range(N_HEADS):
            qq_ref[h] = _stacked_query(q_ref[:, _head(h)].astype(F32))
        m_ref[...] = jnp.full(m_ref.shape, NEG_INF, F32)
        l_ref[...] = jnp.zeros(l_ref.shape, F32)
        acc_ref[...] = jnp.zeros(acc_ref.shape, F32)

    def update(h, k_bf, v_bf):
        s = jnp.dot(k_bf, qq_ref[h], preferred_element_type=F32)
        _softmax_update(s, v_bf, m_ref.at[h], l_ref.at[h], acc_ref.at[h])

    for h in range(N_HEADS):
        rows = pl.ds(h, tk, stride=N_HEADS)
        update(h, kc_ref[0, rows, :].astype(BF16), vc_ref[0, rows, :].astype(BF16))

    @pl.when(t == pl.num_programs(1) - 1)
    def _():
        lam = _lambda(lq1_ref, lk1_ref, lq2_ref, lk2_ref)
        for h in range(N_HEADS):
            update(h, kn_ref[:, _head(h)], vn_ref[:, _head(h)])
            o_ref[:, _head(h)] = _attn_finish(acc_ref[h], l_ref[h], lam, sub_ref[...], n_q).astype(BF16)


def _attn_sample(q_bf, cache_k, cache_v, kn_bf, vn_bf, lams, subln, n_q, tk):
    n_b = cache_k.shape[0]
    past = cache_k.shape[1] // N_HEADS
    assert past % CHUNK == 0 and n_q <= CHUNK and past % tk == 0
    small = lambda shape: pl.BlockSpec(shape, lambda b, t: (0, 0))
    rows = pl.BlockSpec((n_q, ATT_WIDTH), lambda b, t: (b, 0))
    cache = pl.BlockSpec((1, tk * N_HEADS, HEAD_W), lambda b, t: (b, t, 0))
    return pl.pallas_call(
        functools.partial(_attn_sample_kernel, n_q=n_q, tk=tk),
        grid=(n_b, past // tk),
        in_specs=[
            rows, cache, cache, rows, rows,
            small((1, ATT_DH)), small((1, ATT_DH)), small((1, ATT_DH)), small((1, ATT_DH)),
            small((1, HEAD_W)),
        ],
        out_specs=rows,
        out_shape=jax.ShapeDtypeStruct((n_b * n_q, ATT_WIDTH), BF16),
        scratch_shapes=[
            pltpu.VMEM((N_HEADS, HEAD_W, 2 * n_q), BF16),
            pltpu.VMEM((N_HEADS, 1, 2 * n_q), F32),
            pltpu.VMEM((N_HEADS, 1, 2 * n_q), F32),
            pltpu.VMEM((N_HEADS, HEAD_W, 2 * n_q), F32),
        ],
        compiler_params=pltpu.CompilerParams(
            dimension_semantics=("arbitrary", "arbitrary"), vmem_limit_bytes=VMEM_LIMIT),
        name="attn_sample",
    )(q_bf, cache_k, cache_v, kn_bf, vn_bf, *lams, subln)


def _outproj_kernel(x_ref, a_ref, b_ref, w_ref, o_ref):
    o_ref[...] = (x_ref[...]
                  + jnp.dot(a_ref[...], w_ref[:HG_WIDTH, :], preferred_element_type=F32)
                  + jnp.dot(b_ref[...], w_ref[HG_WIDTH:, :], preferred_element_type=F32))


def _outproj(x, mix_hg, mix_at, w_bf, tm):
    m = x.shape[0]
    row = lambda i: (i, 0)
    return pl.pallas_call(
        _outproj_kernel,
        grid=(m // tm,),
        in_specs=[
            pl.BlockSpec((tm, D_MODEL), row),
            pl.BlockSpec((tm, HG_WIDTH), row),
            pl.BlockSpec((tm, ATT_WIDTH), row),
            pl.BlockSpec((HG_WIDTH + ATT_WIDTH, D_MODEL), lambda i: (0, 0), pipeline_mode=pl.Buffered(1)),
        ],
        out_specs=pl.BlockSpec((tm, D_MODEL), row),
        out_shape=jax.ShapeDtypeStruct((m, D_MODEL), F32),
        compiler_params=pltpu.CompilerParams(
            dimension_semantics=("arbitrary",), vmem_limit_bytes=VMEM_LIMIT),
        name="outproj",
    )(x, mix_hg, mix_at, w_bf)


def _mlp_kernel(x_ref, gain_ref, wu_ref, wd_ref, gfin_ref, o_ref, h_ref):
    j = pl.program_id(1)

    @pl.when(j == 0)
    def _():
        x = x_ref[...]
        h_ref[...] = _rmsnorm_rows(x, gain_ref[...]).astype(BF16)
        o_ref[...] = x

    u = jnp.dot(h_ref[...], wu_ref[...], preferred_element_type=F32)
    u = jnp.square(jnp.maximum(u, 0.0)).astype(BF16)
    o_ref[...] += jnp.dot(u, wd_ref[...], preferred_element_type=F32)

    @pl.when(j == pl.num_programs(1) - 1)
    def _():
        o_ref[...] = _rmsnorm_rows(o_ref[...], gfin_ref[...])


def _mlp(x, gain, wu_bf, wd_bf, gfin, tm, tf):
    m = x.shape[0]
    return pl.pallas_call(
        _mlp_kernel,
        grid=(m // tm, D_FF // tf),
        in_specs=[
            pl.BlockSpec((tm, D_MODEL), lambda i, j: (i, 0)),
            pl.BlockSpec((1, D_MODEL), lambda i, j: (0, 0)),
            pl.BlockSpec((D_MODEL, tf), lambda i, j: (0, j)),
            pl.BlockSpec((tf, D_MODEL), lambda i, j: (j, 0)),
            pl.BlockSpec((1, D_MODEL), lambda i, j: (0, 0)),
        ],
        out_specs=pl.BlockSpec((tm, D_MODEL), lambda i, j: (i, 0)),
        out_shape=jax.ShapeDtypeStruct((m, D_MODEL), F32),
        scratch_shapes=[pltpu.VMEM((tm, D_MODEL), BF16)],
        compiler_params=pltpu.CompilerParams(
            dimension_semantics=("arbitrary", "arbitrary"), vmem_limit_bytes=VMEM_LIMIT),
        name="mlp",
    )(x, gain, wu_bf, wd_bf, gfin)


def kernel(x_prompt, x_sample, cache_k, cache_v, state_hgrn, norm_attn, w_in, lower_bounds, hg_norm,
           lambda_q1, lambda_k1, lambda_q2, lambda_k2, subln, w_out, norm_mlp, w_up, w_down, norm_final):
    depth = w_in.shape[0]
    assert depth == 1
    n_pb, n_p, _ = x_prompt.shape
    n_sb, n_s, _ = x_sample.shape
    assert n_pb == 1
    past = cache_k.shape[2]

    w_in_bf = w_in[0].astype(BF16)
    w_out_bf = w_out[0].astype(BF16)
    w_up_bf = w_up[0].astype(BF16)
    w_down_bf = w_down[0].astype(BF16)
    lams = (lambda_q1, lambda_k1, lambda_q2, lambda_k2)
    gfin = norm_final.reshape(1, D_MODEL)

    xp = x_prompt.reshape(n_p, D_MODEL)
    xs = x_sample.reshape(n_sb * n_s, D_MODEL)

    gates_p, q_p, kf_p, vf_p, kb_p, vtb_p = _inproj(xp, norm_attn, w_in_bf, tm=256, v_transposed=True)
    gates_s, q_s, kf_s, vf_s, kb_s, vb_s = _inproj(xs, norm_attn, w_in_bf, tm=256, v_transposed=False)

    s0_p = jnp.zeros((1, N_HEADS, HEAD_W, HEAD_W), F32)
    ohg_p, st_p = _gla(gates_p, lower_bounds, hg_norm, s0_p, n_seq=1, seq_len=n_p, rows=512)
    ohg_s, st_s = _gla(gates_s, lower_bounds, hg_norm, state_hgrn[0], n_seq=n_sb, seq_len=n_s, rows=n_s)

    oat_p = _attn_prompt(q_p, kb_p, vtb_p, lams, subln, t=512)
    ck = cache_k[0].reshape(n_sb, past * N_HEADS, HEAD_W)
    cv = cache_v[0].reshape(n_sb, past * N_HEADS, HEAD_W)
    oat_s = _attn_sample(q_s, ck, cv, kb_s, vb_s, lams, subln, n_q=n_s, tk=1024)

    x1_p = _outproj(xp, ohg_p, oat_p, w_out_bf, tm=512)
    x1_s = _outproj(xs, ohg_s, oat_s, w_out_bf, tm=512)

    y_p = _mlp(x1_p, norm_mlp, w_up_bf, w_down_bf, gfin, tm=1024, tf=512)
    y_s = _mlp(x1_s, norm_mlp, w_up_bf, w_down_bf, gfin, tm=512, tf=512)

    return (
        y_p.reshape(n_pb, n_p, D_MODEL),
        y_s.reshape(n_sb, n_s, D_MODEL),
        kf_p.reshape(1, n_pb, n_p, N_HEADS, HEAD_W),
        vf_p.reshape(1, n_pb, n_p, N_HEADS, HEAD_W),
        st_p.reshape(1, n_pb, N_HEADS, HEAD_W, HEAD_W),
        kf_s.reshape(1, n_sb, n_s, N_HEADS, HEAD_W),
        vf_s.reshape(1, n_sb, n_s, N_HEADS, HEAD_W),
        st_s.reshape(1, n_sb, N_HEADS, HEAD_W, HEAD_W),
    )
```

```python
import functools
import math

import jax
import jax.numpy as jnp
from jax import lax
from jax.experimental import pallas as pl
from jax.experimental.pallas import tpu as pltpu

F32 = jnp.float32
BF16 = jnp.bfloat16

D_MODEL = 2048
HG_WIDTH = 1024
ATT_WIDTH = 1024
N_HEADS = 8
HEAD_W = 128
ATT_DH = 64
CHUNK = 64
SUB = 8
N_SUB = CHUNK // SUB
D_FF = 4 * D_MODEL
N_SEG = 7
EPS = 1e-6
NEG_INF = -1e30
LOG2E = 1.4426950408889634
Q_SCALE = ATT_DH ** -0.5 * LOG2E
LAM_INIT = 0.8 - 0.6 * math.exp(-0.3 * 0)
ONES_ROWS = 16
FINISH_COLS = 256

VMEM_LIMIT = 56 * 1024 * 1024


def _rmsnorm_rows(x, gain):
    return x * lax.rsqrt(jnp.mean(x * x, axis=-1, keepdims=True) + EPS) * gain


def _sigmoid(x):
    return 1.0 / (1.0 + jnp.exp(-x))


def _head(h):
    return slice(h * HEAD_W, (h + 1) * HEAD_W)


def _inproj_kernel(x_ref, gain_ref, w_ref, gates_ref, q_ref, kf_ref, vf_ref, kb_ref, vb_ref, *, v_transposed):
    h = _rmsnorm_rows(x_ref[...], gain_ref[...]).astype(BF16)

    def segment(s):
        return jnp.dot(h, w_ref[:, s * HG_WIDTH:(s + 1) * HG_WIDTH], preferred_element_type=F32)

    for s in range(4):
        gates_ref[s] = segment(s)
    q_ref[...] = (segment(4) * Q_SCALE).astype(BF16)
    k = segment(5)
    kf_ref[...] = k
    kb_ref[...] = k.astype(BF16)
    v = segment(6)
    vf_ref[...] = v
    vb_ref[...] = (v.T if v_transposed else v).astype(BF16)


def _inproj(x, gain, w_bf, tm, v_transposed):
    m = x.shape[0]
    seg = HG_WIDTH
    row = lambda i: (i, 0)
    vb_spec = pl.BlockSpec((seg, tm), lambda i: (0, i)) if v_transposed else pl.BlockSpec((tm, seg), row)
    return pl.pallas_call(
        functools.partial(_inproj_kernel, v_transposed=v_transposed),
        grid=(m // tm,),
        in_specs=[
            pl.BlockSpec((tm, D_MODEL), row),
            pl.BlockSpec((1, D_MODEL), lambda i: (0, 0)),
            pl.BlockSpec((D_MODEL, N_SEG * seg), lambda i: (0, 0), pipeline_mode=pl.Buffered(1)),
        ],
        out_specs=[
            pl.BlockSpec((4, tm, seg), lambda i: (0, i, 0)),
            pl.BlockSpec((tm, seg), row),
            pl.BlockSpec((tm, seg), row),
            pl.BlockSpec((tm, seg), row),
            pl.BlockSpec((tm, seg), row),
            vb_spec,
        ],
        out_shape=[
            jax.ShapeDtypeStruct((4, m, seg), F32),
            jax.ShapeDtypeStruct((m, seg), BF16),
            jax.ShapeDtypeStruct((m, seg), F32),
            jax.ShapeDtypeStruct((m, seg), F32),
            jax.ShapeDtypeStruct((m, seg), BF16),
            jax.ShapeDtypeStruct((seg, m) if v_transposed else (m, seg), BF16),
        ],
        compiler_params=pltpu.CompilerParams(
            dimension_semantics=("arbitrary",), vmem_limit_bytes=VMEM_LIMIT),
        name="inproj",
    )(x, gain, w_bf)


def _gla_kernel(hq_ref, hf_ref, hi_ref, hg_ref, lbnd_ref, gn_ref, s0_ref, o_ref, sout_ref, st_ref,
                q_ref, b_ref, *, n_chunks):
    r = pl.program_id(1)

    @pl.when(r == 0)
    def _():
        st_ref[...] = jnp.concatenate([s0_ref[0, h].T for h in range(N_HEADS)], axis=1)

    lbs = lbnd_ref[...]
    e = jnp.exp(lbs - jnp.max(lbs, axis=0, keepdims=True))
    lb = e[0:1, :] / jnp.sum(e, axis=0, keepdims=True)
    gn = jnp.concatenate([gn_ref[...]] * N_HEADS, axis=1)

    ri = lax.broadcasted_iota(jnp.int32, (CHUNK, CHUNK), 0)
    ci = lax.broadcasted_iota(jnp.int32, (CHUNK, CHUNK), 1)
    tril = (ri >= ci).astype(F32)
    pr = lax.broadcasted_iota(jnp.int32, (2 * HEAD_W, 2 * HEAD_W), 0)
    pc = lax.broadcasted_iota(jnp.int32, (2 * HEAD_W, 2 * HEAD_W), 1)
    pair_ones = ((pr >> 7) == (pc >> 7)).astype(BF16)
    gr = lax.broadcasted_iota(jnp.int32, (CHUNK, CHUNK * SUB), 0)
    gc = lax.broadcasted_iota(jnp.int32, (CHUNK, CHUNK * SUB), 1)
    seg_sum = (((gc >> 3) == gr) & ((gc & (SUB - 1)) <= (gr & (SUB - 1)))).astype(BF16)
    n_off = (N_SUB - 1) * SUB
    n_key = SUB * (N_SUB * (N_SUB - 1) // 2)
    orow = lax.broadcasted_iota(jnp.int32, (n_off, n_key), 0) >> 3
    ocol = lax.broadcasted_iota(jnp.int32, (n_off, n_key), 1)
    ocol_seg = sum((ocol >= 4 * i * (i - 1)).astype(jnp.int32) for i in range(2, N_SUB))
    off_mask = orow == ocol_seg
    nt = (((1,), (1,)), ((), ()))
    tn = (((0,), (0,)), ((), ()))

    def chunk(c, carry):
        r0 = pl.multiple_of(c * CHUNK, CHUNK)
        hq = hq_ref[0, pl.ds(r0, CHUNK), :]
        hf = hf_ref[0, pl.ds(r0, CHUNK), :]
        v = hi_ref[0, pl.ds(r0, CHUNK), :]
        hg = hg_ref[0, pl.ds(r0, CHUNK), :]

        f = lb + (1.0 - lb) * _sigmoid(hf)
        g = jnp.log(f) * LOG2E
        kk = 1.0 - f
        q = hq * _sigmoid(hq)
        b = jnp.dot(tril, g, precision=lax.Precision.HIGHEST, preferred_element_type=F32)
        v_bf = v.astype(BF16)
        for h in range(N_HEADS):
            q_ref[h] = q[:, _head(h)]
            b_ref[h] = b[:, _head(h)]

        st = st_ref[...]
        st_bf = st.astype(BF16)
        b_last = b[CHUNK - 1:CHUNK, :]
        q_in = (q * jnp.exp2(b)).astype(BF16)
        k_dec = (kk * jnp.exp2(b_last - b)).astype(BF16)
        o_inter = jnp.concatenate(
            [lax.dot_general(q_in[:, _head(h)], st_bf[:, _head(h)], nt, preferred_element_type=F32)
             for h in range(N_HEADS)], axis=1)
        upd = jnp.concatenate(
            [lax.dot_general(v_bf[:, _head(h)], k_dec[:, _head(h)], tn, preferred_element_type=F32)
             for h in range(N_HEADS)], axis=1)
        st_ref[...] = jnp.exp2(b_last) * st + upd

        qt, kh, vh = [], [], []
        for i in range(1, N_SUB):
            lo = i * SUB
            b_start = b[lo - 1:lo, :]
            qt.append(q[lo:lo + SUB] * jnp.exp2(b[lo:lo + SUB] - b_start))
            kh.append(kk[:lo] * jnp.exp2(b_start - b[:lo]))
            vh.append(v[:lo])
        qt = jnp.concatenate(qt, axis=0).astype(BF16)
        kh = jnp.concatenate(kh, axis=0).astype(BF16)
        vh = jnp.concatenate(vh, axis=0).astype(BF16)
        a_off = [lax.dot_general(qt[:, _head(h)], kh[:, _head(h)], nt, preferred_element_type=F32)
                 for h in range(N_HEADS)]

        a_rep = []
        for j in range(N_HEADS // 2):
            p_pair = []
            for h in (2 * j, 2 * j + 1):
                kk_h = kk[:, _head(h)]
                b_h = b[:, _head(h)]
                rows = []
                for r in range(CHUNK):
                    lo = r - r % SUB
                    q_row = q_ref[h, pl.ds(r, SUB, stride=0), :]
                    b_row = b_ref[h, pl.ds(r, SUB, stride=0), :]
                    rows.append((q_row * kk_h[lo:lo + SUB])
                                * jnp.exp2(jnp.minimum(b_row - b_h[lo:lo + SUB], 0.0)))
                p_pair.append(jnp.concatenate(rows, axis=0).astype(BF16))
            a_rep.append(jnp.dot(jnp.concatenate(p_pair, axis=1), pair_ones,
                                 preferred_element_type=F32))

        o_off = jnp.concatenate(
            [jnp.dot(jnp.where(off_mask, a_off[h], 0.0).astype(BF16), vh[:, _head(h)],
                     preferred_element_type=F32) for h in range(N_HEADS)], axis=1)
        o_diag = []
        for j in range(N_HEADS // 2):
            pair = slice(2 * HEAD_W * j, 2 * HEAD_W * (j + 1))
            v_rep = jnp.broadcast_to(v[:, pair].reshape(N_SUB, 1, SUB, 2 * HEAD_W),
                                     (N_SUB, SUB, SUB, 2 * HEAD_W))
            w = (a_rep[j].reshape(N_SUB, SUB, SUB, 2 * HEAD_W) * v_rep).reshape(CHUNK * SUB, 2 * HEAD_W)
            o_diag.append(jnp.dot(seg_sum, w.astype(BF16), preferred_element_type=F32))
        o_intra = jnp.concatenate(o_diag, axis=1) + jnp.concatenate(
            [jnp.zeros((SUB, HG_WIDTH), F32), o_off], axis=0)
        o = o_inter + o_intra

        y = jnp.concatenate(
            [o[:, _head(h)] * lax.rsqrt(jnp.mean(o[:, _head(h)] * o[:, _head(h)], axis=-1, keepdims=True) + EPS)
             for h in range(N_HEADS)], axis=1)
        y = y * gn * (hg * _sigmoid(hg))
        o_ref[pl.ds(r0, CHUNK), :] = y.astype(BF16)
        return carry

    lax.fori_loop(0, n_chunks, chunk, 0)

    @pl.when(r == pl.num_programs(1) - 1)
    def _():
        st = st_ref[...]
        for h in range(N_HEADS):
            sout_ref[0, h] = st[:, _head(h)].T


def _gla(gates, lower_bounds, hg_norm, s0, n_seq, seq_len, rows):
    m = gates.shape[1]
    nr = seq_len // rows
    seg = lambda s: pl.BlockSpec((1, rows, HG_WIDTH), lambda b, r: (s, b * nr + r, 0))
    state = pl.BlockSpec((1, N_HEADS, HEAD_W, HEAD_W), lambda b, r: (b, 0, 0, 0))
    return pl.pallas_call(
        functools.partial(_gla_kernel, n_chunks=rows // CHUNK),
        grid=(n_seq, nr),
        in_specs=[
            seg(0), seg(1), seg(2), seg(3),
            pl.BlockSpec((lower_bounds.shape[0], HG_WIDTH), lambda b, r: (0, 0)),
            pl.BlockSpec((1, HEAD_W), lambda b, r: (0, 0)),
            state,
        ],
        out_specs=[pl.BlockSpec((rows, HG_WIDTH), lambda b, r: (b * nr + r, 0)), state],
        out_shape=[
            jax.ShapeDtypeStruct((m, HG_WIDTH), BF16),
            jax.ShapeDtypeStruct((n_seq, N_HEADS, HEAD_W, HEAD_W), F32),
        ],
        scratch_shapes=[pltpu.VMEM((HEAD_W, HG_WIDTH), F32),
                        pltpu.VMEM((N_HEADS, CHUNK, HEAD_W), F32),
                        pltpu.VMEM((N_HEADS, CHUNK, HEAD_W), F32)],
        compiler_params=pltpu.CompilerParams(
            dimension_semantics=("arbitrary", "arbitrary"), vmem_limit_bytes=VMEM_LIMIT),
        name="hgrn2",
    )(gates, gates, gates, gates, lower_bounds, hg_norm, s0)


def _stacked_query(q):
    lane = lax.broadcasted_iota(jnp.int32, q.shape, 1)
    qbig = jnp.concatenate([jnp.where(lane < ATT_DH, q, 0.0), jnp.where(lane >= ATT_DH, q, 0.0)], axis=0)
    return qbig.T.astype(BF16)


def _softmax_update(s, v_bf, m_ref, l_ref, acc_ref):
    m_prev = m_ref[...]
    m_new = jnp.maximum(m_prev, jnp.max(s, axis=0, keepdims=True))
    alpha = jnp.exp2(m_prev - m_new)
    p = jnp.exp2(s - m_new)
    l_ref[...] = alpha * l_ref[...] + jnp.sum(p, axis=0, keepdims=True)
    pv = lax.dot_general(v_bf, p.astype(BF16), (((0,), (0,)), ((), ())), preferred_element_type=F32)
    acc_ref[...] = alpha * acc_ref[...] + pv
    m_ref[...] = m_new


def _lambda(lq1_ref, lk1_ref, lq2_ref, lk2_ref):
    s1 = jnp.sum(lq1_ref[...] * lk1_ref[...], axis=-1, keepdims=True)
    s2 = jnp.sum(lq2_ref[...] * lk2_ref[...], axis=-1, keepdims=True)
    return jnp.exp(s1) - jnp.exp(s2) + LAM_INIT


def _attn_finish(acc, l, lam, sub, n):
    o_both = (acc * (1.0 / l)).T
    o = o_both[:n] - lam * o_both[n:]
    return _rmsnorm_rows(o, sub) * (1.0 - LAM_INIT)


def _attn_prompt_kernel(q_ref, k_ref, vt_ref, lq1_ref, lk1_ref, lq2_ref, lk2_ref, sub_ref, o_ref,
                        qq_ref, s_ref, smax_ref, mask_ref, m_ref, acc_ref, *, t):
    qi = pl.program_id(1)
    qq_ref[...] = _stacked_query(q_ref[...].astype(F32))
    m_ref[...] = jnp.full(m_ref.shape, NEG_INF, F32)
    acc_ref[...] = jnp.zeros(acc_ref.shape, F32)
    ones_rows = jnp.ones((ONES_ROWS, t), BF16)

    def scores(kv):
        k0 = pl.multiple_of(kv * t, t)
        return jnp.dot(k_ref[pl.ds(k0, t), :], qq_ref[...], preferred_element_type=F32)

    def weighted_values(kv, p):
        k0 = pl.multiple_of(kv * t, t)
        vt = jnp.concatenate([vt_ref[:, pl.ds(k0, t)], ones_rows], axis=0)
        return jnp.dot(vt, p, preferred_element_type=F32)

    def softmax(s, s_max):
        m_prev = m_ref[...]
        m_new = jnp.maximum(m_prev, s_max)
        m_ref[...] = m_new
        return jnp.exp2(s - m_new).astype(BF16), jnp.exp2(m_prev - m_new)

    def stage(kv, cur, oth):
        s_next = scores(kv + 1)
        s_ref[oth] = s_next
        smax_ref[oth] = jnp.max(s_next, axis=0, keepdims=True)
        p, alpha = softmax(s_ref[cur], smax_ref[cur])
        acc_ref[...] = alpha * acc_ref[...] + weighted_values(kv, p)

    @pl.when((pl.program_id(0) == 0) & (qi == 0))
    def _():
        kpos = lax.broadcasted_iota(jnp.int32, mask_ref.shape, 0)
        col = lax.broadcasted_iota(jnp.int32, mask_ref.shape, 1)
        qpos = jnp.where(col >= t, col - t, col)
        mask_ref[...] = jnp.where((kpos >> 6) <= (qpos >> 6), 0.0, NEG_INF)

    def finish(cur):
        k0 = pl.multiple_of(qi * t, t)
        vt = jnp.concatenate([vt_ref[:, pl.ds(k0, t)], ones_rows], axis=0)
        acc = []
        for c in range(0, 2 * t, FINISH_COLS):
            cols = slice(c, c + FINISH_COLS)
            s = s_ref[cur, :, cols] + mask_ref[:, cols]
            m_prev = m_ref[:, cols]
            m_new = jnp.maximum(m_prev, jnp.max(s, axis=0, keepdims=True))
            p = jnp.exp2(s - m_new).astype(BF16)
            acc.append(jnp.exp2(m_prev - m_new) * acc_ref[:, cols]
                       + jnp.dot(vt, p, preferred_element_type=F32))
        acc = jnp.concatenate(acc, axis=1)
        lam = _lambda(lq1_ref, lk1_ref, lq2_ref, lk2_ref)
        o_ref[...] = _attn_finish(acc[:HEAD_W], acc[HEAD_W:HEAD_W + 1], lam, sub_ref[...], t).astype(BF16)

    s_first = scores(0)
    s_ref[0] = s_first
    smax_ref[0] = jnp.max(s_first, axis=0, keepdims=True)

    def pair(j, carry):
        stage(2 * j, 0, 1)
        stage(2 * j + 1, 1, 0)
        return carry

    lax.fori_loop(0, qi // 2, pair, 0)

    @pl.when(qi % 2 == 1)
    def _():
        stage(qi - 1, 0, 1)
        finish(1)

    @pl.when(qi % 2 == 0)
    def _():
        finish(0)


def _attn_prompt(q_bf, k_bf, vt_bf, lams, subln, t):
    n = q_bf.shape[0]
    small = lambda shape: pl.BlockSpec(shape, lambda h, i: (0, 0))
    return pl.pallas_call(
        functools.partial(_attn_prompt_kernel, t=t),
        grid=(N_HEADS, n // t),
        in_specs=[
            pl.BlockSpec((t, HEAD_W), lambda h, i: (i, h)),
            pl.BlockSpec((n, HEAD_W), lambda h, i: (0, h)),
            pl.BlockSpec((HEAD_W, n), lambda h, i: (h, 0)),
            small((1, ATT_DH)), small((1, ATT_DH)), small((1, ATT_DH)), small((1, ATT_DH)),
            small((1, HEAD_W)),
        ],
        out_specs=pl.BlockSpec((t, HEAD_W), lambda h, i: (i, h)),
        out_shape=jax.ShapeDtypeStruct((n, ATT_WIDTH), BF16),
        scratch_shapes=[
            pltpu.VMEM((HEAD_W, 2 * t), BF16),
            pltpu.VMEM((2, t, 2 * t), F32),
            pltpu.VMEM((2, 1, 2 * t), F32),
            pltpu.VMEM((t, 2 * t), F32),
            pltpu.VMEM((1, 2 * t), F32),
            pltpu.VMEM((HEAD_W + ONES_ROWS, 2 * t), F32),
        ],
        compiler_params=pltpu.CompilerParams(
            dimension_semantics=("arbitrary", "arbitrary"), vmem_limit_bytes=VMEM_LIMIT),
        name="attn_prompt",
    )(q_bf, k_bf, vt_bf, *lams, subln)


def _attn_sample_kernel(q_ref, kc_ref, vc_ref, kn_ref, vn_ref, lq1_ref, lk1_ref, lq2_ref, lk2_ref,
                        sub_ref, o_ref, qq_ref, m_ref, l_ref, acc_ref, *, n_q, tk):
    t = pl.program_id(1)

    @pl.when(t == 0)
    def _():
        for h in range(N_HEADS):
            qq_ref[h] = _stacked_query(q_ref[:, _head(h)].astype(F32))
        m_ref[...] = jnp.full(m_ref.shape, NEG_INF, F32)
        l_ref[...] = jnp.zeros(l_ref.shape, F32)
        acc_ref[...] = jnp.zeros(acc_ref.shape, F32)

    def update(h, k_bf, v_bf):
        s = jnp.dot(k_bf, qq_ref[h], preferred_element_type=F32)
        _softmax_update(s, v_bf, m_ref.at[h], l_ref.at[h], acc_ref.at[h])

    for h in range(N_HEADS):
        rows = pl.ds(h, tk, stride=N_HEADS)
        update(h, kc_ref[0, rows, :].astype(BF16), vc_ref[0, rows, :].astype(BF16))

    @pl.when(t == pl.num_programs(1) - 1)
    def _():
        lam = _lambda(lq1_ref, lk1_ref, lq2_ref, lk2_ref)
        for h in range(N_HEADS):
            update(h, kn_ref[:, _head(h)], vn_ref[:, _head(h)])
            o_ref[:, _head(h)] = _attn_finish(acc_ref[h], l_ref[h], lam, sub_ref[...], n_q).astype(BF16)


def _attn_sample(q_bf, cache_k, cache_v, kn_bf, vn_bf, lams, subln, n_q, tk):
    n_b = cache_k.shape[0]
    past = cache_k.shape[1] // N_HEADS
    assert past % CHUNK == 0 and n_q <= CHUNK and past % tk == 0
    small = lambda shape: pl.BlockSpec(shape, lambda b, t: (0, 0))
    rows = pl.BlockSpec((n_q, ATT_WIDTH), lambda b, t: (b, 0))
    cache = pl.BlockSpec((1, tk * N_HEADS, HEAD_W), lambda b, t: (b, t, 0))
    return pl.pallas_call(
        functools.partial(_attn_sample_kernel, n_q=n_q, tk=tk),
        grid=(n_b, past // tk),
        in_specs=[
            rows, cache, cache, rows, rows,
            small((1, ATT_DH)), small((1, ATT_DH)), small((1, ATT_DH)), small((1, ATT_DH)),
            small((1, HEAD_W)),
        ],
        out_specs=rows,
        out_shape=jax.ShapeDtypeStruct((n_b * n_q, ATT_WIDTH), BF16),
        scratch_shapes=[
            pltpu.VMEM((N_HEADS, HEAD_W, 2 * n_q), BF16),
            pltpu.VMEM((N_HEADS, 1, 2 * n_q), F32),
            pltpu.VMEM((N_HEADS, 1, 2 * n_q), F32),
            pltpu.VMEM((N_HEADS, HEAD_W, 2 * n_q), F32),
        ],
        compiler_params=pltpu.CompilerParams(
            dimension_semantics=("arbitrary", "arbitrary"), vmem_limit_bytes=VMEM_LIMIT),
        name="attn_sample",
    )(q_bf, cache_k, cache_v, kn_bf, vn_bf, *lams, subln)


def _outproj_kernel(x_ref, a_ref, b_ref, w_ref, o_ref, wb_ref):
    @pl.when(pl.program_id(0) == 0)
    def _():
        wb_ref[...] = w_ref[...].astype(BF16)

    o_ref[...] = (x_ref[...]
                  + jnp.dot(a_ref[...], wb_ref[:HG_WIDTH, :], preferred_element_type=F32)
                  + jnp.dot(b_ref[...], wb_ref[HG_WIDTH:, :], preferred_element_type=F32))


def _outproj(x, mix_hg, mix_at, w, tm):
    m = x.shape[0]
    row = lambda i: (i, 0)
    return pl.pallas_call(
        _outproj_kernel,
        grid=(m // tm,),
        in_specs=[
            pl.BlockSpec((tm, D_MODEL), row),
            pl.BlockSpec((tm, HG_WIDTH), row),
            pl.BlockSpec((tm, ATT_WIDTH), row),
            pl.BlockSpec((HG_WIDTH + ATT_WIDTH, D_MODEL), lambda i: (0, 0), pipeline_mode=pl.Buffered(1)),
        ],
        out_specs=pl.BlockSpec((tm, D_MODEL), row),
        out_shape=jax.ShapeDtypeStruct((m, D_MODEL), F32),
        scratch_shapes=[pltpu.VMEM((HG_WIDTH + ATT_WIDTH, D_MODEL), BF16)],
        compiler_params=pltpu.CompilerParams(
            dimension_semantics=("arbitrary",), vmem_limit_bytes=VMEM_LIMIT),
        name="outproj",
    )(x, mix_hg, mix_at, w)


def _mlp_kernel(x_ref, gain_ref, wu_ref, wd_ref, gfin_ref, o_ref, h_ref):
    j = pl.program_id(1)

    @pl.when(j == 0)
    def _():
        x = x_ref[...]
        h_ref[...] = _rmsnorm_rows(x, gain_ref[...]).astype(BF16)
        o_ref[...] = x

    u = jnp.dot(h_ref[...], wu_ref[...].astype(BF16), preferred_element_type=F32)
    u = jnp.square(jnp.maximum(u, 0.0)).astype(BF16)
    o_ref[...] += jnp.dot(u, wd_ref[...].astype(BF16), preferred_element_type=F32)

    @pl.when(j == pl.num_programs(1) - 1)
    def _():
        o_ref[...] = _rmsnorm_rows(o_ref[...], gfin_ref[...])


def _mlp(x, gain, wu, wd, gfin, tm, tf):
    m = x.shape[0]
    return pl.pallas_call(
        _mlp_kernel,
        grid=(m // tm, D_FF // tf),
        in_specs=[
            pl.BlockSpec((tm, D_MODEL), lambda i, j: (i, 0)),
            pl.BlockSpec((1, D_MODEL), lambda i, j: (0, 0)),
            pl.BlockSpec((D_MODEL, tf), lambda i, j: (0, j)),
            pl.BlockSpec((tf, D_MODEL), lambda i, j: (j, 0)),
            pl.BlockSpec((1, D_MODEL), lambda i, j: (0, 0)),
        ],
        out_specs=pl.BlockSpec((tm, D_MODEL), lambda i, j: (i, 0)),
        out_shape=jax.ShapeDtypeStruct((m, D_MODEL), F32),
        scratch_shapes=[pltpu.VMEM((tm, D_MODEL), BF16)],
        compiler_params=pltpu.CompilerParams(
            dimension_semantics=("arbitrary", "arbitrary"), vmem_limit_bytes=VMEM_LIMIT),
        name="mlp",
    )(x, gain, wu, wd, gfin)


def kernel(x_prompt, x_sample, cache_k, cache_v, state_hgrn, norm_attn, w_in, lower_bounds, hg_norm,
           lambda_q1, lambda_k1, lambda_q2, lambda_k2, subln, w_out, norm_mlp, w_up, w_down, norm_final):
    depth = w_in.shape[0]
    assert depth == 1
    n_pb, n_p, _ = x_prompt.shape
    n_sb, n_s, _ = x_sample.shape
    assert n_pb == 1
    past = cache_k.shape[2]

    w_in_bf = w_in[0].astype(BF16)
    lams = (lambda_q1, lambda_k1, lambda_q2, lambda_k2)
    gfin = norm_final.reshape(1, D_MODEL)

    xp = x_prompt.reshape(n_p, D_MODEL)
    xs = x_sample.reshape(n_sb * n_s, D_MODEL)

    gates_p, q_p, kf_p, vf_p, kb_p, vtb_p = _inproj(xp, norm_attn, w_in_bf, tm=256, v_transposed=True)
    gates_s, q_s, kf_s, vf_s, kb_s, vb_s = _inproj(xs, norm_attn, w_in_bf, tm=256, v_transposed=False)

    s0_p = jnp.zeros((1, N_HEADS, HEAD_W, HEAD_W), F32)
    ohg_p, st_p = _gla(gates_p, lower_bounds, hg_norm, s0_p, n_seq=1, seq_len=n_p, rows=512)
    ohg_s, st_s = _gla(gates_s, lower_bounds, hg_norm, state_hgrn[0], n_seq=n_sb, seq_len=n_s, rows=n_s)

    oat_p = _attn_prompt(q_p, kb_p, vtb_p, lams, subln, t=512)
    ck = cache_k[0].reshape(n_sb, past * N_HEADS, HEAD_W)
    cv = cache_v[0].reshape(n_sb, past * N_HEADS, HEAD_W)
    oat_s = _attn_sample(q_s, ck, cv, kb_s, vb_s, lams, subln, n_q=n_s, tk=1024)

    x1_p = _outproj(xp, ohg_p, oat_p, w_out[0], tm=512)
    x1_s = _outproj(xs, ohg_s, oat_s, w_out[0], tm=512)

    y_p = _mlp(x1_p, norm_mlp, w_up[0], w_down[0], gfin, tm=1024, tf=512)
    y_s = _mlp(x1_s, norm_mlp, w_up[0], w_down[0], gfin, tm=512, tf=512)

    return (
        y_p.reshape(n_pb, n_p, D_MODEL),
        y_s.reshape(n_sb, n_s, D_MODEL),
        kf_p.reshape(1, n_pb, n_p, N_HEADS, HEAD_W),
        vf_p.reshape(1, n_pb, n_p, N_HEADS, HEAD_W),
        st_p.reshape(1, n_pb, N_HEADS, HEAD_W, HEAD_W),
        kf_s.reshape(1, n_sb, n_s, N_HEADS, HEAD_W),
        vf_s.reshape(1, n_sb, n_s, N_HEADS, HEAD_W),
        st_s.reshape(1, n_sb, N_HEADS, HEAD_W, HEAD_W),
    )
```

```python
import functools
import math

import jax
import jax.numpy as jnp
from jax import lax
from jax.experimental import pallas as pl
from jax.experimental.pallas import tpu as pltpu

F32 = jnp.float32
BF16 = jnp.bfloat16

D_MODEL = 2048
HG_WIDTH = 1024
ATT_WIDTH = 1024
N_HEADS = 8
HEAD_W = 128
ATT_DH = 64
CHUNK = 64
SUB = 8
N_SUB = CHUNK // SUB
D_FF = 4 * D_MODEL
N_SEG = 7
EPS = 1e-6
NEG_INF = -1e30
LOG2E = 1.4426950408889634
Q_SCALE = ATT_DH ** -0.5 * LOG2E
LAM_INIT = 0.8 - 0.6 * math.exp(-0.3 * 0)
ONES_ROWS = 16
FINISH_COLS = 256
STAGE_COLS = 256

VMEM_LIMIT = 56 * 1024 * 1024


def _rmsnorm_rows(x, gain):
    return x * lax.rsqrt(jnp.mean(x * x, axis=-1, keepdims=True) + EPS) * gain


def _sigmoid(x):
    return 1.0 / (1.0 + jnp.exp(-x))


def _head(h):
    return slice(h * HEAD_W, (h + 1) * HEAD_W)


def _inproj_kernel(x_ref, gain_ref, w_ref, gates_ref, q_ref, kf_ref, vf_ref, kb_ref, vb_ref, *, v_transposed):
    h = _rmsnorm_rows(x_ref[...], gain_ref[...]).astype(BF16)

    def segment(s):
        return jnp.dot(h, w_ref[:, s * HG_WIDTH:(s + 1) * HG_WIDTH], preferred_element_type=F32)

    for s in range(4):
        gates_ref[s] = segment(s)
    q_ref[...] = (segment(4) * Q_SCALE).astype(BF16)
    k = segment(5)
    kf_ref[...] = k
    kb_ref[...] = k.astype(BF16)
    v = segment(6)
    vf_ref[...] = v
    vb_ref[...] = (v.T if v_transposed else v).astype(BF16)


def _inproj(x, gain, w_bf, tm, v_transposed):
    m = x.shape[0]
    seg = HG_WIDTH
    row = lambda i: (i, 0)
    vb_spec = pl.BlockSpec((seg, tm), lambda i: (0, i)) if v_transposed else pl.BlockSpec((tm, seg), row)
    return pl.pallas_call(
        functools.partial(_inproj_kernel, v_transposed=v_transposed),
        grid=(m // tm,),
        in_specs=[
            pl.BlockSpec((tm, D_MODEL), row),
            pl.BlockSpec((1, D_MODEL), lambda i: (0, 0)),
            pl.BlockSpec((D_MODEL, N_SEG * seg), lambda i: (0, 0), pipeline_mode=pl.Buffered(1)),
        ],
        out_specs=[
            pl.BlockSpec((4, tm, seg), lambda i: (0, i, 0)),
            pl.BlockSpec((tm, seg), row),
            pl.BlockSpec((tm, seg), row),
            pl.BlockSpec((tm, seg), row),
            pl.BlockSpec((tm, seg), row),
            vb_spec,
        ],
        out_shape=[
            jax.ShapeDtypeStruct((4, m, seg), F32),
            jax.ShapeDtypeStruct((m, seg), BF16),
            jax.ShapeDtypeStruct((m, seg), F32),
            jax.ShapeDtypeStruct((m, seg), F32),
            jax.ShapeDtypeStruct((m, seg), BF16),
            jax.ShapeDtypeStruct((seg, m) if v_transposed else (m, seg), BF16),
        ],
        compiler_params=pltpu.CompilerParams(
            dimension_semantics=("arbitrary",), vmem_limit_bytes=VMEM_LIMIT),
        name="inproj",
    )(x, gain, w_bf)


def _gla_kernel(hq_ref, hf_ref, hi_ref, hg_ref, lbnd_ref, gn_ref, s0_ref, o_ref, sout_ref, st_ref,
                q_ref, b_ref, *, n_chunks):
    r = pl.program_id(1)

    @pl.when(r == 0)
    def _():
        st_ref[...] = jnp.concatenate([s0_ref[0, h].T for h in range(N_HEADS)], axis=1)

    lbs = lbnd_ref[...]
    e = jnp.exp(lbs - jnp.max(lbs, axis=0, keepdims=True))
    lb = e[0:1, :] / jnp.sum(e, axis=0, keepdims=True)
    gn = jnp.concatenate([gn_ref[...]] * N_HEADS, axis=1)

    ri = lax.broadcasted_iota(jnp.int32, (CHUNK, CHUNK), 0)
    ci = lax.broadcasted_iota(jnp.int32, (CHUNK, CHUNK), 1)
    tril = (ri >= ci).astype(F32)
    pr = lax.broadcasted_iota(jnp.int32, (2 * HEAD_W, 2 * HEAD_W), 0)
    pc = lax.broadcasted_iota(jnp.int32, (2 * HEAD_W, 2 * HEAD_W), 1)
    pair_ones = ((pr >> 7) == (pc >> 7)).astype(BF16)
    gr = lax.broadcasted_iota(jnp.int32, (CHUNK, CHUNK * SUB), 0)
    gc = lax.broadcasted_iota(jnp.int32, (CHUNK, CHUNK * SUB), 1)
    seg_sum = (((gc >> 3) == gr) & ((gc & (SUB - 1)) <= (gr & (SUB - 1)))).astype(BF16)
    n_off = (N_SUB - 1) * SUB
    n_key = SUB * (N_SUB * (N_SUB - 1) // 2)
    orow = lax.broadcasted_iota(jnp.int32, (n_off, n_key), 0) >> 3
    ocol = lax.broadcasted_iota(jnp.int32, (n_off, n_key), 1)
    ocol_seg = sum((ocol >= 4 * i * (i - 1)).astype(jnp.int32) for i in range(2, N_SUB))
    off_mask = orow == ocol_seg
    nt = (((1,), (1,)), ((), ()))
    tn = (((0,), (0,)), ((), ()))

    def chunk(c, carry):
        r0 = pl.multiple_of(c * CHUNK, CHUNK)
        hq = hq_ref[0, pl.ds(r0, CHUNK), :]
        hf = hf_ref[0, pl.ds(r0, CHUNK), :]
        v = hi_ref[0, pl.ds(r0, CHUNK), :]
        hg = hg_ref[0, pl.ds(r0, CHUNK), :]

        f = lb + (1.0 - lb) * _sigmoid(hf)
        g = jnp.log(f) * LOG2E
        kk = 1.0 - f
        q = hq * _sigmoid(hq)
        b = jnp.dot(tril, g, precision=lax.Precision.HIGHEST, preferred_element_type=F32)
        v_bf = v.astype(BF16)
        for h in range(N_HEADS):
            q_ref[h] = q[:, _head(h)]
            b_ref[h] = b[:, _head(h)]

        st = st_ref[...]
        st_bf = st.astype(BF16)
        b_last = b[CHUNK - 1:CHUNK, :]
        q_in = (q * jnp.exp2(b)).astype(BF16)
        k_dec = (kk * jnp.exp2(b_last - b)).astype(BF16)
        o_inter = jnp.concatenate(
            [lax.dot_general(q_in[:, _head(h)], st_bf[:, _head(h)], nt, preferred_element_type=F32)
             for h in range(N_HEADS)], axis=1)
        upd = jnp.concatenate(
            [lax.dot_general(v_bf[:, _head(h)], k_dec[:, _head(h)], tn, preferred_element_type=F32)
             for h in range(N_HEADS)], axis=1)
        st_ref[...] = jnp.exp2(b_last) * st + upd

        qt, kh, vh = [], [], []
        for i in range(1, N_SUB):
            lo = i * SUB
            b_start = b[lo - 1:lo, :]
            qt.append(q[lo:lo + SUB] * jnp.exp2(b[lo:lo + SUB] - b_start))
            kh.append(kk[:lo] * jnp.exp2(b_start - b[:lo]))
            vh.append(v[:lo])
        qt = jnp.concatenate(qt, axis=0).astype(BF16)
        kh = jnp.concatenate(kh, axis=0).astype(BF16)
        vh = jnp.concatenate(vh, axis=0).astype(BF16)
        a_off = [lax.dot_general(qt[:, _head(h)], kh[:, _head(h)], nt, preferred_element_type=F32)
                 for h in range(N_HEADS)]

        a_rep = []
        for j in range(N_HEADS // 2):
            p_pair = []
            for h in (2 * j, 2 * j + 1):
                kk_h = kk[:, _head(h)]
                b_h = b[:, _head(h)]
                rows = []
                for r in range(CHUNK):
                    lo = r - r % SUB
                    q_row = q_ref[h, pl.ds(r, SUB, stride=0), :]
                    b_row = b_ref[h, pl.ds(r, SUB, stride=0), :]
                    rows.append((q_row * kk_h[lo:lo + SUB])
                                * jnp.exp2(jnp.minimum(b_row - b_h[lo:lo + SUB], 0.0)))
                p_pair.append(jnp.concatenate(rows, axis=0).astype(BF16))
            a_rep.append(jnp.dot(jnp.concatenate(p_pair, axis=1), pair_ones,
                                 preferred_element_type=F32))

        o_off = jnp.concatenate(
            [jnp.dot(jnp.where(off_mask, a_off[h], 0.0).astype(BF16), vh[:, _head(h)],
                     preferred_element_type=F32) for h in range(N_HEADS)], axis=1)
        o_diag = []
        for j in range(N_HEADS // 2):
            pair = slice(2 * HEAD_W * j, 2 * HEAD_W * (j + 1))
            v_rep = jnp.broadcast_to(v[:, pair].reshape(N_SUB, 1, SUB, 2 * HEAD_W),
                                     (N_SUB, SUB, SUB, 2 * HEAD_W))
            w = (a_rep[j].reshape(N_SUB, SUB, SUB, 2 * HEAD_W) * v_rep).reshape(CHUNK * SUB, 2 * HEAD_W)
            o_diag.append(jnp.dot(seg_sum, w.astype(BF16), preferred_element_type=F32))
        o_intra = jnp.concatenate(o_diag, axis=1) + jnp.concatenate(
            [jnp.zeros((SUB, HG_WIDTH), F32), o_off], axis=0)
        o = o_inter + o_intra

        y = jnp.concatenate(
            [o[:, _head(h)] * lax.rsqrt(jnp.mean(o[:, _head(h)] * o[:, _head(h)], axis=-1, keepdims=True) + EPS)
             for h in range(N_HEADS)], axis=1)
        y = y * gn * (hg * _sigmoid(hg))
        o_ref[pl.ds(r0, CHUNK), :] = y.astype(BF16)
        return carry

    lax.fori_loop(0, n_chunks, chunk, 0)

    @pl.when(r == pl.num_programs(1) - 1)
    def _():
        st = st_ref[...]
        for h in range(N_HEADS):
            sout_ref[0, h] = st[:, _head(h)].T


def _gla(gates, lower_bounds, hg_norm, s0, n_seq, seq_len, rows):
    m = gates.shape[1]
    nr = seq_len // rows
    seg = lambda s: pl.BlockSpec((1, rows, HG_WIDTH), lambda b, r: (s, b * nr + r, 0))
    state = pl.BlockSpec((1, N_HEADS, HEAD_W, HEAD_W), lambda b, r: (b, 0, 0, 0))
    return pl.pallas_call(
        functools.partial(_gla_kernel, n_chunks=rows // CHUNK),
        grid=(n_seq, nr),
        in_specs=[
            seg(0), seg(1), seg(2), seg(3),
            pl.BlockSpec((lower_bounds.shape[0], HG_WIDTH), lambda b, r: (0, 0)),
            pl.BlockSpec((1, HEAD_W), lambda b, r: (0, 0)),
            state,
        ],
        out_specs=[pl.BlockSpec((rows, HG_WIDTH), lambda b, r: (b * nr + r, 0)), state],
        out_shape=[
            jax.ShapeDtypeStruct((m, HG_WIDTH), BF16),
            jax.ShapeDtypeStruct((n_seq, N_HEADS, HEAD_W, HEAD_W), F32),
        ],
        scratch_shapes=[pltpu.VMEM((HEAD_W, HG_WIDTH), F32),
                        pltpu.VMEM((N_HEADS, CHUNK, HEAD_W), F32),
                        pltpu.VMEM((N_HEADS, CHUNK, HEAD_W), F32)],
        compiler_params=pltpu.CompilerParams(
            dimension_semantics=("arbitrary", "arbitrary"), vmem_limit_bytes=VMEM_LIMIT),
        name="hgrn2",
    )(gates, gates, gates, gates, lower_bounds, hg_norm, s0)


def _stacked_query(q):
    lane = lax.broadcasted_iota(jnp.int32, q.shape, 1)
    qbig = jnp.concatenate([jnp.where(lane < ATT_DH, q, 0.0), jnp.where(lane >= ATT_DH, q, 0.0)], axis=0)
    return qbig.T.astype(BF16)


def _softmax_update(s, v_bf, m_ref, l_ref, acc_ref):
    m_prev = m_ref[...]
    m_new = jnp.maximum(m_prev, jnp.max(s, axis=0, keepdims=True))
    alpha = jnp.exp2(m_prev - m_new)
    p = jnp.exp2(s - m_new)
    l_ref[...] = alpha * l_ref[...] + jnp.sum(p, axis=0, keepdims=True)
    pv = lax.dot_general(v_bf, p.astype(BF16), (((0,), (0,)), ((), ())), preferred_element_type=F32)
    acc_ref[...] = alpha * acc_ref[...] + pv
    m_ref[...] = m_new


def _lambda(lq1_ref, lk1_ref, lq2_ref, lk2_ref):
    s1 = jnp.sum(lq1_ref[...] * lk1_ref[...], axis=-1, keepdims=True)
    s2 = jnp.sum(lq2_ref[...] * lk2_ref[...], axis=-1, keepdims=True)
    return jnp.exp(s1) - jnp.exp(s2) + LAM_INIT


def _attn_finish(acc, l, lam, sub, n):
    o_both = (acc * (1.0 / l)).T
    o = o_both[:n] - lam * o_both[n:]
    return _rmsnorm_rows(o, sub) * (1.0 - LAM_INIT)


def _attn_prompt_kernel(q_ref, k_ref, vt_ref, lq1_ref, lk1_ref, lq2_ref, lk2_ref, sub_ref, o_ref,
                        qq_ref, s_ref, smax_ref, mask_ref, m_ref, acc_ref, *, t, n_grp):
    qi = pl.program_id(1)
    grp = range(n_grp)
    for g in grp:
        qq_ref[g] = _stacked_query(q_ref[:, _head(g)].astype(F32))
    m_ref[...] = jnp.full(m_ref.shape, NEG_INF, F32)
    acc_ref[...] = jnp.zeros(acc_ref.shape, F32)
    ones_rows = jnp.ones((ONES_ROWS, t), BF16)

    def scores(g, kv):
        k0 = pl.multiple_of(kv * t, t)
        return jnp.dot(k_ref[pl.ds(k0, t), _head(g)], qq_ref[g], preferred_element_type=F32)

    def values_t(g, kv):
        k0 = pl.multiple_of(kv * t, t)
        return jnp.concatenate([vt_ref[_head(g), pl.ds(k0, t)], ones_rows], axis=0)

    def softmax(g, s, s_max):
        m_prev = m_ref[g]
        m_new = jnp.maximum(m_prev, s_max)
        m_ref[g] = m_new
        return jnp.exp2(s - m_new).astype(BF16), jnp.exp2(m_prev - m_new)

    def put_scores(g, buf, s):
        s_ref[g, buf] = s
        smax_ref[g, buf] = jnp.max(s, axis=0, keepdims=True)

    def stage(kv, cur, oth):
        k_next = pl.multiple_of((kv + 1) * t, t)
        vt = [values_t(g, kv) for g in grp]
        for c in range(0, 2 * t, STAGE_COLS):
            cols = slice(c, c + STAGE_COLS)
            for g in grp:
                s_next = jnp.dot(k_ref[pl.ds(k_next, t), _head(g)], qq_ref[g, :, cols],
                                 preferred_element_type=F32)
                s_ref[g, oth, :, cols] = s_next
                smax_ref[g, oth, :, cols] = jnp.max(s_next, axis=0, keepdims=True)
                m_prev = m_ref[g, :, cols]
                m_new = jnp.maximum(m_prev, smax_ref[g, cur, :, cols])
                m_ref[g, :, cols] = m_new
                p = jnp.exp2(s_ref[g, cur, :, cols] - m_new).astype(BF16)
                acc_ref[g, :, cols] = (jnp.exp2(m_prev - m_new) * acc_ref[g, :, cols]
                                       + jnp.dot(vt[g], p, preferred_element_type=F32))

    @pl.when((pl.program_id(0) == 0) & (qi == 0))
    def _():
        kpos = lax.broadcasted_iota(jnp.int32, mask_ref.shape, 0)
        col = lax.broadcasted_iota(jnp.int32, mask_ref.shape, 1)
        qpos = jnp.where(col >= t, col - t, col)
        mask_ref[...] = jnp.where((kpos >> 6) <= (qpos >> 6), 0.0, NEG_INF)

    def finish(cur):
        vt = [values_t(g, qi) for g in grp]
        acc = [[] for _ in grp]
        for c in range(0, 2 * t, FINISH_COLS):
            cols = slice(c, c + FINISH_COLS)
            for g in grp:
                s = s_ref[g, cur, :, cols] + mask_ref[:, cols]
                m_prev = m_ref[g, :, cols]
                m_new = jnp.maximum(m_prev, jnp.max(s, axis=0, keepdims=True))
                p = jnp.exp2(s - m_new).astype(BF16)
                acc[g].append(jnp.exp2(m_prev - m_new) * acc_ref[g, :, cols]
                              + jnp.dot(vt[g], p, preferred_element_type=F32))
        lam = _lambda(lq1_ref, lk1_ref, lq2_ref, lk2_ref)
        for g in grp:
            a = jnp.concatenate(acc[g], axis=1)
            o_ref[:, _head(g)] = _attn_finish(a[:HEAD_W], a[HEAD_W:HEAD_W + 1], lam, sub_ref[...],
                                              t).astype(BF16)

    s_first = [scores(g, 0) for g in grp]
    for g in grp:
        put_scores(g, 0, s_first[g])

    def pair(j, carry):
        stage(2 * j, 0, 1)
        stage(2 * j + 1, 1, 0)
        return carry

    lax.fori_loop(0, qi // 2, pair, 0)

    @pl.when(qi % 2 == 1)
    def _():
        stage(qi - 1, 0, 1)
        finish(1)

    @pl.when(qi % 2 == 0)
    def _():
        finish(0)


def _attn_prompt(q_bf, k_bf, vt_bf, lams, subln, t, n_grp):
    n = q_bf.shape[0]
    w = n_grp * HEAD_W
    small = lambda shape: pl.BlockSpec(shape, lambda h, i: (0, 0))
    return pl.pallas_call(
        functools.partial(_attn_prompt_kernel, t=t, n_grp=n_grp),
        grid=(N_HEADS // n_grp, n // t),
        in_specs=[
            pl.BlockSpec((t, w), lambda h, i: (i, h)),
            pl.BlockSpec((n, w), lambda h, i: (0, h)),
            pl.BlockSpec((w, n), lambda h, i: (h, 0)),
            small((1, ATT_DH)), small((1, ATT_DH)), small((1, ATT_DH)), small((1, ATT_DH)),
            small((1, HEAD_W)),
        ],
        out_specs=pl.BlockSpec((t, w), lambda h, i: (i, h)),
        out_shape=jax.ShapeDtypeStruct((n, ATT_WIDTH), BF16),
        scratch_shapes=[
            pltpu.VMEM((n_grp, HEAD_W, 2 * t), BF16),
            pltpu.VMEM((n_grp, 2, t, 2 * t), F32),
            pltpu.VMEM((n_grp, 2, 1, 2 * t), F32),
            pltpu.VMEM((t, 2 * t), F32),
            pltpu.VMEM((n_grp, 1, 2 * t), F32),
            pltpu.VMEM((n_grp, HEAD_W + ONES_ROWS, 2 * t), F32),
        ],
        compiler_params=pltpu.CompilerParams(
            dimension_semantics=("arbitrary", "arbitrary"), vmem_limit_bytes=VMEM_LIMIT),
        name="attn_prompt",
    )(q_bf, k_bf, vt_bf, *lams, subln)


def _attn_sample_kernel(q_ref, kc_ref, vc_ref, kn_ref, vn_ref, lq1_ref, lk1_ref, lq2_ref, lk2_ref,
                        sub_ref, o_ref, qq_ref, m_ref, l_ref, acc_ref, *, n_q, tk):
    t = pl.program_id(1)

    @pl.when(t == 0)
    def _():
        for h in range(N_HEADS):
            qq_ref[h] = _stacked_query(q_ref[:, _head(h)].astype(F32))
        m_ref[...] = jnp.full(m_ref.shape, NEG_INF, F32)
        l_ref[...] = jnp.zeros(l_ref.shape, F32)
        acc_ref[...] = jnp.zeros(acc_ref.shape, F32)

    def update(h, k_bf, v_bf):
        s = jnp.dot(k_bf, qq_ref[h], preferred_element_type=F32)
        _softmax_update(s, v_bf, m_ref.at[h], l_ref.at[h], acc_ref.at[h])

    for h in range(N_HEADS):
        rows = pl.ds(h, tk, stride=N_HEADS)
        update(h, kc_ref[0, rows, :].astype(BF16), vc_ref[0, rows, :].astype(BF16))

    @pl.when(t == pl.num_programs(1) - 1)
    def _():
        lam = _lambda(lq1_ref, lk1_ref, lq2_ref, lk2_ref)
        for h in range(N_HEADS):
            update(h, kn_ref[:, _head(h)], vn_ref[:, _head(h)])
            o_ref[:, _head(h)] = _attn_finish(acc_ref[h], l_ref[h], lam, sub_ref[...], n_q).astype(BF16)


def _attn_sample(q_bf, cache_k, cache_v, kn_bf, vn_bf, lams, subln, n_q, tk):
    n_b = cache_k.shape[0]
    past = cache_k.shape[1] // N_HEADS
    assert past % CHUNK == 0 and n_q <= CHUNK and past % tk == 0
    small = lambda shape: pl.BlockSpec(shape, lambda b, t: (0, 0))
    rows = pl.BlockSpec((n_q, ATT_WIDTH), lambda b, t: (b, 0))
    cache = pl.BlockSpec((1, tk * N_HEADS, HEAD_W), lambda b, t: (b, t, 0))
    return pl.pallas_call(
        functools.partial(_attn_sample_kernel, n_q=n_q, tk=tk),
        grid=(n_b, past // tk),
        in_specs=[
            rows, cache, cache, rows, rows,
            small((1, ATT_DH)), small((1, ATT_DH)), small((1, ATT_DH)), small((1, ATT_DH)),
            small((1, HEAD_W)),
        ],
        out_specs=rows,
        out_shape=jax.ShapeDtypeStruct((n_b * n_q, ATT_WIDTH), BF16),
        scratch_shapes=[
            pltpu.VMEM((N_HEADS, HEAD_W, 2 * n_q), BF16),
            pltpu.VMEM((N_HEADS, 1, 2 * n_q), F32),
            pltpu.VMEM((N_HEADS, 1, 2 * n_q), F32),
            pltpu.VMEM((N_HEADS, HEAD_W, 2 * n_q), F32),
        ],
        compiler_params=pltpu.CompilerParams(
            dimension_semantics=("arbitrary", "arbitrary"), vmem_limit_bytes=VMEM_LIMIT),
        name="attn_sample",
    )(q_bf, cache_k, cache_v, kn_bf, vn_bf, *lams, subln)


def _outproj_kernel(x_ref, a_ref, b_ref, w_ref, o_ref, wb_ref):
    @pl.when(pl.program_id(0) == 0)
    def _():
        wb_ref[...] = w_ref[...].astype(BF16)

    o_ref[...] = (x_ref[...]
                  + jnp.dot(a_ref[...], wb_ref[:HG_WIDTH, :], preferred_element_type=F32)
                  + jnp.dot(b_ref[...], wb_ref[HG_WIDTH:, :], preferred_element_type=F32))


def _outproj(x, mix_hg, mix_at, w, tm):
    m = x.shape[0]
    row = lambda i: (i, 0)
    return pl.pallas_call(
        _outproj_kernel,
        grid=(m // tm,),
        in_specs=[
            pl.BlockSpec((tm, D_MODEL), row),
            pl.BlockSpec((tm, HG_WIDTH), row),
            pl.BlockSpec((tm, ATT_WIDTH), row),
            pl.BlockSpec((HG_WIDTH + ATT_WIDTH, D_MODEL), lambda i: (0, 0), pipeline_mode=pl.Buffered(1)),
        ],
        out_specs=pl.BlockSpec((tm, D_MODEL), row),
        out_shape=jax.ShapeDtypeStruct((m, D_MODEL), F32),
        scratch_shapes=[pltpu.VMEM((HG_WIDTH + ATT_WIDTH, D_MODEL), BF16)],
        compiler_params=pltpu.CompilerParams(
            dimension_semantics=("arbitrary",), vmem_limit_bytes=VMEM_LIMIT),
        name="outproj",
    )(x, mix_hg, mix_at, w)


def _mlp_kernel(x_ref, gain_ref, wu_ref, wd_ref, gfin_ref, o_ref, h_ref):
    j = pl.program_id(1)

    @pl.when(j == 0)
    def _():
        x = x_ref[...]
        h_ref[...] = _rmsnorm_rows(x, gain_ref[...]).astype(BF16)
        o_ref[...] = x

    u = jnp.dot(h_ref[...], wu_ref[...].astype(BF16), preferred_element_type=F32)
    u = jnp.square(jnp.maximum(u, 0.0)).astype(BF16)
    o_ref[...] += jnp.dot(u, wd_ref[...].astype(BF16), preferred_element_type=F32)

    @pl.when(j == pl.num_programs(1) - 1)
    def _():
        o_ref[...] = _rmsnorm_rows(o_ref[...], gfin_ref[...])


def _mlp(x, gain, wu, wd, gfin, tm, tf):
    m = x.shape[0]
    return pl.pallas_call(
        _mlp_kernel,
        grid=(m // tm, D_FF // tf),
        in_specs=[
            pl.BlockSpec((tm, D_MODEL), lambda i, j: (i, 0)),
            pl.BlockSpec((1, D_MODEL), lambda i, j: (0, 0)),
            pl.BlockSpec((D_MODEL, tf), lambda i, j: (0, j)),
            pl.BlockSpec((tf, D_MODEL), lambda i, j: (j, 0)),
            pl.BlockSpec((1, D_MODEL), lambda i, j: (0, 0)),
        ],
        out_specs=pl.BlockSpec((tm, D_MODEL), lambda i, j: (i, 0)),
        out_shape=jax.ShapeDtypeStruct((m, D_MODEL), F32),
        scratch_shapes=[pltpu.VMEM((tm, D_MODEL), BF16)],
        compiler_params=pltpu.CompilerParams(
            dimension_semantics=("arbitrary", "arbitrary"), vmem_limit_bytes=VMEM_LIMIT),
        name="mlp",
    )(x, gain, wu, wd, gfin)


def kernel(x_prompt, x_sample, cache_k, cache_v, state_hgrn, norm_attn, w_in, lower_bounds, hg_norm,
           lambda_q1, lambda_k1, lambda_q2, lambda_k2, subln, w_out, norm_mlp, w_up, w_down, norm_final):
    depth = w_in.shape[0]
    assert depth == 1
    n_pb, n_p, _ = x_prompt.shape
    n_sb, n_s, _ = x_sample.shape
    assert n_pb == 1
    past = cache_k.shape[2]

    w_in_bf = w_in[0].astype(BF16)
    lams = (lambda_q1, lambda_k1, lambda_q2, lambda_k2)
    gfin = norm_final.reshape(1, D_MODEL)

    xp = x_prompt.reshape(n_p, D_MODEL)
    xs = x_sample.reshape(n_sb * n_s, D_MODEL)

    gates_p, q_p, kf_p, vf_p, kb_p, vtb_p = _inproj(xp, norm_attn, w_in_bf, tm=256, v_transposed=True)
    gates_s, q_s, kf_s, vf_s, kb_s, vb_s = _inproj(xs, norm_attn, w_in_bf, tm=256, v_transposed=False)

    s0_p = jnp.zeros((1, N_HEADS, HEAD_W, HEAD_W), F32)
    ohg_p, st_p = _gla(gates_p, lower_bounds, hg_norm, s0_p, n_seq=1, seq_len=n_p, rows=512)
    ohg_s, st_s = _gla(gates_s, lower_bounds, hg_norm, state_hgrn[0], n_seq=n_sb, seq_len=n_s, rows=n_s)

    oat_p = _attn_prompt(q_p, kb_p, vtb_p, lams, subln, t=512, n_grp=2)
    ck = cache_k[0].reshape(n_sb, past * N_HEADS, HEAD_W)
    cv = cache_v[0].reshape(n_sb, past * N_HEADS, HEAD_W)
    oat_s = _attn_sample(q_s, ck, cv, kb_s, vb_s, lams, subln, n_q=n_s, tk=1024)

    x1_p = _outproj(xp, ohg_p, oat_p, w_out[0], tm=512)
    x1_s = _outproj(xs, ohg_s, oat_s, w_out[0], tm=512)

    y_p = _mlp(x1_p, norm_mlp, w_up[0], w_down[0], gfin, tm=1024, tf=512)
    y_s = _mlp(x1_s, norm_mlp, w_up[0], w_down[0], gfin, tm=512, tf=512)

    return (
        y_p.reshape(n_pb, n_p, D_MODEL),
        y_s.reshape(n_sb, n_s, D_MODEL),
        kf_p.reshape(1, n_pb, n_p, N_HEADS, HEAD_W),
        vf_p.reshape(1, n_pb, n_p, N_HEADS, HEAD_W),
        st_p.reshape(1, n_pb, N_HEADS, HEAD_W, HEAD_W),
        kf_s.reshape(1, n_sb, n_s, N_HEADS, HEAD_W),
        vf_s.reshape(1, n_sb, n_s, N_HEADS, HEAD_W),
        st_s.reshape(1, n_sb, N_HEADS, HEAD_W, HEAD_W),
    )
```

```python
import functools
import math

import jax
import jax.numpy as jnp
from jax import lax
from jax.experimental import pallas as pl
from jax.experimental.pallas import tpu as pltpu

F32 = jnp.float32
BF16 = jnp.bfloat16

D_MODEL = 2048
HG_WIDTH = 1024
ATT_WIDTH = 1024
N_HEADS = 8
HEAD_W = 128
ATT_DH = 64
CHUNK = 64
SUB = 8
N_SUB = CHUNK // SUB
D_FF = 4 * D_MODEL
N_SEG = 7
EPS = 1e-6
NEG_INF = -1e30
LOG2E = 1.4426950408889634
Q_SCALE = ATT_DH ** -0.5 * LOG2E
LAM_INIT = 0.8 - 0.6 * math.exp(-0.3 * 0)
ONES_ROWS = 16
FINISH_COLS = 256
STAGE_COLS = 256

VMEM_LIMIT = 56 * 1024 * 1024


def _rmsnorm_rows(x, gain):
    return x * lax.rsqrt(jnp.mean(x * x, axis=-1, keepdims=True) + EPS) * gain


def _sigmoid(x):
    return 1.0 / (1.0 + jnp.exp(-x))


def _head(h):
    return slice(h * HEAD_W, (h + 1) * HEAD_W)


def _inproj_kernel(xp_ref, xs_ref, gain_ref, w_ref, gates_ref, q_ref, kb_ref, vt_ref, vbs_ref,
                   kfp_ref, vfp_ref, kfs_ref, vfs_ref, *, n_p):
    def project(x_ref, kf_ref, vf_ref, vb_ref):
        tm = x_ref.shape[0]
        h = _rmsnorm_rows(x_ref[...], gain_ref[...]).astype(BF16)

        def segment(s):
            return jnp.dot(h, w_ref[:, s * HG_WIDTH:(s + 1) * HG_WIDTH], preferred_element_type=F32)

        k = segment(5)
        v = segment(6)
        kb_ref[...] = k.astype(BF16)
        vt_ref[...] = v.T.astype(BF16)
        if vb_ref is not None:
            vb_ref[...] = v.astype(BF16)
        for hd in range(N_HEADS):
            kf_ref[pl.ds(hd, tm, stride=N_HEADS), :] = k[:, _head(hd)]
            vf_ref[pl.ds(hd, tm, stride=N_HEADS), :] = v[:, _head(hd)]
        q_ref[...] = (segment(4) * Q_SCALE).astype(BF16)
        for s in range(4):
            gates_ref[s] = segment(s)

    is_prompt = pl.program_id(0) < n_p

    @pl.when(is_prompt)
    def _():
        project(xp_ref, kfp_ref, vfp_ref, None)

    @pl.when(jnp.logical_not(is_prompt))
    def _():
        project(xs_ref, kfs_ref, vfs_ref, vbs_ref)


def _inproj(xp, xs, gain, w_bf, tm):
    m_p, m_s = xp.shape[0], xs.shape[0]
    n_p, n_s = m_p // tm, m_s // tm
    m = m_p + m_s
    seg = HG_WIDTH
    row = lambda i: (i, 0)
    p_row = lambda i: (jnp.minimum(i, n_p - 1), 0)
    s_row = lambda i: (jnp.maximum(i - n_p, 0), 0)
    once = dict(pipeline_mode=pl.Buffered(1))
    return pl.pallas_call(
        functools.partial(_inproj_kernel, n_p=n_p),
        grid=(n_p + n_s,),
        in_specs=[
            pl.BlockSpec((tm, D_MODEL), p_row),
            pl.BlockSpec((tm, D_MODEL), s_row, **once),
            pl.BlockSpec((1, D_MODEL), lambda i: (0, 0)),
            pl.BlockSpec((D_MODEL, N_SEG * seg), lambda i: (0, 0), **once),
        ],
        out_specs=[
            pl.BlockSpec((4, tm, seg), lambda i: (0, i, 0)),
            pl.BlockSpec((tm, seg), row),
            pl.BlockSpec((tm, seg), row),
            pl.BlockSpec((seg, tm), lambda i: (0, i)),
            pl.BlockSpec((tm, seg), s_row, **once),
            pl.BlockSpec((tm * N_HEADS, HEAD_W), p_row),
            pl.BlockSpec((tm * N_HEADS, HEAD_W), p_row),
            pl.BlockSpec((tm * N_HEADS, HEAD_W), s_row, **once),
            pl.BlockSpec((tm * N_HEADS, HEAD_W), s_row, **once),
        ],
        out_shape=[
            jax.ShapeDtypeStruct((4, m, seg), F32),
            jax.ShapeDtypeStruct((m, seg), BF16),
            jax.ShapeDtypeStruct((m, seg), BF16),
            jax.ShapeDtypeStruct((seg, m), BF16),
            jax.ShapeDtypeStruct((m_s, seg), BF16),
            jax.ShapeDtypeStruct((m_p * N_HEADS, HEAD_W), F32),
            jax.ShapeDtypeStruct((m_p * N_HEADS, HEAD_W), F32),
            jax.ShapeDtypeStruct((m_s * N_HEADS, HEAD_W), F32),
            jax.ShapeDtypeStruct((m_s * N_HEADS, HEAD_W), F32),
        ],
        compiler_params=pltpu.CompilerParams(
            dimension_semantics=("arbitrary",), vmem_limit_bytes=VMEM_LIMIT),
        name="inproj",
    )(xp, xs, gain, w_bf)


def _gla_kernel(hq_ref, hf_ref, hi_ref, hg_ref, lbnd_ref, gn_ref, s0_ref, o_ref, sout_ref, st_ref,
                q_ref, b_ref, *, n_chunks):
    r = pl.program_id(1)

    @pl.when(r == 0)
    def _():
        st_ref[...] = jnp.concatenate([s0_ref[0, h].T for h in range(N_HEADS)], axis=1)

    lbs = lbnd_ref[...]
    e = jnp.exp(lbs - jnp.max(lbs, axis=0, keepdims=True))
    lb = e[0:1, :] / jnp.sum(e, axis=0, keepdims=True)
    gn = jnp.concatenate([gn_ref[...]] * N_HEADS, axis=1)

    ri = lax.broadcasted_iota(jnp.int32, (CHUNK, CHUNK), 0)
    ci = lax.broadcasted_iota(jnp.int32, (CHUNK, CHUNK), 1)
    tril = (ri >= ci).astype(F32)
    pr = lax.broadcasted_iota(jnp.int32, (2 * HEAD_W, 2 * HEAD_W), 0)
    pc = lax.broadcasted_iota(jnp.int32, (2 * HEAD_W, 2 * HEAD_W), 1)
    pair_ones = ((pr >> 7) == (pc >> 7)).astype(BF16)
    gr = lax.broadcasted_iota(jnp.int32, (CHUNK, CHUNK * SUB), 0)
    gc = lax.broadcasted_iota(jnp.int32, (CHUNK, CHUNK * SUB), 1)
    seg_sum = (((gc >> 3) == gr) & ((gc & (SUB - 1)) <= (gr & (SUB - 1)))).astype(BF16)
    n_off = (N_SUB - 1) * SUB
    n_key = SUB * (N_SUB * (N_SUB - 1) // 2)
    orow = lax.broadcasted_iota(jnp.int32, (n_off, n_key), 0) >> 3
    ocol = lax.broadcasted_iota(jnp.int32, (n_off, n_key), 1)
    ocol_seg = sum((ocol >= 4 * i * (i - 1)).astype(jnp.int32) for i in range(2, N_SUB))
    off_mask = orow == ocol_seg
    nt = (((1,), (1,)), ((), ()))
    tn = (((0,), (0,)), ((), ()))

    def chunk(c, carry):
        r0 = pl.multiple_of(c * CHUNK, CHUNK)
        hq = hq_ref[0, pl.ds(r0, CHUNK), :]
        hf = hf_ref[0, pl.ds(r0, CHUNK), :]
        v = hi_ref[0, pl.ds(r0, CHUNK), :]
        hg = hg_ref[0, pl.ds(r0, CHUNK), :]

        f = lb + (1.0 - lb) * _sigmoid(hf)
        g = jnp.log(f) * LOG2E
        kk = 1.0 - f
        q = hq * _sigmoid(hq)
        b = jnp.dot(tril, g, precision=lax.Precision.HIGHEST, preferred_element_type=F32)
        v_bf = v.astype(BF16)
        for h in range(N_HEADS):
            q_ref[h] = q[:, _head(h)]
            b_ref[h] = b[:, _head(h)]

        st = st_ref[...]
        st_bf = st.astype(BF16)
        b_last = b[CHUNK - 1:CHUNK, :]
        q_in = (q * jnp.exp2(b)).astype(BF16)
        k_dec = (kk * jnp.exp2(b_last - b)).astype(BF16)
        o_inter = jnp.concatenate(
            [lax.dot_general(q_in[:, _head(h)], st_bf[:, _head(h)], nt, preferred_element_type=F32)
             for h in range(N_HEADS)], axis=1)
        upd = jnp.concatenate(
            [lax.dot_general(v_bf[:, _head(h)], k_dec[:, _head(h)], tn, preferred_element_type=F32)
             for h in range(N_HEADS)], axis=1)
        st_ref[...] = jnp.exp2(b_last) * st + upd

        qt, kh, vh = [], [], []
        for i in range(1, N_SUB):
            lo = i * SUB
            b_start = b[lo - 1:lo, :]
            qt.append(q[lo:lo + SUB] * jnp.exp2(b[lo:lo + SUB] - b_start))
            kh.append(kk[:lo] * jnp.exp2(b_start - b[:lo]))
            vh.append(v[:lo])
        qt = jnp.concatenate(qt, axis=0).astype(BF16)
        kh = jnp.concatenate(kh, axis=0).astype(BF16)
        vh = jnp.concatenate(vh, axis=0).astype(BF16)
        a_off = [lax.dot_general(qt[:, _head(h)], kh[:, _head(h)], nt, preferred_element_type=F32)
                 for h in range(N_HEADS)]

        a_rep = []
        for j in range(N_HEADS // 2):
            p_pair = []
            for h in (2 * j, 2 * j + 1):
                kk_h = kk[:, _head(h)]
                b_h = b[:, _head(h)]
                rows = []
                for r in range(CHUNK):
                    lo = r - r % SUB
                    q_row = q_ref[h, pl.ds(r, SUB, stride=0), :]
                    b_row = b_ref[h, pl.ds(r, SUB, stride=0), :]
                    rows.append((q_row * kk_h[lo:lo + SUB])
                                * jnp.exp2(jnp.minimum(b_row - b_h[lo:lo + SUB], 0.0)))
                p_pair.append(jnp.concatenate(rows, axis=0).astype(BF16))
            a_rep.append(jnp.dot(jnp.concatenate(p_pair, axis=1), pair_ones,
                                 preferred_element_type=F32))

        o_off = jnp.concatenate(
            [jnp.dot(jnp.where(off_mask, a_off[h], 0.0).astype(BF16), vh[:, _head(h)],
                     preferred_element_type=F32) for h in range(N_HEADS)], axis=1)
        o_diag = []
        for j in range(N_HEADS // 2):
            pair = slice(2 * HEAD_W * j, 2 * HEAD_W * (j + 1))
            v_rep = jnp.broadcast_to(v[:, pair].reshape(N_SUB, 1, SUB, 2 * HEAD_W),
                                     (N_SUB, SUB, SUB, 2 * HEAD_W))
            w = (a_rep[j].reshape(N_SUB, SUB, SUB, 2 * HEAD_W) * v_rep).reshape(CHUNK * SUB, 2 * HEAD_W)
            o_diag.append(jnp.dot(seg_sum, w.astype(BF16), preferred_element_type=F32))
        o_intra = jnp.concatenate(o_diag, axis=1) + jnp.concatenate(
            [jnp.zeros((SUB, HG_WIDTH), F32), o_off], axis=0)
        o = o_inter + o_intra

        y = jnp.concatenate(
            [o[:, _head(h)] * lax.rsqrt(jnp.mean(o[:, _head(h)] * o[:, _head(h)], axis=-1, keepdims=True) + EPS)
             for h in range(N_HEADS)], axis=1)
        y = y * gn * (hg * _sigmoid(hg))
        o_ref[pl.ds(r0, CHUNK), :] = y.astype(BF16)
        return carry

    lax.fori_loop(0, n_chunks, chunk, 0)

    @pl.when(r == pl.num_programs(1) - 1)
    def _():
        st = st_ref[...]
        for h in range(N_HEADS):
            sout_ref[0, h] = st[:, _head(h)].T


def _gla(gates, lower_bounds, hg_norm, s0, row0, n_seq, seq_len, rows):
    m = n_seq * seq_len
    nr = seq_len // rows
    blk0 = row0 // rows
    seg = lambda s: pl.BlockSpec((1, rows, HG_WIDTH), lambda b, r: (s, blk0 + b * nr + r, 0))
    state = pl.BlockSpec((1, N_HEADS, HEAD_W, HEAD_W), lambda b, r: (b, 0, 0, 0))
    return pl.pallas_call(
        functools.partial(_gla_kernel, n_chunks=rows // CHUNK),
        grid=(n_seq, nr),
        in_specs=[
            seg(0), seg(1), seg(2), seg(3),
            pl.BlockSpec((lower_bounds.shape[0], HG_WIDTH), lambda b, r: (0, 0)),
            pl.BlockSpec((1, HEAD_W), lambda b, r: (0, 0)),
            state,
        ],
        out_specs=[pl.BlockSpec((rows, HG_WIDTH), lambda b, r: (b * nr + r, 0)), state],
        out_shape=[
            jax.ShapeDtypeStruct((m, HG_WIDTH), BF16),
            jax.ShapeDtypeStruct((n_seq, N_HEADS, HEAD_W, HEAD_W), F32),
        ],
        scratch_shapes=[pltpu.VMEM((HEAD_W, HG_WIDTH), F32),
                        pltpu.VMEM((N_HEADS, CHUNK, HEAD_W), F32),
                        pltpu.VMEM((N_HEADS, CHUNK, HEAD_W), F32)],
        compiler_params=pltpu.CompilerParams(
            dimension_semantics=("arbitrary", "arbitrary"), vmem_limit_bytes=VMEM_LIMIT),
        name="hgrn2",
    )(gates, gates, gates, gates, lower_bounds, hg_norm, s0)


def _stacked_query(q):
    lane = lax.broadcasted_iota(jnp.int32, q.shape, 1)
    qbig = jnp.concatenate([jnp.where(lane < ATT_DH, q, 0.0), jnp.where(lane >= ATT_DH, q, 0.0)], axis=0)
    return qbig.T.astype(BF16)


def _softmax_update(s, v_bf, m_ref, l_ref, acc_ref):
    m_prev = m_ref[...]
    m_new = jnp.maximum(m_prev, jnp.max(s, axis=0, keepdims=True))
    alpha = jnp.exp2(m_prev - m_new)
    p = jnp.exp2(s - m_new)
    l_ref[...] = alpha * l_ref[...] + jnp.sum(p, axis=0, keepdims=True)
    pv = lax.dot_general(v_bf, p.astype(BF16), (((0,), (0,)), ((), ())), preferred_element_type=F32)
    acc_ref[...] = alpha * acc_ref[...] + pv
    m_ref[...] = m_new


def _lambda(lq1_ref, lk1_ref, lq2_ref, lk2_ref):
    s1 = jnp.sum(lq1_ref[...] * lk1_ref[...], axis=-1, keepdims=True)
    s2 = jnp.sum(lq2_ref[...] * lk2_ref[...], axis=-1, keepdims=True)
    return jnp.exp(s1) - jnp.exp(s2) + LAM_INIT


def _attn_finish(acc, l, lam, sub, n):
    o_both = (acc * (1.0 / l)).T
    o = o_both[:n] - lam * o_both[n:]
    return _rmsnorm_rows(o, sub) * (1.0 - LAM_INIT)


def _attn_prompt_kernel(q_ref, k_ref, vt_ref, lq1_ref, lk1_ref, lq2_ref, lk2_ref, sub_ref, o_ref,
                        qq_ref, s_ref, smax_ref, mask_ref, m_ref, acc_ref, *, t, n_grp):
    qi = pl.program_id(1)
    grp = range(n_grp)
    for g in grp:
        qq_ref[g] = _stacked_query(q_ref[:, _head(g)].astype(F32))
    m_ref[...] = jnp.full(m_ref.shape, NEG_INF, F32)
    acc_ref[...] = jnp.zeros(acc_ref.shape, F32)
    ones_rows = jnp.ones((ONES_ROWS, t), BF16)

    def scores(g, kv):
        k0 = pl.multiple_of(kv * t, t)
        return jnp.dot(k_ref[pl.ds(k0, t), _head(g)], qq_ref[g], preferred_element_type=F32)

    def values_t(g, kv):
        k0 = pl.multiple_of(kv * t, t)
        return jnp.concatenate([vt_ref[_head(g), pl.ds(k0, t)], ones_rows], axis=0)

    def softmax(g, s, s_max):
        m_prev = m_ref[g]
        m_new = jnp.maximum(m_prev, s_max)
        m_ref[g] = m_new
        return jnp.exp2(s - m_new).astype(BF16), jnp.exp2(m_prev - m_new)

    def put_scores(g, buf, s):
        s_ref[g, buf] = s
        smax_ref[g, buf] = jnp.max(s, axis=0, keepdims=True)

    def stage(kv, cur, oth):
        k_next = pl.multiple_of((kv + 1) * t, t)
        vt = [values_t(g, kv) for g in grp]
        for c in range(0, 2 * t, STAGE_COLS):
            cols = slice(c, c + STAGE_COLS)
            for g in grp:
                s_next = jnp.dot(k_ref[pl.ds(k_next, t), _head(g)], qq_ref[g, :, cols],
                                 preferred_element_type=F32)
                s_ref[g, oth, :, cols] = s_next
                smax_ref[g, oth, :, cols] = jnp.max(s_next, axis=0, keepdims=True)
                m_prev = m_ref[g, :, cols]
                m_new = jnp.maximum(m_prev, smax_ref[g, cur, :, cols])
                m_ref[g, :, cols] = m_new
                p = jnp.exp2(s_ref[g, cur, :, cols] - m_new).astype(BF16)
                acc_ref[g, :, cols] = (jnp.exp2(m_prev - m_new) * acc_ref[g, :, cols]
                                       + jnp.dot(vt[g], p, preferred_element_type=F32))

    @pl.when((pl.program_id(0) == 0) & (qi == 0))
    def _():
        kpos = lax.broadcasted_iota(jnp.int32, mask_ref.shape, 0)
        col = lax.broadcasted_iota(jnp.int32, mask_ref.shape, 1)
        qpos = jnp.where(col >= t, col - t, col)
        mask_ref[...] = jnp.where((kpos >> 6) <= (qpos >> 6), 0.0, NEG_INF)

    def finish(cur):
        vt = [values_t(g, qi) for g in grp]
        acc = [[] for _ in grp]
        for c in range(0, 2 * t, FINISH_COLS):
            cols = slice(c, c + FINISH_COLS)
            for g in grp:
                s = s_ref[g, cur, :, cols] + mask_ref[:, cols]
                m_prev = m_ref[g, :, cols]
                m_new = jnp.maximum(m_prev, jnp.max(s, axis=0, keepdims=True))
                p = jnp.exp2(s - m_new).astype(BF16)
                acc[g].append(jnp.exp2(m_prev - m_new) * acc_ref[g, :, cols]
                              + jnp.dot(vt[g], p, preferred_element_type=F32))
        lam = _lambda(lq1_ref, lk1_ref, lq2_ref, lk2_ref)
        for g in grp:
            a = jnp.concatenate(acc[g], axis=1)
            o_ref[:, _head(g)] = _attn_finish(a[:HEAD_W], a[HEAD_W:HEAD_W + 1], lam, sub_ref[...],
                                              t).astype(BF16)

    s_first = [scores(g, 0) for g in grp]
    for g in grp:
        put_scores(g, 0, s_first[g])

    def pair(j, carry):
        stage(2 * j, 0, 1)
        stage(2 * j + 1, 1, 0)
        return carry

    lax.fori_loop(0, qi // 2, pair, 0)

    @pl.when(qi % 2 == 1)
    def _():
        stage(qi - 1, 0, 1)
        finish(1)

    @pl.when(qi % 2 == 0)
    def _():
        finish(0)


def _attn_prompt(q_bf, k_bf, vt_bf, lams, subln, n, t, n_grp):
    w = n_grp * HEAD_W
    small = lambda shape: pl.BlockSpec(shape, lambda h, i: (0, 0))
    return pl.pallas_call(
        functools.partial(_attn_prompt_kernel, t=t, n_grp=n_grp),
        grid=(N_HEADS // n_grp, n // t),
        in_specs=[
            pl.BlockSpec((t, w), lambda h, i: (i, h)),
            pl.BlockSpec((n, w), lambda h, i: (0, h)),
            pl.BlockSpec((w, n), lambda h, i: (h, 0)),
            small((1, ATT_DH)), small((1, ATT_DH)), small((1, ATT_DH)), small((1, ATT_DH)),
            small((1, HEAD_W)),
        ],
        out_specs=pl.BlockSpec((t, w), lambda h, i: (i, h)),
        out_shape=jax.ShapeDtypeStruct((n, ATT_WIDTH), BF16),
        scratch_shapes=[
            pltpu.VMEM((n_grp, HEAD_W, 2 * t), BF16),
            pltpu.VMEM((n_grp, 2, t, 2 * t), F32),
            pltpu.VMEM((n_grp, 2, 1, 2 * t), F32),
            pltpu.VMEM((t, 2 * t), F32),
            pltpu.VMEM((n_grp, 1, 2 * t), F32),
            pltpu.VMEM((n_grp, HEAD_W + ONES_ROWS, 2 * t), F32),
        ],
        compiler_params=pltpu.CompilerParams(
            dimension_semantics=("arbitrary", "arbitrary"), vmem_limit_bytes=VMEM_LIMIT),
        name="attn_prompt",
    )(q_bf, k_bf, vt_bf, *lams, subln)


def _attn_sample_kernel(q_ref, kc_ref, vc_ref, kn_ref, vn_ref, lq1_ref, lk1_ref, lq2_ref, lk2_ref,
                        sub_ref, o_ref, qq_ref, m_ref, l_ref, acc_ref, *, n_q, tk):
    t = pl.program_id(1)

    @pl.when(t == 0)
    def _():
        for h in range(N_HEADS):
            qq_ref[h] = _stacked_query(q_ref[:, _head(h)].astype(F32))
        m_ref[...] = jnp.full(m_ref.shape, NEG_INF, F32)
        l_ref[...] = jnp.zeros(l_ref.shape, F32)
        acc_ref[...] = jnp.zeros(acc_ref.shape, F32)

    def update(h, k_bf, v_bf):
        s = jnp.dot(k_bf, qq_ref[h], preferred_element_type=F32)
        _softmax_update(s, v_bf, m_ref.at[h], l_ref.at[h], acc_ref.at[h])

    for h in range(N_HEADS):
        rows = pl.ds(h, tk, stride=N_HEADS)
        update(h, kc_ref[0, rows, :].astype(BF16), vc_ref[0, rows, :].astype(BF16))

    @pl.when(t == pl.num_programs(1) - 1)
    def _():
        lam = _lambda(lq1_ref, lk1_ref, lq2_ref, lk2_ref)
        for h in range(N_HEADS):
            update(h, kn_ref[:, _head(h)], vn_ref[:, _head(h)])
            o_ref[:, _head(h)] = _attn_finish(acc_ref[h], l_ref[h], lam, sub_ref[...], n_q).astype(BF16)


def _attn_sample(q_bf, cache_k, cache_v, kn_bf, vn_bf, lams, subln, row0, n_q, tk):
    n_b = cache_k.shape[0]
    past = cache_k.shape[1] // N_HEADS
    assert past % CHUNK == 0 and n_q <= CHUNK and past % tk == 0 and row0 % n_q == 0
    blk0 = row0 // n_q
    small = lambda shape: pl.BlockSpec(shape, lambda b, t: (0, 0))
    rows = pl.BlockSpec((n_q, ATT_WIDTH), lambda b, t: (b, 0))
    rows_all = pl.BlockSpec((n_q, ATT_WIDTH), lambda b, t: (blk0 + b, 0))
    cache = pl.BlockSpec((1, tk * N_HEADS, HEAD_W), lambda b, t: (b, t, 0))
    return pl.pallas_call(
        functools.partial(_attn_sample_kernel, n_q=n_q, tk=tk),
        grid=(n_b, past // tk),
        in_specs=[
            rows_all, cache, cache, rows_all, rows,
            small((1, ATT_DH)), small((1, ATT_DH)), small((1, ATT_DH)), small((1, ATT_DH)),
            small((1, HEAD_W)),
        ],
        out_specs=rows,
        out_shape=jax.ShapeDtypeStruct((n_b * n_q, ATT_WIDTH), BF16),
        scratch_shapes=[
            pltpu.VMEM((N_HEADS, HEAD_W, 2 * n_q), BF16),
            pltpu.VMEM((N_HEADS, 1, 2 * n_q), F32),
            pltpu.VMEM((N_HEADS, 1, 2 * n_q), F32),
            pltpu.VMEM((N_HEADS, HEAD_W, 2 * n_q), F32),
        ],
        compiler_params=pltpu.CompilerParams(
            dimension_semantics=("arbitrary", "arbitrary"), vmem_limit_bytes=VMEM_LIMIT),
        name="attn_sample",
    )(q_bf, cache_k, cache_v, kn_bf, vn_bf, *lams, subln)


def _outproj_kernel(x_ref, a_ref, b_ref, w_ref, o_ref, wb_ref):
    @pl.when(pl.program_id(0) == 0)
    def _():
        wb_ref[...] = w_ref[...].astype(BF16)

    o_ref[...] = (x_ref[...]
                  + jnp.dot(a_ref[...], wb_ref[:HG_WIDTH, :], preferred_element_type=F32)
                  + jnp.dot(b_ref[...], wb_ref[HG_WIDTH:, :], preferred_element_type=F32))


def _outproj(x, mix_hg, mix_at, w, tm):
    m = x.shape[0]
    row = lambda i: (i, 0)
    return pl.pallas_call(
        _outproj_kernel,
        grid=(m // tm,),
        in_specs=[
            pl.BlockSpec((tm, D_MODEL), row),
            pl.BlockSpec((tm, HG_WIDTH), row),
            pl.BlockSpec((tm, ATT_WIDTH), row),
            pl.BlockSpec((HG_WIDTH + ATT_WIDTH, D_MODEL), lambda i: (0, 0), pipeline_mode=pl.Buffered(1)),
        ],
        out_specs=pl.BlockSpec((tm, D_MODEL), row),
        out_shape=jax.ShapeDtypeStruct((m, D_MODEL), F32),
        scratch_shapes=[pltpu.VMEM((HG_WIDTH + ATT_WIDTH, D_MODEL), BF16)],
        compiler_params=pltpu.CompilerParams(
            dimension_semantics=("arbitrary",), vmem_limit_bytes=VMEM_LIMIT),
        name="outproj",
    )(x, mix_hg, mix_at, w)


def _mlp_kernel(x_ref, gain_ref, wu_ref, wd_ref, gfin_ref, o_ref, h_ref):
    j = pl.program_id(1)

    @pl.when(j == 0)
    def _():
        x = x_ref[...]
        h_ref[...] = _rmsnorm_rows(x, gain_ref[...]).astype(BF16)
        o_ref[...] = x

    u = jnp.dot(h_ref[...], wu_ref[...].astype(BF16), preferred_element_type=F32)
    u = jnp.square(jnp.maximum(u, 0.0)).astype(BF16)
    o_ref[...] += jnp.dot(u, wd_ref[...].astype(BF16), preferred_element_type=F32)

    @pl.when(j == pl.num_programs(1) - 1)
    def _():
        o_ref[...] = _rmsnorm_rows(o_ref[...], gfin_ref[...])


def _mlp(x, gain, wu, wd, gfin, tm, tf):
    m = x.shape[0]
    return pl.pallas_call(
        _mlp_kernel,
        grid=(m // tm, D_FF // tf),
        in_specs=[
            pl.BlockSpec((tm, D_MODEL), lambda i, j: (i, 0)),
            pl.BlockSpec((1, D_MODEL), lambda i, j: (0, 0)),
            pl.BlockSpec((D_MODEL, tf), lambda i, j: (0, j)),
            pl.BlockSpec((tf, D_MODEL), lambda i, j: (j, 0)),
            pl.BlockSpec((1, D_MODEL), lambda i, j: (0, 0)),
        ],
        out_specs=pl.BlockSpec((tm, D_MODEL), lambda i, j: (i, 0)),
        out_shape=jax.ShapeDtypeStruct((m, D_MODEL), F32),
        scratch_shapes=[pltpu.VMEM((tm, D_MODEL), BF16)],
        compiler_params=pltpu.CompilerParams(
            dimension_semantics=("arbitrary", "arbitrary"), vmem_limit_bytes=VMEM_LIMIT),
        name="mlp",
    )(x, gain, wu, wd, gfin)


def kernel(x_prompt, x_sample, cache_k, cache_v, state_hgrn, norm_attn, w_in, lower_bounds, hg_norm,
           lambda_q1, lambda_k1, lambda_q2, lambda_k2, subln, w_out, norm_mlp, w_up, w_down, norm_final):
    depth = w_in.shape[0]
    assert depth == 1
    n_pb, n_p, _ = x_prompt.shape
    n_sb, n_s, _ = x_sample.shape
    assert n_pb == 1
    past = cache_k.shape[2]

    w_in_bf = w_in[0].astype(BF16)
    lams = (lambda_q1, lambda_k1, lambda_q2, lambda_k2)
    gfin = norm_final.reshape(1, D_MODEL)

    xp = x_prompt.reshape(n_p, D_MODEL)
    xs = x_sample.reshape(n_sb * n_s, D_MODEL)

    gates, q_bf, k_bf, vt_bf, vs_bf, kf_p, vf_p, kf_s, vf_s = _inproj(xp, xs, norm_attn, w_in_bf, tm=256)

    s0_p = jnp.zeros((1, N_HEADS, HEAD_W, HEAD_W), F32)
    ohg_p, st_p = _gla(gates, lower_bounds, hg_norm, s0_p, row0=0, n_seq=1, seq_len=n_p, rows=512)
    ohg_s, st_s = _gla(gates, lower_bounds, hg_norm, state_hgrn[0], row0=n_p, n_seq=n_sb, seq_len=n_s, rows=n_s)

    oat_p = _attn_prompt(q_bf, k_bf, vt_bf, lams, subln, n=n_p, t=512, n_grp=2)
    ck = cache_k[0].reshape(n_sb, past * N_HEADS, HEAD_W)
    cv = cache_v[0].reshape(n_sb, past * N_HEADS, HEAD_W)
    oat_s = _attn_sample(q_bf, ck, cv, k_bf, vs_bf, lams, subln, row0=n_p, n_q=n_s, tk=1024)

    x1_p = _outproj(xp, ohg_p, oat_p, w_out[0], tm=512)
    x1_s = _outproj(xs, ohg_s, oat_s, w_out[0], tm=512)

    y_p = _mlp(x1_p, norm_mlp, w_up[0], w_down[0], gfin, tm=1024, tf=512)
    y_s = _mlp(x1_s, norm_mlp, w_up[0], w_down[0], gfin, tm=512, tf=512)

    return (
        y_p.reshape(n_pb, n_p, D_MODEL),
        y_s.reshape(n_sb, n_s, D_MODEL),
        kf_p.reshape(1, n_pb, n_p, N_HEADS, HEAD_W),
        vf_p.reshape(1, n_pb, n_p, N_HEADS, HEAD_W),
        st_p.reshape(1, n_pb, N_HEADS, HEAD_W, HEAD_W),
        kf_s.reshape(1, n_sb, n_s, N_HEADS, HEAD_W),
        vf_s.reshape(1, n_sb, n_s, N_HEADS, HEAD_W),
        st_s.reshape(1, n_sb, N_HEADS, HEAD_W, HEAD_W),
    )
```

```python
import functools
import math

import jax
import jax.numpy as jnp
from jax import lax
from jax.experimental import pallas as pl
from jax.experimental.pallas import tpu as pltpu

F32 = jnp.float32
BF16 = jnp.bfloat16

D_MODEL = 2048
HG_WIDTH = 1024
ATT_WIDTH = 1024
N_HEADS = 8
HEAD_W = 128
ATT_DH = 64
CHUNK = 64
SUB = 8
N_SUB = CHUNK // SUB
MIN_FACTORED_LB = 2.0 ** (-100.0 / SUB)
D_FF = 4 * D_MODEL
N_SEG = 7
EPS = 1e-6
NEG_INF = -1e30
LOG2E = 1.4426950408889634
Q_SCALE = ATT_DH ** -0.5 * LOG2E
LAM_INIT = 0.8 - 0.6 * math.exp(-0.3 * 0)
ONES_ROWS = 16
FINISH_COLS = 256
STAGE_COLS = 256

VMEM_LIMIT = 56 * 1024 * 1024


def _rmsnorm_rows(x, gain):
    return x * lax.rsqrt(jnp.mean(x * x, axis=-1, keepdims=True) + EPS) * gain


def _sigmoid(x):
    return 1.0 / (1.0 + jnp.exp(-x))


def _head(h):
    return slice(h * HEAD_W, (h + 1) * HEAD_W)


def _inproj_kernel(xp_ref, xs_ref, gain_ref, w_ref, gates_ref, q_ref, kb_ref, vt_ref, vbs_ref,
                   kfp_ref, vfp_ref, kfs_ref, vfs_ref, *, n_p):
    def project(x_ref, kf_ref, vf_ref, vb_ref):
        tm = x_ref.shape[0]
        h = _rmsnorm_rows(x_ref[...], gain_ref[...]).astype(BF16)

        def segment(s):
            return jnp.dot(h, w_ref[:, s * HG_WIDTH:(s + 1) * HG_WIDTH], preferred_element_type=F32)

        k = segment(5)
        v = segment(6)
        kb_ref[...] = k.astype(BF16)
        vt_ref[...] = v.T.astype(BF16)
        if vb_ref is not None:
            vb_ref[...] = v.astype(BF16)
        for hd in range(N_HEADS):
            kf_ref[pl.ds(hd, tm, stride=N_HEADS), :] = k[:, _head(hd)]
            vf_ref[pl.ds(hd, tm, stride=N_HEADS), :] = v[:, _head(hd)]
        q_ref[...] = (segment(4) * Q_SCALE).astype(BF16)
        for s in range(4):
            gates_ref[s] = segment(s)

    is_prompt = pl.program_id(0) < n_p

    @pl.when(is_prompt)
    def _():
        project(xp_ref, kfp_ref, vfp_ref, None)

    @pl.when(jnp.logical_not(is_prompt))
    def _():
        project(xs_ref, kfs_ref, vfs_ref, vbs_ref)


def _inproj(xp, xs, gain, w_bf, tm):
    m_p, m_s = xp.shape[0], xs.shape[0]
    n_p, n_s = m_p // tm, m_s // tm
    m = m_p + m_s
    seg = HG_WIDTH
    row = lambda i: (i, 0)
    p_row = lambda i: (jnp.minimum(i, n_p - 1), 0)
    s_row = lambda i: (jnp.maximum(i - n_p, 0), 0)
    once = dict(pipeline_mode=pl.Buffered(1))
    return pl.pallas_call(
        functools.partial(_inproj_kernel, n_p=n_p),
        grid=(n_p + n_s,),
        in_specs=[
            pl.BlockSpec((tm, D_MODEL), p_row),
            pl.BlockSpec((tm, D_MODEL), s_row, **once),
            pl.BlockSpec((1, D_MODEL), lambda i: (0, 0)),
            pl.BlockSpec((D_MODEL, N_SEG * seg), lambda i: (0, 0), **once),
        ],
        out_specs=[
            pl.BlockSpec((4, tm, seg), lambda i: (0, i, 0)),
            pl.BlockSpec((tm, seg), row),
            pl.BlockSpec((tm, seg), row),
            pl.BlockSpec((seg, tm), lambda i: (0, i)),
            pl.BlockSpec((tm, seg), s_row, **once),
            pl.BlockSpec((tm * N_HEADS, HEAD_W), p_row),
            pl.BlockSpec((tm * N_HEADS, HEAD_W), p_row),
            pl.BlockSpec((tm * N_HEADS, HEAD_W), s_row, **once),
            pl.BlockSpec((tm * N_HEADS, HEAD_W), s_row, **once),
        ],
        out_shape=[
            jax.ShapeDtypeStruct((4, m, seg), F32),
            jax.ShapeDtypeStruct((m, seg), BF16),
            jax.ShapeDtypeStruct((m, seg), BF16),
            jax.ShapeDtypeStruct((seg, m), BF16),
            jax.ShapeDtypeStruct((m_s, seg), BF16),
            jax.ShapeDtypeStruct((m_p * N_HEADS, HEAD_W), F32),
            jax.ShapeDtypeStruct((m_p * N_HEADS, HEAD_W), F32),
            jax.ShapeDtypeStruct((m_s * N_HEADS, HEAD_W), F32),
            jax.ShapeDtypeStruct((m_s * N_HEADS, HEAD_W), F32),
        ],
        compiler_params=pltpu.CompilerParams(
            dimension_semantics=("arbitrary",), vmem_limit_bytes=VMEM_LIMIT),
        name="inproj",
    )(xp, xs, gain, w_bf)


def _gla_kernel(hq_ref, hf_ref, hi_ref, hg_ref, lbnd_ref, gn_ref, s0_ref, o_ref, sout_ref, st_ref,
                q_ref, b_ref, *, n_chunks):
    r = pl.program_id(1)

    @pl.when(r == 0)
    def _():
        st_ref[...] = jnp.concatenate([s0_ref[0, h].T for h in range(N_HEADS)], axis=1)

    lbs = lbnd_ref[...]
    e = jnp.exp(lbs - jnp.max(lbs, axis=0, keepdims=True))
    lb = e[0:1, :] / jnp.sum(e, axis=0, keepdims=True)
    gn = jnp.concatenate([gn_ref[...]] * N_HEADS, axis=1)

    ri = lax.broadcasted_iota(jnp.int32, (CHUNK, CHUNK), 0)
    ci = lax.broadcasted_iota(jnp.int32, (CHUNK, CHUNK), 1)
    tril = (ri >= ci).astype(F32)
    pr = lax.broadcasted_iota(jnp.int32, (2 * HEAD_W, 2 * HEAD_W), 0)
    pc = lax.broadcasted_iota(jnp.int32, (2 * HEAD_W, 2 * HEAD_W), 1)
    pair_ones = ((pr >> 7) == (pc >> 7)).astype(BF16)
    gr = lax.broadcasted_iota(jnp.int32, (CHUNK, CHUNK * SUB), 0)
    gc = lax.broadcasted_iota(jnp.int32, (CHUNK, CHUNK * SUB), 1)
    seg_sum = (((gc >> 3) == gr) & ((gc & (SUB - 1)) <= (gr & (SUB - 1)))).astype(BF16)
    n_off = (N_SUB - 1) * SUB
    n_key = SUB * (N_SUB * (N_SUB - 1) // 2)
    orow = lax.broadcasted_iota(jnp.int32, (n_off, n_key), 0) >> 3
    ocol = lax.broadcasted_iota(jnp.int32, (n_off, n_key), 1)
    ocol_seg = sum((ocol >= 4 * i * (i - 1)).astype(jnp.int32) for i in range(2, N_SUB))
    off_mask = orow == ocol_seg
    nt = (((1,), (1,)), ((), ()))
    tn = (((0,), (0,)), ((), ()))

    n_all = n_key + CHUNK
    frow = lax.broadcasted_iota(jnp.int32, (CHUNK, n_all), 0)
    fcol = lax.broadcasted_iota(jnp.int32, (CHUNK, n_all), 1)
    fseg = sum((fcol >= 4 * i * (i - 1)).astype(jnp.int32) for i in range(2, N_SUB))
    fsame = fcol - n_key
    all_mask = (((fcol < n_key) & (fseg + 1 == (frow >> 3)))
                | ((fcol >= n_key) & ((fsame >> 3) == (frow >> 3)) & ((fsame & (SUB - 1)) <= (frow & (SUB - 1)))))

    def chunk(c, carry, factored):
        r0 = pl.multiple_of(c * CHUNK, CHUNK)
        hq = hq_ref[0, pl.ds(r0, CHUNK), :]
        hf = hf_ref[0, pl.ds(r0, CHUNK), :]
        v = hi_ref[0, pl.ds(r0, CHUNK), :]
        hg = hg_ref[0, pl.ds(r0, CHUNK), :]

        f = lb + (1.0 - lb) * _sigmoid(hf)
        g = jnp.log(f) * LOG2E
        kk = 1.0 - f
        q = hq * _sigmoid(hq)
        b = jnp.dot(tril, g, precision=lax.Precision.HIGHEST, preferred_element_type=F32)
        v_bf = v.astype(BF16)

        st = st_ref[...]
        st_bf = st.astype(BF16)
        b_last = b[CHUNK - 1:CHUNK, :]
        q_dec = q * jnp.exp2(b)
        q_in = q_dec.astype(BF16)
        k_dec = (kk * jnp.exp2(b_last - b)).astype(BF16)
        o_inter = jnp.concatenate(
            [lax.dot_general(q_in[:, _head(h)], st_bf[:, _head(h)], nt, preferred_element_type=F32)
             for h in range(N_HEADS)], axis=1)
        upd = jnp.concatenate(
            [lax.dot_general(v_bf[:, _head(h)], k_dec[:, _head(h)], tn, preferred_element_type=F32)
             for h in range(N_HEADS)], axis=1)
        st_ref[...] = jnp.exp2(b_last) * st + upd

        qt, kh, vh = [], [], []
        for i in range(1, N_SUB):
            lo = i * SUB
            b_start = b[lo - 1:lo, :]
            qt.append(q[lo:lo + SUB] * jnp.exp2(b[lo:lo + SUB] - b_start))
            kh.append(kk[:lo] * jnp.exp2(b_start - b[:lo]))
            vh.append(v[:lo])

        if factored:
            b0 = jnp.concatenate(
                [jnp.zeros((SUB, HG_WIDTH), F32)]
                + [jnp.broadcast_to(b[i * SUB - 1:i * SUB, :], (SUB, HG_WIDTH)) for i in range(1, N_SUB)], axis=0)
            kd = kk * jnp.exp2(b0 - b)
            q_all = jnp.concatenate([q_dec[:SUB]] + qt, axis=0).astype(BF16)
            k_all = jnp.concatenate(kh + [kd], axis=0).astype(BF16)
            v_all = jnp.concatenate(vh + [v], axis=0).astype(BF16)
            a_all = [lax.dot_general(q_all[:, _head(h)], k_all[:, _head(h)], nt, preferred_element_type=F32)
                     for h in range(N_HEADS)]
            o_intra = jnp.concatenate(
                [jnp.dot(jnp.where(all_mask, a_all[h], 0.0).astype(BF16), v_all[:, _head(h)],
                         preferred_element_type=F32) for h in range(N_HEADS)], axis=1)
        else:
            for h in range(N_HEADS):
                q_ref[h] = q[:, _head(h)]
                b_ref[h] = b[:, _head(h)]
            qt = jnp.concatenate(qt, axis=0).astype(BF16)
            kh = jnp.concatenate(kh, axis=0).astype(BF16)
            vh = jnp.concatenate(vh, axis=0).astype(BF16)
            a_off = [lax.dot_general(qt[:, _head(h)], kh[:, _head(h)], nt, preferred_element_type=F32)
                     for h in range(N_HEADS)]

            a_rep = []
            for j in range(N_HEADS // 2):
                p_pair = []
                for h in (2 * j, 2 * j + 1):
                    kk_h = kk[:, _head(h)]
                    b_h = b[:, _head(h)]
                    rows = []
                    for r in range(CHUNK):
                        lo = r - r % SUB
                        q_row = q_ref[h, pl.ds(r, SUB, stride=0), :]
                        b_row = b_ref[h, pl.ds(r, SUB, stride=0), :]
                        rows.append((q_row * kk_h[lo:lo + SUB])
                                    * jnp.exp2(jnp.minimum(b_row - b_h[lo:lo + SUB], 0.0)))
                    p_pair.append(jnp.concatenate(rows, axis=0).astype(BF16))
                a_rep.append(jnp.dot(jnp.concatenate(p_pair, axis=1), pair_ones,
                                     preferred_element_type=F32))

            o_off = jnp.concatenate(
                [jnp.dot(jnp.where(off_mask, a_off[h], 0.0).astype(BF16), vh[:, _head(h)],
                         preferred_element_type=F32) for h in range(N_HEADS)], axis=1)
            o_diag = []
            for j in range(N_HEADS // 2):
                pair = slice(2 * HEAD_W * j, 2 * HEAD_W * (j + 1))
                v_rep = jnp.broadcast_to(v[:, pair].reshape(N_SUB, 1, SUB, 2 * HEAD_W),
                                         (N_SUB, SUB, SUB, 2 * HEAD_W))
                w = (a_rep[j].reshape(N_SUB, SUB, SUB, 2 * HEAD_W) * v_rep).reshape(CHUNK * SUB, 2 * HEAD_W)
                o_diag.append(jnp.dot(seg_sum, w.astype(BF16), preferred_element_type=F32))
            o_intra = jnp.concatenate(o_diag, axis=1) + jnp.concatenate(
                [jnp.zeros((SUB, HG_WIDTH), F32), o_off], axis=0)
        o = o_inter + o_intra

        y = jnp.concatenate(
            [o[:, _head(h)] * lax.rsqrt(jnp.mean(o[:, _head(h)] * o[:, _head(h)], axis=-1, keepdims=True) + EPS)
             for h in range(N_HEADS)], axis=1)
        y = y * gn * (hg * _sigmoid(hg))
        o_ref[pl.ds(r0, CHUNK), :] = y.astype(BF16)
        return carry

    factor_ok = jnp.min(lb) > MIN_FACTORED_LB

    @pl.when(factor_ok)
    def _():
        lax.fori_loop(0, n_chunks, functools.partial(chunk, factored=True), 0, unroll=min(n_chunks, 4))

    @pl.when(jnp.logical_not(factor_ok))
    def _():
        lax.fori_loop(0, n_chunks, functools.partial(chunk, factored=False), 0)

    @pl.when(r == pl.num_programs(1) - 1)
    def _():
        st = st_ref[...]
        for h in range(N_HEADS):
            sout_ref[0, h] = st[:, _head(h)].T


def _gla(gates, lower_bounds, hg_norm, s0, row0, n_seq, seq_len, rows):
    m = n_seq * seq_len
    nr = seq_len // rows
    blk0 = row0 // rows
    seg = lambda s: pl.BlockSpec((1, rows, HG_WIDTH), lambda b, r: (s, blk0 + b * nr + r, 0))
    state = pl.BlockSpec((1, N_HEADS, HEAD_W, HEAD_W), lambda b, r: (b, 0, 0, 0))
    return pl.pallas_call(
        functools.partial(_gla_kernel, n_chunks=rows // CHUNK),
        grid=(n_seq, nr),
        in_specs=[
            seg(0), seg(1), seg(2), seg(3),
            pl.BlockSpec((lower_bounds.shape[0], HG_WIDTH), lambda b, r: (0, 0)),
            pl.BlockSpec((1, HEAD_W), lambda b, r: (0, 0)),
            state,
        ],
        out_specs=[pl.BlockSpec((rows, HG_WIDTH), lambda b, r: (b * nr + r, 0)), state],
        out_shape=[
            jax.ShapeDtypeStruct((m, HG_WIDTH), BF16),
            jax.ShapeDtypeStruct((n_seq, N_HEADS, HEAD_W, HEAD_W), F32),
        ],
        scratch_shapes=[pltpu.VMEM((HEAD_W, HG_WIDTH), F32),
                        pltpu.VMEM((N_HEADS, CHUNK, HEAD_W), F32),
                        pltpu.VMEM((N_HEADS, CHUNK, HEAD_W), F32)],
        compiler_params=pltpu.CompilerParams(
            dimension_semantics=("arbitrary", "arbitrary"), vmem_limit_bytes=VMEM_LIMIT),
        name="hgrn2",
    )(gates, gates, gates, gates, lower_bounds, hg_norm, s0)


def _stacked_query(q):
    lane = lax.broadcasted_iota(jnp.int32, q.shape, 1)
    qbig = jnp.concatenate([jnp.where(lane < ATT_DH, q, 0.0), jnp.where(lane >= ATT_DH, q, 0.0)], axis=0)
    return qbig.T.astype(BF16)


def _softmax_update(s, v_bf, m_ref, l_ref, acc_ref):
    m_prev = m_ref[...]
    m_new = jnp.maximum(m_prev, jnp.max(s, axis=0, keepdims=True))
    alpha = jnp.exp2(m_prev - m_new)
    p = jnp.exp2(s - m_new)
    l_ref[...] = alpha * l_ref[...] + jnp.sum(p, axis=0, keepdims=True)
    pv = lax.dot_general(v_bf, p.astype(BF16), (((0,), (0,)), ((), ())), preferred_element_type=F32)
    acc_ref[...] = alpha * acc_ref[...] + pv
    m_ref[...] = m_new


def _lambda(lq1_ref, lk1_ref, lq2_ref, lk2_ref):
    s1 = jnp.sum(lq1_ref[...] * lk1_ref[...], axis=-1, keepdims=True)
    s2 = jnp.sum(lq2_ref[...] * lk2_ref[...], axis=-1, keepdims=True)
    return jnp.exp(s1) - jnp.exp(s2) + LAM_INIT


def _attn_finish(acc, l, lam, sub, n):
    o_both = (acc * (1.0 / l)).T
    o = o_both[:n] - lam * o_both[n:]
    return _rmsnorm_rows(o, sub) * (1.0 - LAM_INIT)


def _attn_prompt_kernel(q_ref, k_ref, vt_ref, lq1_ref, lk1_ref, lq2_ref, lk2_ref, sub_ref, o_ref,
                        qq_ref, s_ref, smax_ref, mask_ref, m_ref, acc_ref, *, t, n_grp):
    qi = pl.program_id(1)
    grp = range(n_grp)
    for g in grp:
        qq_ref[g] = _stacked_query(q_ref[:, _head(g)].astype(F32))
    m_ref[...] = jnp.full(m_ref.shape, NEG_INF, F32)
    acc_ref[...] = jnp.zeros(acc_ref.shape, F32)
    ones_rows = jnp.ones((ONES_ROWS, t), BF16)

    def scores(g, kv):
        k0 = pl.multiple_of(kv * t, t)
        return jnp.dot(k_ref[pl.ds(k0, t), _head(g)], qq_ref[g], preferred_element_type=F32)

    def values_t(g, kv):
        k0 = pl.multiple_of(kv * t, t)
        return jnp.concatenate([vt_ref[_head(g), pl.ds(k0, t)], ones_rows], axis=0)

    def softmax(g, s, s_max):
        m_prev = m_ref[g]
        m_new = jnp.maximum(m_prev, s_max)
        m_ref[g] = m_new
        return jnp.exp2(s - m_new).astype(BF16), jnp.exp2(m_prev - m_new)

    def put_scores(g, buf, s):
        s_ref[g, buf] = s
        smax_ref[g, buf] = jnp.max(s, axis=0, keepdims=True)

    def stage(kv, cur, oth):
        k_next = pl.multiple_of((kv + 1) * t, t)
        vt = [values_t(g, kv) for g in grp]
        for c in range(0, 2 * t, STAGE_COLS):
            cols = slice(c, c + STAGE_COLS)
            for g in grp:
                s_next = jnp.dot(k_ref[pl.ds(k_next, t), _head(g)], qq_ref[g, :, cols],
                                 preferred_element_type=F32)
                s_ref[g, oth, :, cols] = s_next
                smax_ref[g, oth, :, cols] = jnp.max(s_next, axis=0, keepdims=True)
                m_prev = m_ref[g, :, cols]
                m_new = jnp.maximum(m_prev, smax_ref[g, cur, :, cols])
                m_ref[g, :, cols] = m_new
                p = jnp.exp2(s_ref[g, cur, :, cols] - m_new).astype(BF16)
                acc_ref[g, :, cols] = (jnp.exp2(m_prev - m_new) * acc_ref[g, :, cols]
                                       + jnp.dot(vt[g], p, preferred_element_type=F32))

    @pl.when((pl.program_id(0) == 0) & (qi == 0))
    def _():
        kpos = lax.broadcasted_iota(jnp.int32, mask_ref.shape, 0)
        col = lax.broadcasted_iota(jnp.int32, mask_ref.shape, 1)
        qpos = jnp.where(col >= t, col - t, col)
        mask_ref[...] = jnp.where((kpos >> 6) <= (qpos >> 6), 0.0, NEG_INF)

    def finish(cur):
        vt = [values_t(g, qi) for g in grp]
        acc = [[] for _ in grp]
        for c in range(0, 2 * t, FINISH_COLS):
            cols = slice(c, c + FINISH_COLS)
            for g in grp:
                s = s_ref[g, cur, :, cols] + mask_ref[:, cols]
                m_prev = m_ref[g, :, cols]
                m_new = jnp.maximum(m_prev, jnp.max(s, axis=0, keepdims=True))
                p = jnp.exp2(s - m_new).astype(BF16)
                acc[g].append(jnp.exp2(m_prev - m_new) * acc_ref[g, :, cols]
                              + jnp.dot(vt[g], p, preferred_element_type=F32))
        lam = _lambda(lq1_ref, lk1_ref, lq2_ref, lk2_ref)
        for g in grp:
            a = jnp.concatenate(acc[g], axis=1)
            o_ref[:, _head(g)] = _attn_finish(a[:HEAD_W], a[HEAD_W:HEAD_W + 1], lam, sub_ref[...],
                                              t).astype(BF16)

    s_first = [scores(g, 0) for g in grp]
    for g in grp:
        put_scores(g, 0, s_first[g])

    def pair(j, carry):
        stage(2 * j, 0, 1)
        stage(2 * j + 1, 1, 0)
        return carry

    lax.fori_loop(0, qi // 2, pair, 0)

    @pl.when(qi % 2 == 1)
    def _():
        stage(qi - 1, 0, 1)
        finish(1)

    @pl.when(qi % 2 == 0)
    def _():
        finish(0)


def _attn_prompt(q_bf, k_bf, vt_bf, lams, subln, n, t, n_grp):
    w = n_grp * HEAD_W
    small = lambda shape: pl.BlockSpec(shape, lambda h, i: (0, 0))
    return pl.pallas_call(
        functools.partial(_attn_prompt_kernel, t=t, n_grp=n_grp),
        grid=(N_HEADS // n_grp, n // t),
        in_specs=[
            pl.BlockSpec((t, w), lambda h, i: (i, h)),
            pl.BlockSpec((n, w), lambda h, i: (0, h)),
            pl.BlockSpec((w, n), lambda h, i: (h, 0)),
            small((1, ATT_DH)), small((1, ATT_DH)), small((1, ATT_DH)), small((1, ATT_DH)),
            small((1, HEAD_W)),
        ],
        out_specs=pl.BlockSpec((t, w), lambda h, i: (i, h)),
        out_shape=jax.ShapeDtypeStruct((n, ATT_WIDTH), BF16),
        scratch_shapes=[
            pltpu.VMEM((n_grp, HEAD_W, 2 * t), BF16),
            pltpu.VMEM((n_grp, 2, t, 2 * t), F32),
            pltpu.VMEM((n_grp, 2, 1, 2 * t), F32),
            pltpu.VMEM((t, 2 * t), F32),
            pltpu.VMEM((n_grp, 1, 2 * t), F32),
            pltpu.VMEM((n_grp, HEAD_W + ONES_ROWS, 2 * t), F32),
        ],
        compiler_params=pltpu.CompilerParams(
            dimension_semantics=("arbitrary", "arbitrary"), vmem_limit_bytes=VMEM_LIMIT),
        name="attn_prompt",
    )(q_bf, k_bf, vt_bf, *lams, subln)


def _attn_sample_kernel(q_ref, kc_ref, vc_ref, kn_ref, vn_ref, lq1_ref, lk1_ref, lq2_ref, lk2_ref,
                        sub_ref, o_ref, qq_ref, m_ref, l_ref, acc_ref, *, n_q, tk):
    t = pl.program_id(1)

    @pl.when(t == 0)
    def _():
        for h in range(N_HEADS):
            qq_ref[h] = _stacked_query(q_ref[:, _head(h)].astype(F32))
        m_ref[...] = jnp.full(m_ref.shape, NEG_INF, F32)
        l_ref[...] = jnp.zeros(l_ref.shape, F32)
        acc_ref[...] = jnp.zeros(acc_ref.shape, F32)

    def update(h, k_bf, v_bf):
        s = jnp.dot(k_bf, qq_ref[h], preferred_element_type=F32)
        _softmax_update(s, v_bf, m_ref.at[h], l_ref.at[h], acc_ref.at[h])

    for h in range(N_HEADS):
        rows = pl.ds(h, tk, stride=N_HEADS)
        update(h, kc_ref[0, rows, :].astype(BF16), vc_ref[0, rows, :].astype(BF16))

    @pl.when(t == pl.num_programs(1) - 1)
    def _():
        lam = _lambda(lq1_ref, lk1_ref, lq2_ref, lk2_ref)
        for h in range(N_HEADS):
            update(h, kn_ref[:, _head(h)], vn_ref[:, _head(h)])
            o_ref[:, _head(h)] = _attn_finish(acc_ref[h], l_ref[h], lam, sub_ref[...], n_q).astype(BF16)


def _attn_sample(q_bf, cache_k, cache_v, kn_bf, vn_bf, lams, subln, row0, n_q, tk):
    n_b = cache_k.shape[0]
    past = cache_k.shape[1] // N_HEADS
    assert past % CHUNK == 0 and n_q <= CHUNK and past % tk == 0 and row0 % n_q == 0
    blk0 = row0 // n_q
    small = lambda shape: pl.BlockSpec(shape, lambda b, t: (0, 0))
    rows = pl.BlockSpec((n_q, ATT_WIDTH), lambda b, t: (b, 0))
    rows_all = pl.BlockSpec((n_q, ATT_WIDTH), lambda b, t: (blk0 + b, 0))
    cache = pl.BlockSpec((1, tk * N_HEADS, HEAD_W), lambda b, t: (b, t, 0))
    return pl.pallas_call(
        functools.partial(_attn_sample_kernel, n_q=n_q, tk=tk),
        grid=(n_b, past // tk),
        in_specs=[
            rows_all, cache, cache, rows_all, rows,
            small((1, ATT_DH)), small((1, ATT_DH)), small((1, ATT_DH)), small((1, ATT_DH)),
            small((1, HEAD_W)),
        ],
        out_specs=rows,
        out_shape=jax.ShapeDtypeStruct((n_b * n_q, ATT_WIDTH), BF16),
        scratch_shapes=[
            pltpu.VMEM((N_HEADS, HEAD_W, 2 * n_q), BF16),
            pltpu.VMEM((N_HEADS, 1, 2 * n_q), F32),
            pltpu.VMEM((N_HEADS, 1, 2 * n_q), F32),
            pltpu.VMEM((N_HEADS, HEAD_W, 2 * n_q), F32),
        ],
        compiler_params=pltpu.CompilerParams(
            dimension_semantics=("arbitrary", "arbitrary"), vmem_limit_bytes=VMEM_LIMIT),
        name="attn_sample",
    )(q_bf, cache_k, cache_v, kn_bf, vn_bf, *lams, subln)


def _outproj_kernel(x_ref, a_ref, b_ref, w_ref, o_ref, wb_ref):
    @pl.when(pl.program_id(0) == 0)
    def _():
        wb_ref[...] = w_ref[...].astype(BF16)

    o_ref[...] = (x_ref[...]
                  + jnp.dot(a_ref[...], wb_ref[:HG_WIDTH, :], preferred_element_type=F32)
                  + jnp.dot(b_ref[...], wb_ref[HG_WIDTH:, :], preferred_element_type=F32))


def _outproj(x, mix_hg, mix_at, w, tm):
    m = x.shape[0]
    row = lambda i: (i, 0)
    return pl.pallas_call(
        _outproj_kernel,
        grid=(m // tm,),
        in_specs=[
            pl.BlockSpec((tm, D_MODEL), row),
            pl.BlockSpec((tm, HG_WIDTH), row),
            pl.BlockSpec((tm, ATT_WIDTH), row),
            pl.BlockSpec((HG_WIDTH + ATT_WIDTH, D_MODEL), lambda i: (0, 0), pipeline_mode=pl.Buffered(1)),
        ],
        out_specs=pl.BlockSpec((tm, D_MODEL), row),
        out_shape=jax.ShapeDtypeStruct((m, D_MODEL), F32),
        scratch_shapes=[pltpu.VMEM((HG_WIDTH + ATT_WIDTH, D_MODEL), BF16)],
        compiler_params=pltpu.CompilerParams(
            dimension_semantics=("arbitrary",), vmem_limit_bytes=VMEM_LIMIT),
        name="outproj",
    )(x, mix_hg, mix_at, w)


def _mlp_kernel(x_ref, gain_ref, wu_ref, wd_ref, gfin_ref, o_ref, h_ref):
    j = pl.program_id(1)

    @pl.when(j == 0)
    def _():
        x = x_ref[...]
        h_ref[...] = _rmsnorm_rows(x, gain_ref[...]).astype(BF16)
        o_ref[...] = x

    u = jnp.dot(h_ref[...], wu_ref[...].astype(BF16), preferred_element_type=F32)
    u = jnp.square(jnp.maximum(u, 0.0)).astype(BF16)
    o_ref[...] += jnp.dot(u, wd_ref[...].astype(BF16), preferred_element_type=F32)

    @pl.when(j == pl.num_programs(1) - 1)
    def _():
        o_ref[...] = _rmsnorm_rows(o_ref[...], gfin_ref[...])


def _mlp(x, gain, wu, wd, gfin, tm, tf):
    m = x.shape[0]
    return pl.pallas_call(
        _mlp_kernel,
        grid=(m // tm, D_FF // tf),
        in_specs=[
            pl.BlockSpec((tm, D_MODEL), lambda i, j: (i, 0)),
            pl.BlockSpec((1, D_MODEL), lambda i, j: (0, 0)),
            pl.BlockSpec((D_MODEL, tf), lambda i, j: (0, j)),
            pl.BlockSpec((tf, D_MODEL), lambda i, j: (j, 0)),
            pl.BlockSpec((1, D_MODEL), lambda i, j: (0, 0)),
        ],
        out_specs=pl.BlockSpec((tm, D_MODEL), lambda i, j: (i, 0)),
        out_shape=jax.ShapeDtypeStruct((m, D_MODEL), F32),
        scratch_shapes=[pltpu.VMEM((tm, D_MODEL), BF16)],
        compiler_params=pltpu.CompilerParams(
            dimension_semantics=("arbitrary", "arbitrary"), vmem_limit_bytes=VMEM_LIMIT),
        name="mlp",
    )(x, gain, wu, wd, gfin)


def kernel(x_prompt, x_sample, cache_k, cache_v, state_hgrn, norm_attn, w_in, lower_bounds, hg_norm,
           lambda_q1, lambda_k1, lambda_q2, lambda_k2, subln, w_out, norm_mlp, w_up, w_down, norm_final):
    depth = w_in.shape[0]
    assert depth == 1
    n_pb, n_p, _ = x_prompt.shape
    n_sb, n_s, _ = x_sample.shape
    assert n_pb == 1
    past = cache_k.shape[2]

    w_in_bf = w_in[0].astype(BF16)
    lams = (lambda_q1, lambda_k1, lambda_q2, lambda_k2)
    gfin = norm_final.reshape(1, D_MODEL)

    xp = x_prompt.reshape(n_p, D_MODEL)
    xs = x_sample.reshape(n_sb * n_s, D_MODEL)

    gates, q_bf, k_bf, vt_bf, vs_bf, kf_p, vf_p, kf_s, vf_s = _inproj(xp, xs, norm_attn, w_in_bf, tm=256)

    s0_p = jnp.zeros((1, N_HEADS, HEAD_W, HEAD_W), F32)
    ohg_p, st_p = _gla(gates, lower_bounds, hg_norm, s0_p, row0=0, n_seq=1, seq_len=n_p, rows=512)
    ohg_s, st_s = _gla(gates, lower_bounds, hg_norm, state_hgrn[0], row0=n_p, n_seq=n_sb, seq_len=n_s, rows=n_s)

    oat_p = _attn_prompt(q_bf, k_bf, vt_bf, lams, subln, n=n_p, t=512, n_grp=2)
    ck = cache_k[0].reshape(n_sb, past * N_HEADS, HEAD_W)
    cv = cache_v[0].reshape(n_sb, past * N_HEADS, HEAD_W)
    oat_s = _attn_sample(q_bf, ck, cv, k_bf, vs_bf, lams, subln, row0=n_p, n_q=n_s, tk=1024)

    x1_p = _outproj(xp, ohg_p, oat_p, w_out[0], tm=512)
    x1_s = _outproj(xs, ohg_s, oat_s, w_out[0], tm=512)

    y_p = _mlp(x1_p, norm_mlp, w_up[0], w_down[0], gfin, tm=1024, tf=512)
    y_s = _mlp(x1_s, norm_mlp, w_up[0], w_down[0], gfin, tm=512, tf=512)

    return (
        y_p.reshape(n_pb, n_p, D_MODEL),
        y_s.reshape(n_sb, n_s, D_MODEL),
        kf_p.reshape(1, n_pb, n_p, N_HEADS, HEAD_W),
        vf_p.reshape(1, n_pb, n_p, N_HEADS, HEAD_W),
        st_p.reshape(1, n_pb, N_HEADS, HEAD_W, HEAD_W),
        kf_s.reshape(1, n_sb, n_s, N_HEADS, HEAD_W),
        vf_s.reshape(1, n_sb, n_s, N_HEADS, HEAD_W),
        st_s.reshape(1, n_sb, N_HEADS, HEAD_W, HEAD_W),
    )
```

```python
import functools
import math

import jax
import jax.numpy as jnp
from jax import lax
from jax.experimental import pallas as pl
from jax.experimental.pallas import tpu as pltpu

F32 = jnp.float32
BF16 = jnp.bfloat16

D_MODEL = 2048
HG_WIDTH = 1024
ATT_WIDTH = 1024
N_HEADS = 8
HEAD_W = 128
ATT_DH = 64
CHUNK = 64
SUB = 8
N_SUB = CHUNK // SUB
MIN_FACTORED_LB = 2.0 ** (-100.0 / SUB)
D_FF = 4 * D_MODEL
N_SEG = 7
EPS = 1e-6
NEG_INF = -1e30
LOG2E = 1.4426950408889634
Q_SCALE = ATT_DH ** -0.5 * LOG2E
LAM_INIT = 0.8 - 0.6 * math.exp(-0.3 * 0)
ONES_ROWS = 16
FINISH_COLS = 256
STAGE_COLS = 256

VMEM_LIMIT = 56 * 1024 * 1024


def _rmsnorm_rows(x, gain):
    return x * lax.rsqrt(jnp.mean(x * x, axis=-1, keepdims=True) + EPS) * gain


def _sigmoid(x):
    return 1.0 / (1.0 + jnp.exp(-x))


def _head(h):
    return slice(h * HEAD_W, (h + 1) * HEAD_W)


def _inproj_kernel(xp_ref, xs_ref, gain_ref, w_ref, gates_ref, q_ref, kb_ref, vt_ref, vbs_ref,
                   kfp_ref, vfp_ref, kfs_ref, vfs_ref, *, n_p):
    def project(x_ref, kf_ref, vf_ref, vb_ref):
        tm = x_ref.shape[0]
        h = _rmsnorm_rows(x_ref[...], gain_ref[...]).astype(BF16)

        def segment(s):
            return jnp.dot(h, w_ref[:, s * HG_WIDTH:(s + 1) * HG_WIDTH], preferred_element_type=F32)

        k = segment(5)
        v = segment(6)
        kb_ref[...] = k.astype(BF16)
        vt_ref[...] = v.T.astype(BF16)
        if vb_ref is not None:
            vb_ref[...] = v.astype(BF16)
        for hd in range(N_HEADS):
            kf_ref[pl.ds(hd, tm, stride=N_HEADS), :] = k[:, _head(hd)]
            vf_ref[pl.ds(hd, tm, stride=N_HEADS), :] = v[:, _head(hd)]
        q_ref[...] = (segment(4) * Q_SCALE).astype(BF16)
        for s in range(4):
            gates_ref[s] = segment(s)

    is_prompt = pl.program_id(0) < n_p

    @pl.when(is_prompt)
    def _():
        project(xp_ref, kfp_ref, vfp_ref, None)

    @pl.when(jnp.logical_not(is_prompt))
    def _():
        project(xs_ref, kfs_ref, vfs_ref, vbs_ref)


def _inproj(xp, xs, gain, w_bf, tm):
    m_p, m_s = xp.shape[0], xs.shape[0]
    n_p, n_s = m_p // tm, m_s // tm
    m = m_p + m_s
    seg = HG_WIDTH
    row = lambda i: (i, 0)
    p_row = lambda i: (jnp.minimum(i, n_p - 1), 0)
    s_row = lambda i: (jnp.maximum(i - n_p, 0), 0)
    once = dict(pipeline_mode=pl.Buffered(1))
    return pl.pallas_call(
        functools.partial(_inproj_kernel, n_p=n_p),
        grid=(n_p + n_s,),
        in_specs=[
            pl.BlockSpec((tm, D_MODEL), p_row),
            pl.BlockSpec((tm, D_MODEL), s_row, **once),
            pl.BlockSpec((1, D_MODEL), lambda i: (0, 0)),
            pl.BlockSpec((D_MODEL, N_SEG * seg), lambda i: (0, 0), **once),
        ],
        out_specs=[
            pl.BlockSpec((4, tm, seg), lambda i: (0, i, 0)),
            pl.BlockSpec((tm, seg), row),
            pl.BlockSpec((tm, seg), row),
            pl.BlockSpec((seg, tm), lambda i: (0, i)),
            pl.BlockSpec((tm, seg), s_row, **once),
            pl.BlockSpec((tm * N_HEADS, HEAD_W), p_row),
            pl.BlockSpec((tm * N_HEADS, HEAD_W), p_row),
            pl.BlockSpec((tm * N_HEADS, HEAD_W), s_row, **once),
            pl.BlockSpec((tm * N_HEADS, HEAD_W), s_row, **once),
        ],
        out_shape=[
            jax.ShapeDtypeStruct((4, m, seg), F32),
            jax.ShapeDtypeStruct((m, seg), BF16),
            jax.ShapeDtypeStruct((m, seg), BF16),
            jax.ShapeDtypeStruct((seg, m), BF16),
            jax.ShapeDtypeStruct((m_s, seg), BF16),
            jax.ShapeDtypeStruct((m_p * N_HEADS, HEAD_W), F32),
            jax.ShapeDtypeStruct((m_p * N_HEADS, HEAD_W), F32),
            jax.ShapeDtypeStruct((m_s * N_HEADS, HEAD_W), F32),
            jax.ShapeDtypeStruct((m_s * N_HEADS, HEAD_W), F32),
        ],
        compiler_params=pltpu.CompilerParams(
            dimension_semantics=("arbitrary",), vmem_limit_bytes=VMEM_LIMIT),
        name="inproj",
    )(xp, xs, gain, w_bf)


def _gla_kernel(hq_ref, hf_ref, hi_ref, hg_ref, lbnd_ref, gn_ref, s0_ref, o_ref, sout_ref, st_ref,
                q_ref, b_ref, *, n_chunks):
    r = pl.program_id(1)

    @pl.when(r == 0)
    def _():
        st_ref[...] = jnp.concatenate([s0_ref[0, h].T for h in range(N_HEADS)], axis=1)

    lbs = lbnd_ref[...]
    e = jnp.exp(lbs - jnp.max(lbs, axis=0, keepdims=True))
    lb = e[0:1, :] / jnp.sum(e, axis=0, keepdims=True)
    gn = jnp.concatenate([gn_ref[...]] * N_HEADS, axis=1)

    ri = lax.broadcasted_iota(jnp.int32, (CHUNK, CHUNK), 0)
    ci = lax.broadcasted_iota(jnp.int32, (CHUNK, CHUNK), 1)
    tril = (ri >= ci).astype(F32)
    pr = lax.broadcasted_iota(jnp.int32, (2 * HEAD_W, 2 * HEAD_W), 0)
    pc = lax.broadcasted_iota(jnp.int32, (2 * HEAD_W, 2 * HEAD_W), 1)
    pair_ones = ((pr >> 7) == (pc >> 7)).astype(BF16)
    gr = lax.broadcasted_iota(jnp.int32, (CHUNK, CHUNK * SUB), 0)
    gc = lax.broadcasted_iota(jnp.int32, (CHUNK, CHUNK * SUB), 1)
    seg_sum = (((gc >> 3) == gr) & ((gc & (SUB - 1)) <= (gr & (SUB - 1)))).astype(BF16)
    n_off = (N_SUB - 1) * SUB
    n_key = SUB * (N_SUB * (N_SUB - 1) // 2)
    orow = lax.broadcasted_iota(jnp.int32, (n_off, n_key), 0) >> 3
    ocol = lax.broadcasted_iota(jnp.int32, (n_off, n_key), 1)
    ocol_seg = sum((ocol >= 4 * i * (i - 1)).astype(jnp.int32) for i in range(2, N_SUB))
    off_mask = orow == ocol_seg
    nt = (((1,), (1,)), ((), ()))
    tn = (((0,), (0,)), ((), ()))

    n_all = n_key + CHUNK
    frow = lax.broadcasted_iota(jnp.int32, (CHUNK, n_all), 0)
    fcol = lax.broadcasted_iota(jnp.int32, (CHUNK, n_all), 1)
    fseg = sum((fcol >= 4 * i * (i - 1)).astype(jnp.int32) for i in range(2, N_SUB))
    fsame = fcol - n_key
    all_mask = (((fcol < n_key) & (fseg + 1 == (frow >> 3)))
                | ((fcol >= n_key) & ((fsame >> 3) == (frow >> 3)) & ((fsame & (SUB - 1)) <= (frow & (SUB - 1)))))

    def chunk(c, carry, factored):
        r0 = pl.multiple_of(c * CHUNK, CHUNK)
        hq = hq_ref[0, pl.ds(r0, CHUNK), :]
        hf = hf_ref[0, pl.ds(r0, CHUNK), :]
        v = hi_ref[0, pl.ds(r0, CHUNK), :]
        hg = hg_ref[0, pl.ds(r0, CHUNK), :]

        f = lb + (1.0 - lb) * _sigmoid(hf)
        g = jnp.log(f) * LOG2E
        kk = 1.0 - f
        q = hq * _sigmoid(hq)
        b = jnp.dot(tril, g, precision=lax.Precision.HIGHEST, preferred_element_type=F32)
        v_bf = v.astype(BF16)

        st = st_ref[...]
        st_bf = st.astype(BF16)
        b_last = b[CHUNK - 1:CHUNK, :]
        q_dec = q * jnp.exp2(b)
        q_in = q_dec.astype(BF16)
        k_dec = (kk * jnp.exp2(b_last - b)).astype(BF16)
        o_inter = jnp.concatenate(
            [lax.dot_general(q_in[:, _head(h)], st_bf[:, _head(h)], nt, preferred_element_type=F32)
             for h in range(N_HEADS)], axis=1)
        upd = jnp.concatenate(
            [lax.dot_general(v_bf[:, _head(h)], k_dec[:, _head(h)], tn, preferred_element_type=F32)
             for h in range(N_HEADS)], axis=1)
        st_ref[...] = jnp.exp2(b_last) * st + upd

        qt, kh, vh = [], [], []
        for i in range(1, N_SUB):
            lo = i * SUB
            b_start = b[lo - 1:lo, :]
            qt.append(q[lo:lo + SUB] * jnp.exp2(b[lo:lo + SUB] - b_start))
            kh.append(kk[:lo] * jnp.exp2(b_start - b[:lo]))
            vh.append(v[:lo])

        if factored:
            b0 = jnp.concatenate(
                [jnp.zeros((SUB, HG_WIDTH), F32)]
                + [jnp.broadcast_to(b[i * SUB - 1:i * SUB, :], (SUB, HG_WIDTH)) for i in range(1, N_SUB)], axis=0)
            kd = kk * jnp.exp2(b0 - b)
            q_all = jnp.concatenate([q_dec[:SUB]] + qt, axis=0).astype(BF16)
            k_all = jnp.concatenate(kh + [kd], axis=0).astype(BF16)
            v_all = jnp.concatenate(vh + [v], axis=0).astype(BF16)
            a_all = [lax.dot_general(q_all[:, _head(h)], k_all[:, _head(h)], nt, preferred_element_type=F32)
                     for h in range(N_HEADS)]
            o_intra = jnp.concatenate(
                [jnp.dot(jnp.where(all_mask, a_all[h], 0.0).astype(BF16), v_all[:, _head(h)],
                         preferred_element_type=F32) for h in range(N_HEADS)], axis=1)
        else:
            for h in range(N_HEADS):
                q_ref[h] = q[:, _head(h)]
                b_ref[h] = b[:, _head(h)]
            qt = jnp.concatenate(qt, axis=0).astype(BF16)
            kh = jnp.concatenate(kh, axis=0).astype(BF16)
            vh = jnp.concatenate(vh, axis=0).astype(BF16)
            a_off = [lax.dot_general(qt[:, _head(h)], kh[:, _head(h)], nt, preferred_element_type=F32)
                     for h in range(N_HEADS)]

            a_rep = []
            for j in range(N_HEADS // 2):
                p_pair = []
                for h in (2 * j, 2 * j + 1):
                    kk_h = kk[:, _head(h)]
                    b_h = b[:, _head(h)]
                    rows = []
                    for r in range(CHUNK):
                        lo = r - r % SUB
                        q_row = q_ref[h, pl.ds(r, SUB, stride=0), :]
                        b_row = b_ref[h, pl.ds(r, SUB, stride=0), :]
                        rows.append((q_row * kk_h[lo:lo + SUB])
                                    * jnp.exp2(jnp.minimum(b_row - b_h[lo:lo + SUB], 0.0)))
                    p_pair.append(jnp.concatenate(rows, axis=0).astype(BF16))
                a_rep.append(jnp.dot(jnp.concatenate(p_pair, axis=1), pair_ones,
                                     preferred_element_type=F32))

            o_off = jnp.concatenate(
                [jnp.dot(jnp.where(off_mask, a_off[h], 0.0).astype(BF16), vh[:, _head(h)],
                         preferred_element_type=F32) for h in range(N_HEADS)], axis=1)
            o_diag = []
            for j in range(N_HEADS // 2):
                pair = slice(2 * HEAD_W * j, 2 * HEAD_W * (j + 1))
                v_rep = jnp.broadcast_to(v[:, pair].reshape(N_SUB, 1, SUB, 2 * HEAD_W),
                                         (N_SUB, SUB, SUB, 2 * HEAD_W))
                w = (a_rep[j].reshape(N_SUB, SUB, SUB, 2 * HEAD_W) * v_rep).reshape(CHUNK * SUB, 2 * HEAD_W)
                o_diag.append(jnp.dot(seg_sum, w.astype(BF16), preferred_element_type=F32))
            o_intra = jnp.concatenate(o_diag, axis=1) + jnp.concatenate(
                [jnp.zeros((SUB, HG_WIDTH), F32), o_off], axis=0)
        o = o_inter + o_intra

        y = jnp.concatenate(
            [o[:, _head(h)] * lax.rsqrt(jnp.mean(o[:, _head(h)] * o[:, _head(h)], axis=-1, keepdims=True) + EPS)
             for h in range(N_HEADS)], axis=1)
        y = y * gn * (hg * _sigmoid(hg))
        o_ref[pl.ds(r0, CHUNK), :] = y.astype(BF16)
        return carry

    factor_ok = jnp.min(lb) > MIN_FACTORED_LB

    @pl.when(factor_ok)
    def _():
        lax.fori_loop(0, n_chunks, functools.partial(chunk, factored=True), 0, unroll=min(n_chunks, 4))

    @pl.when(jnp.logical_not(factor_ok))
    def _():
        lax.fori_loop(0, n_chunks, functools.partial(chunk, factored=False), 0)

    @pl.when(r == pl.num_programs(1) - 1)
    def _():
        st = st_ref[...]
        for h in range(N_HEADS):
            sout_ref[0, h] = st[:, _head(h)].T


def _gla(gates, lower_bounds, hg_norm, s0, row0, n_seq, seq_len, rows):
    m = n_seq * seq_len
    nr = seq_len // rows
    blk0 = row0 // rows
    seg = lambda s: pl.BlockSpec((1, rows, HG_WIDTH), lambda b, r: (s, blk0 + b * nr + r, 0))
    state = pl.BlockSpec((1, N_HEADS, HEAD_W, HEAD_W), lambda b, r: (b, 0, 0, 0))
    return pl.pallas_call(
        functools.partial(_gla_kernel, n_chunks=rows // CHUNK),
        grid=(n_seq, nr),
        in_specs=[
            seg(0), seg(1), seg(2), seg(3),
            pl.BlockSpec((lower_bounds.shape[0], HG_WIDTH), lambda b, r: (0, 0)),
            pl.BlockSpec((1, HEAD_W), lambda b, r: (0, 0)),
            state,
        ],
        out_specs=[pl.BlockSpec((rows, HG_WIDTH), lambda b, r: (b * nr + r, 0)), state],
        out_shape=[
            jax.ShapeDtypeStruct((m, HG_WIDTH), BF16),
            jax.ShapeDtypeStruct((n_seq, N_HEADS, HEAD_W, HEAD_W), F32),
        ],
        scratch_shapes=[pltpu.VMEM((HEAD_W, HG_WIDTH), F32),
                        pltpu.VMEM((N_HEADS, CHUNK, HEAD_W), F32),
                        pltpu.VMEM((N_HEADS, CHUNK, HEAD_W), F32)],
        compiler_params=pltpu.CompilerParams(
            dimension_semantics=("arbitrary", "arbitrary"), vmem_limit_bytes=VMEM_LIMIT),
        name="hgrn2",
    )(gates, gates, gates, gates, lower_bounds, hg_norm, s0)


def _stacked_query(q):
    lane = lax.broadcasted_iota(jnp.int32, q.shape, 1)
    qbig = jnp.concatenate([jnp.where(lane < ATT_DH, q, 0.0), jnp.where(lane >= ATT_DH, q, 0.0)], axis=0)
    return qbig.T.astype(BF16)


def _softmax_update(s, v_bf, m_ref, l_ref, acc_ref):
    m_prev = m_ref[...]
    m_new = jnp.maximum(m_prev, jnp.max(s, axis=0, keepdims=True))
    alpha = jnp.exp2(m_prev - m_new)
    p = jnp.exp2(s - m_new)
    l_ref[...] = alpha * l_ref[...] + jnp.sum(p, axis=0, keepdims=True)
    pv = lax.dot_general(v_bf, p.astype(BF16), (((0,), (0,)), ((), ())), preferred_element_type=F32)
    acc_ref[...] = alpha * acc_ref[...] + pv
    m_ref[...] = m_new


def _lambda(lq1_ref, lk1_ref, lq2_ref, lk2_ref):
    s1 = jnp.sum(lq1_ref[...] * lk1_ref[...], axis=-1, keepdims=True)
    s2 = jnp.sum(lq2_ref[...] * lk2_ref[...], axis=-1, keepdims=True)
    return jnp.exp(s1) - jnp.exp(s2) + LAM_INIT


def _attn_finish(acc, l, lam, sub, n):
    o_both = (acc * (1.0 / l)).T
    o = o_both[:n] - lam * o_both[n:]
    return _rmsnorm_rows(o, sub) * (1.0 - LAM_INIT)


def _attn_prompt_kernel(q_ref, qn_ref, k_ref, vt_ref, lq1_ref, lk1_ref, lq2_ref, lk2_ref, sub_ref, o_ref,
                        qq_ref, s_ref, smax_ref, mask_ref, m_ref, acc_ref, *, t, n_grp):
    qi = pl.program_id(1)
    grp = range(n_grp)
    slot = qi % 2
    m_ref[...] = jnp.full(m_ref.shape, NEG_INF, F32)
    acc_ref[...] = jnp.zeros(acc_ref.shape, F32)
    ones_rows = jnp.ones((ONES_ROWS, t), BF16)

    def values_t(g, kv):
        k0 = pl.multiple_of(kv * t, t)
        return jnp.concatenate([vt_ref[_head(g), pl.ds(k0, t)], ones_rows], axis=0)

    def first_scores(g, qslot, cols):
        s = jnp.dot(k_ref[0:t, _head(g)], qq_ref[qslot, g, :, cols], preferred_element_type=F32)
        s_ref[g, 0, :, cols] = s
        smax_ref[g, 0, :, cols] = jnp.max(s, axis=0, keepdims=True)

    def stage(kv, cur, oth):
        k_next = pl.multiple_of((kv + 1) * t, t)
        vt = [values_t(g, kv) for g in grp]
        for c in range(0, 2 * t, STAGE_COLS):
            cols = slice(c, c + STAGE_COLS)
            for g in grp:
                s_next = jnp.dot(k_ref[pl.ds(k_next, t), _head(g)], qq_ref[slot, g, :, cols],
                                 preferred_element_type=F32)
                s_ref[g, oth, :, cols] = s_next
                smax_ref[g, oth, :, cols] = jnp.max(s_next, axis=0, keepdims=True)
                m_prev = m_ref[g, :, cols]
                m_new = jnp.maximum(m_prev, smax_ref[g, cur, :, cols])
                m_ref[g, :, cols] = m_new
                p = jnp.exp2(s_ref[g, cur, :, cols] - m_new).astype(BF16)
                acc_ref[g, :, cols] = (jnp.exp2(m_prev - m_new) * acc_ref[g, :, cols]
                                       + jnp.dot(vt[g], p, preferred_element_type=F32))

    @pl.when((pl.program_id(0) == 0) & (qi == 0))
    def _():
        kpos = lax.broadcasted_iota(jnp.int32, mask_ref.shape, 0)
        col = lax.broadcasted_iota(jnp.int32, mask_ref.shape, 1)
        qpos = jnp.where(col >= t, col - t, col)
        mask_ref[...] = jnp.where((kpos >> 6) <= (qpos >> 6), 0.0, NEG_INF)

    def finish(cur):
        for g in grp:
            qq_ref[1 - slot, g] = _stacked_query(qn_ref[:, _head(g)].astype(F32))
        vt = [values_t(g, qi) for g in grp]
        acc = [[] for _ in grp]
        for c in range(0, 2 * t, FINISH_COLS):
            cols = slice(c, c + FINISH_COLS)
            for g in grp:
                s = s_ref[g, cur, :, cols] + mask_ref[:, cols]
                m_prev = m_ref[g, :, cols]
                m_new = jnp.maximum(m_prev, jnp.max(s, axis=0, keepdims=True))
                p = jnp.exp2(s - m_new).astype(BF16)
                acc[g].append(jnp.exp2(m_prev - m_new) * acc_ref[g, :, cols]
                              + jnp.dot(vt[g], p, preferred_element_type=F32))
                first_scores(g, 1 - slot, cols)
        lam = _lambda(lq1_ref, lk1_ref, lq2_ref, lk2_ref)
        for g in grp:
            a = jnp.concatenate(acc[g], axis=1)
            o_ref[:, _head(g)] = _attn_finish(a[:HEAD_W], a[HEAD_W:HEAD_W + 1], lam, sub_ref[...],
                                              t).astype(BF16)

    @pl.when(qi == 0)
    def _():
        for g in grp:
            qq_ref[0, g] = _stacked_query(q_ref[:, _head(g)].astype(F32))
        for c in range(0, 2 * t, FINISH_COLS):
            for g in grp:
                first_scores(g, 0, slice(c, c + FINISH_COLS))

    def pair(j, carry):
        stage(2 * j, 0, 1)
        stage(2 * j + 1, 1, 0)
        return carry

    lax.fori_loop(0, qi // 2, pair, 0)

    @pl.when(qi % 2 == 1)
    def _():
        stage(qi - 1, 0, 1)
        finish(1)

    @pl.when(qi % 2 == 0)
    def _():
        finish(0)


def _attn_prompt(q_bf, k_bf, vt_bf, lams, subln, n, t, n_grp):
    w = n_grp * HEAD_W
    small = lambda shape: pl.BlockSpec(shape, lambda h, i: (0, 0))
    return pl.pallas_call(
        functools.partial(_attn_prompt_kernel, t=t, n_grp=n_grp),
        grid=(N_HEADS // n_grp, n // t),
        in_specs=[
            pl.BlockSpec((t, w), lambda h, i: (i, h)),
            pl.BlockSpec((t, w), lambda h, i: (jnp.minimum(i + 1, n // t - 1), h)),
            pl.BlockSpec((n, w), lambda h, i: (0, h)),
            pl.BlockSpec((w, n), lambda h, i: (h, 0)),
            small((1, ATT_DH)), small((1, ATT_DH)), small((1, ATT_DH)), small((1, ATT_DH)),
            small((1, HEAD_W)),
        ],
        out_specs=pl.BlockSpec((t, w), lambda h, i: (i, h)),
        out_shape=jax.ShapeDtypeStruct((n, ATT_WIDTH), BF16),
        scratch_shapes=[
            pltpu.VMEM((2, n_grp, HEAD_W, 2 * t), BF16),
            pltpu.VMEM((n_grp, 2, t, 2 * t), F32),
            pltpu.VMEM((n_grp, 2, 1, 2 * t), F32),
            pltpu.VMEM((t, 2 * t), F32),
            pltpu.VMEM((n_grp, 1, 2 * t), F32),
            pltpu.VMEM((n_grp, HEAD_W + ONES_ROWS, 2 * t), F32),
        ],
        compiler_params=pltpu.CompilerParams(
            dimension_semantics=("arbitrary", "arbitrary"), vmem_limit_bytes=VMEM_LIMIT),
        name="attn_prompt",
    )(q_bf, q_bf, k_bf, vt_bf, *lams, subln)


def _attn_sample_kernel(q_ref, kc_ref, vc_ref, kn_ref, vn_ref, lq1_ref, lk1_ref, lq2_ref, lk2_ref,
                        sub_ref, o_ref, qq_ref, m_ref, l_ref, acc_ref, *, n_q, tk):
    t = pl.program_id(1)

    @pl.when(t == 0)
    def _():
        for h in range(N_HEADS):
            qq_ref[h] = _stacked_query(q_ref[:, _head(h)].astype(F32))
        m_ref[...] = jnp.full(m_ref.shape, NEG_INF, F32)
        l_ref[...] = jnp.zeros(l_ref.shape, F32)
        acc_ref[...] = jnp.zeros(acc_ref.shape, F32)

    def update(h, k_bf, v_bf):
        s = jnp.dot(k_bf, qq_ref[h], preferred_element_type=F32)
        _softmax_update(s, v_bf, m_ref.at[h], l_ref.at[h], acc_ref.at[h])

    for h in range(N_HEADS):
        rows = pl.ds(h, tk, stride=N_HEADS)
        update(h, kc_ref[0, rows, :].astype(BF16), vc_ref[0, rows, :].astype(BF16))

    @pl.when(t == pl.num_programs(1) - 1)
    def _():
        lam = _lambda(lq1_ref, lk1_ref, lq2_ref, lk2_ref)
        for h in range(N_HEADS):
            update(h, kn_ref[:, _head(h)], vn_ref[:, _head(h)])
            o_ref[:, _head(h)] = _attn_finish(acc_ref[h], l_ref[h], lam, sub_ref[...], n_q).astype(BF16)


def _attn_sample(q_bf, cache_k, cache_v, kn_bf, vn_bf, lams, subln, row0, n_q, tk):
    n_b = cache_k.shape[0]
    past = cache_k.shape[1] // N_HEADS
    assert past % CHUNK == 0 and n_q <= CHUNK and past % tk == 0 and row0 % n_q == 0
    blk0 = row0 // n_q
    small = lambda shape: pl.BlockSpec(shape, lambda b, t: (0, 0))
    rows = pl.BlockSpec((n_q, ATT_WIDTH), lambda b, t: (b, 0))
    rows_all = pl.BlockSpec((n_q, ATT_WIDTH), lambda b, t: (blk0 + b, 0))
    cache = pl.BlockSpec((1, tk * N_HEADS, HEAD_W), lambda b, t: (b, t, 0))
    return pl.pallas_call(
        functools.partial(_attn_sample_kernel, n_q=n_q, tk=tk),
        grid=(n_b, past // tk),
        in_specs=[
            rows_all, cache, cache, rows_all, rows,
            small((1, ATT_DH)), small((1, ATT_DH)), small((1, ATT_DH)), small((1, ATT_DH)),
            small((1, HEAD_W)),
        ],
        out_specs=rows,
        out_shape=jax.ShapeDtypeStruct((n_b * n_q, ATT_WIDTH), BF16),
        scratch_shapes=[
            pltpu.VMEM((N_HEADS, HEAD_W, 2 * n_q), BF16),
            pltpu.VMEM((N_HEADS, 1, 2 * n_q), F32),
            pltpu.VMEM((N_HEADS, 1, 2 * n_q), F32),
            pltpu.VMEM((N_HEADS, HEAD_W, 2 * n_q), F32),
        ],
        compiler_params=pltpu.CompilerParams(
            dimension_semantics=("arbitrary", "arbitrary"), vmem_limit_bytes=VMEM_LIMIT),
        name="attn_sample",
    )(q_bf, cache_k, cache_v, kn_bf, vn_bf, *lams, subln)


def _outproj_kernel(x_ref, a_ref, b_ref, w_ref, o_ref, wb_ref):
    @pl.when(pl.program_id(0) == 0)
    def _():
        wb_ref[...] = w_ref[...].astype(BF16)

    o_ref[...] = (x_ref[...]
                  + jnp.dot(a_ref[...], wb_ref[:HG_WIDTH, :], preferred_element_type=F32)
                  + jnp.dot(b_ref[...], wb_ref[HG_WIDTH:, :], preferred_element_type=F32))


def _outproj(x, mix_hg, mix_at, w, tm):
    m = x.shape[0]
    row = lambda i: (i, 0)
    return pl.pallas_call(
        _outproj_kernel,
        grid=(m // tm,),
        in_specs=[
            pl.BlockSpec((tm, D_MODEL), row),
            pl.BlockSpec((tm, HG_WIDTH), row),
            pl.BlockSpec((tm, ATT_WIDTH), row),
            pl.BlockSpec((HG_WIDTH + ATT_WIDTH, D_MODEL), lambda i: (0, 0), pipeline_mode=pl.Buffered(1)),
        ],
        out_specs=pl.BlockSpec((tm, D_MODEL), row),
        out_shape=jax.ShapeDtypeStruct((m, D_MODEL), F32),
        scratch_shapes=[pltpu.VMEM((HG_WIDTH + ATT_WIDTH, D_MODEL), BF16)],
        compiler_params=pltpu.CompilerParams(
            dimension_semantics=("arbitrary",), vmem_limit_bytes=VMEM_LIMIT),
        name="outproj",
    )(x, mix_hg, mix_at, w)


def _mlp_kernel(x_ref, gain_ref, wu_ref, wd_ref, gfin_ref, o_ref, h_ref):
    j = pl.program_id(1)

    @pl.when(j == 0)
    def _():
        x = x_ref[...]
        h_ref[...] = _rmsnorm_rows(x, gain_ref[...]).astype(BF16)
        o_ref[...] = x

    u = jnp.dot(h_ref[...], wu_ref[...].astype(BF16), preferred_element_type=F32)
    u = jnp.square(jnp.maximum(u, 0.0)).astype(BF16)
    o_ref[...] += jnp.dot(u, wd_ref[...].astype(BF16), preferred_element_type=F32)

    @pl.when(j == pl.num_programs(1) - 1)
    def _():
        o_ref[...] = _rmsnorm_rows(o_ref[...], gfin_ref[...])


def _mlp(x, gain, wu, wd, gfin, tm, tf):
    m = x.shape[0]
    return pl.pallas_call(
        _mlp_kernel,
        grid=(m // tm, D_FF // tf),
        in_specs=[
            pl.BlockSpec((tm, D_MODEL), lambda i, j: (i, 0)),
            pl.BlockSpec((1, D_MODEL), lambda i, j: (0, 0)),
            pl.BlockSpec((D_MODEL, tf), lambda i, j: (0, j)),
            pl.BlockSpec((tf, D_MODEL), lambda i, j: (j, 0)),
            pl.BlockSpec((1, D_MODEL), lambda i, j: (0, 0)),
        ],
        out_specs=pl.BlockSpec((tm, D_MODEL), lambda i, j: (i, 0)),
        out_shape=jax.ShapeDtypeStruct((m, D_MODEL), F32),
        scratch_shapes=[pltpu.VMEM((tm, D_MODEL), BF16)],
        compiler_params=pltpu.CompilerParams(
            dimension_semantics=("arbitrary", "arbitrary"), vmem_limit_bytes=VMEM_LIMIT),
        name="mlp",
    )(x, gain, wu, wd, gfin)


def kernel(x_prompt, x_sample, cache_k, cache_v, state_hgrn, norm_attn, w_in, lower_bounds, hg_norm,
           lambda_q1, lambda_k1, lambda_q2, lambda_k2, subln, w_out, norm_mlp, w_up, w_down, norm_final):
    depth = w_in.shape[0]
    assert depth == 1
    n_pb, n_p, _ = x_prompt.shape
    n_sb, n_s, _ = x_sample.shape
    assert n_pb == 1
    past = cache_k.shape[2]

    w_in_bf = w_in[0].astype(BF16)
    lams = (lambda_q1, lambda_k1, lambda_q2, lambda_k2)
    gfin = norm_final.reshape(1, D_MODEL)

    xp = x_prompt.reshape(n_p, D_MODEL)
    xs = x_sample.reshape(n_sb * n_s, D_MODEL)

    gates, q_bf, k_bf, vt_bf, vs_bf, kf_p, vf_p, kf_s, vf_s = _inproj(xp, xs, norm_attn, w_in_bf, tm=256)

    s0_p = jnp.zeros((1, N_HEADS, HEAD_W, HEAD_W), F32)
    ohg_p, st_p = _gla(gates, lower_bounds, hg_norm, s0_p, row0=0, n_seq=1, seq_len=n_p, rows=512)
    ohg_s, st_s = _gla(gates, lower_bounds, hg_norm, state_hgrn[0], row0=n_p, n_seq=n_sb, seq_len=n_s, rows=n_s)

    oat_p = _attn_prompt(q_bf, k_bf, vt_bf, lams, subln, n=n_p, t=512, n_grp=2)
    ck = cache_k[0].reshape(n_sb, past * N_HEADS, HEAD_W)
    cv = cache_v[0].reshape(n_sb, past * N_HEADS, HEAD_W)
    oat_s = _attn_sample(q_bf, ck, cv, k_bf, vs_bf, lams, subln, row0=n_p, n_q=n_s, tk=1024)

    x1_p = _outproj(xp, ohg_p, oat_p, w_out[0], tm=512)
    x1_s = _outproj(xs, ohg_s, oat_s, w_out[0], tm=512)

    y_p = _mlp(x1_p, norm_mlp, w_up[0], w_down[0], gfin, tm=1024, tf=512)
    y_s = _mlp(x1_s, norm_mlp, w_up[0], w_down[0], gfin, tm=512, tf=512)

    return (
        y_p.reshape(n_pb, n_p, D_MODEL),
        y_s.reshape(n_sb, n_s, D_MODEL),
        kf_p.reshape(1, n_pb, n_p, N_HEADS, HEAD_W),
        vf_p.reshape(1, n_pb, n_p, N_HEADS, HEAD_W),
        st_p.reshape(1, n_pb, N_HEADS, HEAD_W, HEAD_W),
        kf_s.reshape(1, n_sb, n_s, N_HEADS, HEAD_W),
        vf_s.reshape(1, n_sb, n_s, N_HEADS, HEAD_W),
        st_s.reshape(1, n_sb, N_HEADS, HEAD_W, HEAD_W),
    )
```

```python
import functools
import math

import jax
import jax.numpy as jnp
from jax import lax
from jax.experimental import pallas as pl
from jax.experimental.pallas import tpu as pltpu

F32 = jnp.float32
BF16 = jnp.bfloat16

D_MODEL = 2048
HG_WIDTH = 1024
ATT_WIDTH = 1024
N_HEADS = 8
HEAD_W = 128
ATT_DH = 64
CHUNK = 64
SUB = 8
N_SUB = CHUNK // SUB
MIN_FACTORED_LB = 2.0 ** (-100.0 / SUB)
D_FF = 4 * D_MODEL
N_SEG = 7
EPS = 1e-6
NEG_INF = -1e30
LOG2E = 1.4426950408889634
Q_SCALE = ATT_DH ** -0.5 * LOG2E
LAM_INIT = 0.8 - 0.6 * math.exp(-0.3 * 0)
ONES_ROWS = 16
FINISH_COLS = 256
STAGE_COLS = 256

VMEM_LIMIT = 56 * 1024 * 1024


def _rmsnorm_rows(x, gain):
    return x * lax.rsqrt(jnp.mean(x * x, axis=-1, keepdims=True) + EPS) * gain


def _sigmoid(x):
    return 1.0 / (1.0 + jnp.exp(-x))


def _head(h):
    return slice(h * HEAD_W, (h + 1) * HEAD_W)


def _inproj_kernel(xp_ref, xs_ref, gain_ref, w_ref, gates_ref, q_ref, kb_ref, vt_ref, vbs_ref,
                   kfp_ref, vfp_ref, kfs_ref, vfs_ref, *, n_p):
    def project(x_ref, kf_ref, vf_ref, vb_ref):
        tm = x_ref.shape[0]
        h = _rmsnorm_rows(x_ref[...], gain_ref[...]).astype(BF16)

        def segment(s):
            return jnp.dot(h, w_ref[:, s * HG_WIDTH:(s + 1) * HG_WIDTH], preferred_element_type=F32)

        k = segment(5)
        v = segment(6)
        kb_ref[...] = k.astype(BF16)
        vt_ref[...] = v.T.astype(BF16)
        if vb_ref is not None:
            vb_ref[...] = v.astype(BF16)
        for hd in range(N_HEADS):
            kf_ref[pl.ds(hd, tm, stride=N_HEADS), :] = k[:, _head(hd)]
            vf_ref[pl.ds(hd, tm, stride=N_HEADS), :] = v[:, _head(hd)]
        q_ref[...] = (segment(4) * Q_SCALE).astype(BF16)
        for s in range(4):
            gates_ref[s] = segment(s)

    is_prompt = pl.program_id(0) < n_p

    @pl.when(is_prompt)
    def _():
        project(xp_ref, kfp_ref, vfp_ref, None)

    @pl.when(jnp.logical_not(is_prompt))
    def _():
        project(xs_ref, kfs_ref, vfs_ref, vbs_ref)


def _inproj(xp, xs, gain, w_bf, tm):
    m_p, m_s = xp.shape[0], xs.shape[0]
    n_p, n_s = m_p // tm, m_s // tm
    m = m_p + m_s
    seg = HG_WIDTH
    row = lambda i: (i, 0)
    p_row = lambda i: (jnp.minimum(i, n_p - 1), 0)
    s_row = lambda i: (jnp.maximum(i - n_p, 0), 0)
    once = dict(pipeline_mode=pl.Buffered(1))
    return pl.pallas_call(
        functools.partial(_inproj_kernel, n_p=n_p),
        grid=(n_p + n_s,),
        in_specs=[
            pl.BlockSpec((tm, D_MODEL), p_row),
            pl.BlockSpec((tm, D_MODEL), s_row, **once),
            pl.BlockSpec((1, D_MODEL), lambda i: (0, 0)),
            pl.BlockSpec((D_MODEL, N_SEG * seg), lambda i: (0, 0), **once),
        ],
        out_specs=[
            pl.BlockSpec((4, tm, seg), lambda i: (0, i, 0)),
            pl.BlockSpec((tm, seg), row),
            pl.BlockSpec((tm, seg), row),
            pl.BlockSpec((seg, tm), lambda i: (0, i)),
            pl.BlockSpec((tm, seg), s_row, **once),
            pl.BlockSpec((tm * N_HEADS, HEAD_W), p_row),
            pl.BlockSpec((tm * N_HEADS, HEAD_W), p_row),
            pl.BlockSpec((tm * N_HEADS, HEAD_W), s_row, **once),
            pl.BlockSpec((tm * N_HEADS, HEAD_W), s_row, **once),
        ],
        out_shape=[
            jax.ShapeDtypeStruct((4, m, seg), F32),
            jax.ShapeDtypeStruct((m, seg), BF16),
            jax.ShapeDtypeStruct((m, seg), BF16),
            jax.ShapeDtypeStruct((seg, m), BF16),
            jax.ShapeDtypeStruct((m_s, seg), BF16),
            jax.ShapeDtypeStruct((m_p * N_HEADS, HEAD_W), F32),
            jax.ShapeDtypeStruct((m_p * N_HEADS, HEAD_W), F32),
            jax.ShapeDtypeStruct((m_s * N_HEADS, HEAD_W), F32),
            jax.ShapeDtypeStruct((m_s * N_HEADS, HEAD_W), F32),
        ],
        compiler_params=pltpu.CompilerParams(
            dimension_semantics=("arbitrary",), vmem_limit_bytes=VMEM_LIMIT),
        name="inproj",
    )(xp, xs, gain, w_bf)


def _gla_kernel(hq_ref, hf_ref, hi_ref, hg_ref, lbnd_ref, gn_ref, s0_ref, o_ref, sout_ref, st_ref,
                q_ref, b_ref, *, n_chunks):
    r = pl.program_id(1)

    @pl.when(r == 0)
    def _():
        st_ref[...] = jnp.concatenate([s0_ref[0, h].T for h in range(N_HEADS)], axis=1)

    lbs = lbnd_ref[...]
    e = jnp.exp(lbs - jnp.max(lbs, axis=0, keepdims=True))
    lb = e[0:1, :] / jnp.sum(e, axis=0, keepdims=True)
    gn = jnp.concatenate([gn_ref[...]] * N_HEADS, axis=1)

    ri = lax.broadcasted_iota(jnp.int32, (CHUNK, CHUNK), 0)
    ci = lax.broadcasted_iota(jnp.int32, (CHUNK, CHUNK), 1)
    tril = (ri >= ci).astype(F32)
    pr = lax.broadcasted_iota(jnp.int32, (2 * HEAD_W, 2 * HEAD_W), 0)
    pc = lax.broadcasted_iota(jnp.int32, (2 * HEAD_W, 2 * HEAD_W), 1)
    pair_ones = ((pr >> 7) == (pc >> 7)).astype(BF16)
    gr = lax.broadcasted_iota(jnp.int32, (CHUNK, CHUNK * SUB), 0)
    gc = lax.broadcasted_iota(jnp.int32, (CHUNK, CHUNK * SUB), 1)
    seg_sum = (((gc >> 3) == gr) & ((gc & (SUB - 1)) <= (gr & (SUB - 1)))).astype(BF16)
    n_off = (N_SUB - 1) * SUB
    n_key = SUB * (N_SUB * (N_SUB - 1) // 2)
    orow = lax.broadcasted_iota(jnp.int32, (n_off, n_key), 0) >> 3
    ocol = lax.broadcasted_iota(jnp.int32, (n_off, n_key), 1)
    ocol_seg = sum((ocol >= 4 * i * (i - 1)).astype(jnp.int32) for i in range(2, N_SUB))
    off_mask = orow == ocol_seg
    nt = (((1,), (1,)), ((), ()))
    tn = (((0,), (0,)), ((), ()))

    n_all = n_key + CHUNK
    frow = lax.broadcasted_iota(jnp.int32, (CHUNK, n_all), 0)
    fcol = lax.broadcasted_iota(jnp.int32, (CHUNK, n_all), 1)
    fseg = sum((fcol >= 4 * i * (i - 1)).astype(jnp.int32) for i in range(2, N_SUB))
    fsame = fcol - n_key
    all_mask = (((fcol < n_key) & (fseg + 1 == (frow >> 3)))
                | ((fcol >= n_key) & ((fsame >> 3) == (frow >> 3)) & ((fsame & (SUB - 1)) <= (frow & (SUB - 1)))))

    def chunk(c, carry, factored):
        r0 = pl.multiple_of(c * CHUNK, CHUNK)
        hq = hq_ref[0, pl.ds(r0, CHUNK), :]
        hf = hf_ref[0, pl.ds(r0, CHUNK), :]
        v = hi_ref[0, pl.ds(r0, CHUNK), :]
        hg = hg_ref[0, pl.ds(r0, CHUNK), :]

        f = lb + (1.0 - lb) * _sigmoid(hf)
        g = jnp.log(f) * LOG2E
        kk = 1.0 - f
        q = hq * _sigmoid(hq)
        b = jnp.dot(tril, g, precision=lax.Precision.HIGHEST, preferred_element_type=F32)
        v_bf = v.astype(BF16)

        st = st_ref[...]
        st_bf = st.astype(BF16)
        b_last = b[CHUNK - 1:CHUNK, :]
        q_dec = q * jnp.exp2(b)
        q_in = q_dec.astype(BF16)
        k_dec = (kk * jnp.exp2(b_last - b)).astype(BF16)
        o_inter = jnp.concatenate(
            [lax.dot_general(q_in[:, _head(h)], st_bf[:, _head(h)], nt, preferred_element_type=F32)
             for h in range(N_HEADS)], axis=1)
        upd = jnp.concatenate(
            [lax.dot_general(v_bf[:, _head(h)], k_dec[:, _head(h)], tn, preferred_element_type=F32)
             for h in range(N_HEADS)], axis=1)
        st_ref[...] = jnp.exp2(b_last) * st + upd

        qt, kh, vh = [], [], []
        for i in range(1, N_SUB):
            lo = i * SUB
            b_start = b[lo - 1:lo, :]
            qt.append(q[lo:lo + SUB] * jnp.exp2(b[lo:lo + SUB] - b_start))
            kh.append(kk[:lo] * jnp.exp2(b_start - b[:lo]))
            vh.append(v[:lo])

        if factored:
            b0 = jnp.concatenate(
                [jnp.zeros((SUB, HG_WIDTH), F32)]
                + [jnp.broadcast_to(b[i * SUB - 1:i * SUB, :], (SUB, HG_WIDTH)) for i in range(1, N_SUB)], axis=0)
            kd = kk * jnp.exp2(b0 - b)
            q_all = jnp.concatenate([q_dec[:SUB]] + qt, axis=0).astype(BF16)
            k_all = jnp.concatenate(kh + [kd], axis=0).astype(BF16)
            v_all = jnp.concatenate(vh + [v], axis=0).astype(BF16)
            a_all = [lax.dot_general(q_all[:, _head(h)], k_all[:, _head(h)], nt, preferred_element_type=F32)
                     for h in range(N_HEADS)]
            o_intra = jnp.concatenate(
                [jnp.dot(jnp.where(all_mask, a_all[h], 0.0).astype(BF16), v_all[:, _head(h)],
                         preferred_element_type=F32) for h in range(N_HEADS)], axis=1)
        else:
            for h in range(N_HEADS):
                q_ref[h] = q[:, _head(h)]
                b_ref[h] = b[:, _head(h)]
            qt = jnp.concatenate(qt, axis=0).astype(BF16)
            kh = jnp.concatenate(kh, axis=0).astype(BF16)
            vh = jnp.concatenate(vh, axis=0).astype(BF16)
            a_off = [lax.dot_general(qt[:, _head(h)], kh[:, _head(h)], nt, preferred_element_type=F32)
                     for h in range(N_HEADS)]

            a_rep = []
            for j in range(N_HEADS // 2):
                p_pair = []
                for h in (2 * j, 2 * j + 1):
                    kk_h = kk[:, _head(h)]
                    b_h = b[:, _head(h)]
                    rows = []
                    for r in range(CHUNK):
                        lo = r - r % SUB
                        q_row = q_ref[h, pl.ds(r, SUB, stride=0), :]
                        b_row = b_ref[h, pl.ds(r, SUB, stride=0), :]
                        rows.append((q_row * kk_h[lo:lo + SUB])
                                    * jnp.exp2(jnp.minimum(b_row - b_h[lo:lo + SUB], 0.0)))
                    p_pair.append(jnp.concatenate(rows, axis=0).astype(BF16))
                a_rep.append(jnp.dot(jnp.concatenate(p_pair, axis=1), pair_ones,
                                     preferred_element_type=F32))

            o_off = jnp.concatenate(
                [jnp.dot(jnp.where(off_mask, a_off[h], 0.0).astype(BF16), vh[:, _head(h)],
                         preferred_element_type=F32) for h in range(N_HEADS)], axis=1)
            o_diag = []
            for j in range(N_HEADS // 2):
                pair = slice(2 * HEAD_W * j, 2 * HEAD_W * (j + 1))
                v_rep = jnp.broadcast_to(v[:, pair].reshape(N_SUB, 1, SUB, 2 * HEAD_W),
                                         (N_SUB, SUB, SUB, 2 * HEAD_W))
                w = (a_rep[j].reshape(N_SUB, SUB, SUB, 2 * HEAD_W) * v_rep).reshape(CHUNK * SUB, 2 * HEAD_W)
                o_diag.append(jnp.dot(seg_sum, w.astype(BF16), preferred_element_type=F32))
            o_intra = jnp.concatenate(o_diag, axis=1) + jnp.concatenate(
                [jnp.zeros((SUB, HG_WIDTH), F32), o_off], axis=0)
        o = o_inter + o_intra

        y = jnp.concatenate(
            [o[:, _head(h)] * lax.rsqrt(jnp.mean(o[:, _head(h)] * o[:, _head(h)], axis=-1, keepdims=True) + EPS)
             for h in range(N_HEADS)], axis=1)
        y = y * gn * (hg * _sigmoid(hg))
        o_ref[pl.ds(r0, CHUNK), :] = y.astype(BF16)
        return carry

    factor_ok = jnp.min(lb) > MIN_FACTORED_LB

    @pl.when(factor_ok)
    def _():
        lax.fori_loop(0, n_chunks, functools.partial(chunk, factored=True), 0, unroll=min(n_chunks, 4))

    @pl.when(jnp.logical_not(factor_ok))
    def _():
        lax.fori_loop(0, n_chunks, functools.partial(chunk, factored=False), 0)

    @pl.when(r == pl.num_programs(1) - 1)
    def _():
        st = st_ref[...]
        for h in range(N_HEADS):
            sout_ref[0, h] = st[:, _head(h)].T


def _gla(gates, lower_bounds, hg_norm, s0, row0, n_seq, seq_len, rows):
    m = n_seq * seq_len
    nr = seq_len // rows
    blk0 = row0 // rows
    seg = lambda s: pl.BlockSpec((1, rows, HG_WIDTH), lambda b, r: (s, blk0 + b * nr + r, 0))
    state = pl.BlockSpec((1, N_HEADS, HEAD_W, HEAD_W), lambda b, r: (b, 0, 0, 0))
    return pl.pallas_call(
        functools.partial(_gla_kernel, n_chunks=rows // CHUNK),
        grid=(n_seq, nr),
        in_specs=[
            seg(0), seg(1), seg(2), seg(3),
            pl.BlockSpec((lower_bounds.shape[0], HG_WIDTH), lambda b, r: (0, 0)),
            pl.BlockSpec((1, HEAD_W), lambda b, r: (0, 0)),
            state,
        ],
        out_specs=[pl.BlockSpec((rows, HG_WIDTH), lambda b, r: (b * nr + r, 0)), state],
        out_shape=[
            jax.ShapeDtypeStruct((m, HG_WIDTH), BF16),
            jax.ShapeDtypeStruct((n_seq, N_HEADS, HEAD_W, HEAD_W), F32),
        ],
        scratch_shapes=[pltpu.VMEM((HEAD_W, HG_WIDTH), F32),
                        pltpu.VMEM((N_HEADS, CHUNK, HEAD_W), F32),
                        pltpu.VMEM((N_HEADS, CHUNK, HEAD_W), F32)],
        compiler_params=pltpu.CompilerParams(
            dimension_semantics=("arbitrary", "arbitrary"), vmem_limit_bytes=VMEM_LIMIT),
        name="hgrn2",
    )(gates, gates, gates, gates, lower_bounds, hg_norm, s0)


def _stacked_query(q):
    lane = lax.broadcasted_iota(jnp.int32, q.shape, 1)
    qbig = jnp.concatenate([jnp.where(lane < ATT_DH, q, 0.0), jnp.where(lane >= ATT_DH, q, 0.0)], axis=0)
    return qbig.T.astype(BF16)


def _lambda(lq1_ref, lk1_ref, lq2_ref, lk2_ref):
    s1 = jnp.sum(lq1_ref[...] * lk1_ref[...], axis=-1, keepdims=True)
    s2 = jnp.sum(lq2_ref[...] * lk2_ref[...], axis=-1, keepdims=True)
    return jnp.exp(s1) - jnp.exp(s2) + LAM_INIT


def _attn_finish(acc, l, lam, sub, n):
    o_both = (acc * (1.0 / l)).T
    o = o_both[:n] - lam * o_both[n:]
    return _rmsnorm_rows(o, sub) * (1.0 - LAM_INIT)


def _attn_prompt_kernel(q_ref, qn_ref, k_ref, vt_ref, lq1_ref, lk1_ref, lq2_ref, lk2_ref, sub_ref, o_ref,
                        qq_ref, s_ref, smax_ref, mask_ref, m_ref, acc_ref, *, t, n_grp):
    qi = pl.program_id(1)
    grp = range(n_grp)
    slot = qi % 2
    m_ref[...] = jnp.full(m_ref.shape, NEG_INF, F32)
    acc_ref[...] = jnp.zeros(acc_ref.shape, F32)
    ones_rows = jnp.ones((ONES_ROWS, t), BF16)

    def values_t(g, kv):
        k0 = pl.multiple_of(kv * t, t)
        return jnp.concatenate([vt_ref[_head(g), pl.ds(k0, t)], ones_rows], axis=0)

    def first_scores(g, qslot, cols):
        s = jnp.dot(k_ref[0:t, _head(g)], qq_ref[qslot, g, :, cols], preferred_element_type=F32)
        s_ref[g, 0, :, cols] = s
        smax_ref[g, 0, :, cols] = jnp.max(s, axis=0, keepdims=True)

    def stage(kv, cur, oth):
        k_next = pl.multiple_of((kv + 1) * t, t)
        vt = [values_t(g, kv) for g in grp]
        for c in range(0, 2 * t, STAGE_COLS):
            cols = slice(c, c + STAGE_COLS)
            for g in grp:
                s_next = jnp.dot(k_ref[pl.ds(k_next, t), _head(g)], qq_ref[slot, g, :, cols],
                                 preferred_element_type=F32)
                s_ref[g, oth, :, cols] = s_next
                smax_ref[g, oth, :, cols] = jnp.max(s_next, axis=0, keepdims=True)
                m_prev = m_ref[g, :, cols]
                m_new = jnp.maximum(m_prev, smax_ref[g, cur, :, cols])
                m_ref[g, :, cols] = m_new
                p = jnp.exp2(s_ref[g, cur, :, cols] - m_new).astype(BF16)
                acc_ref[g, :, cols] = (jnp.exp2(m_prev - m_new) * acc_ref[g, :, cols]
                                       + jnp.dot(vt[g], p, preferred_element_type=F32))

    @pl.when((pl.program_id(0) == 0) & (qi == 0))
    def _():
        kpos = lax.broadcasted_iota(jnp.int32, mask_ref.shape, 0)
        col = lax.broadcasted_iota(jnp.int32, mask_ref.shape, 1)
        qpos = jnp.where(col >= t, col - t, col)
        mask_ref[...] = jnp.where((kpos >> 6) <= (qpos >> 6), 0.0, NEG_INF)

    def finish(cur):
        for g in grp:
            qq_ref[1 - slot, g] = _stacked_query(qn_ref[:, _head(g)].astype(F32))
        vt = [values_t(g, qi) for g in grp]
        acc = [[] for _ in grp]
        for c in range(0, 2 * t, FINISH_COLS):
            cols = slice(c, c + FINISH_COLS)
            for g in grp:
                s = s_ref[g, cur, :, cols] + mask_ref[:, cols]
                m_prev = m_ref[g, :, cols]
                m_new = jnp.maximum(m_prev, jnp.max(s, axis=0, keepdims=True))
                p = jnp.exp2(s - m_new).astype(BF16)
                acc[g].append(jnp.exp2(m_prev - m_new) * acc_ref[g, :, cols]
                              + jnp.dot(vt[g], p, preferred_element_type=F32))
                first_scores(g, 1 - slot, cols)
        lam = _lambda(lq1_ref, lk1_ref, lq2_ref, lk2_ref)
        for g in grp:
            a = jnp.concatenate(acc[g], axis=1)
            o_ref[:, _head(g)] = _attn_finish(a[:HEAD_W], a[HEAD_W:HEAD_W + 1], lam, sub_ref[...],
                                              t).astype(BF16)

    @pl.when(qi == 0)
    def _():
        for g in grp:
            qq_ref[0, g] = _stacked_query(q_ref[:, _head(g)].astype(F32))
        for c in range(0, 2 * t, FINISH_COLS):
            for g in grp:
                first_scores(g, 0, slice(c, c + FINISH_COLS))

    def pair(j, carry):
        stage(2 * j, 0, 1)
        stage(2 * j + 1, 1, 0)
        return carry

    lax.fori_loop(0, qi // 2, pair, 0)

    @pl.when(qi % 2 == 1)
    def _():
        stage(qi - 1, 0, 1)
        finish(1)

    @pl.when(qi % 2 == 0)
    def _():
        finish(0)


def _attn_prompt(q_bf, k_bf, vt_bf, lams, subln, n, t, n_grp):
    w = n_grp * HEAD_W
    small = lambda shape: pl.BlockSpec(shape, lambda h, i: (0, 0))
    return pl.pallas_call(
        functools.partial(_attn_prompt_kernel, t=t, n_grp=n_grp),
        grid=(N_HEADS // n_grp, n // t),
        in_specs=[
            pl.BlockSpec((t, w), lambda h, i: (i, h)),
            pl.BlockSpec((t, w), lambda h, i: (jnp.minimum(i + 1, n // t - 1), h)),
            pl.BlockSpec((n, w), lambda h, i: (0, h)),
            pl.BlockSpec((w, n), lambda h, i: (h, 0)),
            small((1, ATT_DH)), small((1, ATT_DH)), small((1, ATT_DH)), small((1, ATT_DH)),
            small((1, HEAD_W)),
        ],
        out_specs=pl.BlockSpec((t, w), lambda h, i: (i, h)),
        out_shape=jax.ShapeDtypeStruct((n, ATT_WIDTH), BF16),
        scratch_shapes=[
            pltpu.VMEM((2, n_grp, HEAD_W, 2 * t), BF16),
            pltpu.VMEM((n_grp, 2, t, 2 * t), F32),
            pltpu.VMEM((n_grp, 2, 1, 2 * t), F32),
            pltpu.VMEM((t, 2 * t), F32),
            pltpu.VMEM((n_grp, 1, 2 * t), F32),
            pltpu.VMEM((n_grp, HEAD_W + ONES_ROWS, 2 * t), F32),
        ],
        compiler_params=pltpu.CompilerParams(
            dimension_semantics=("arbitrary", "arbitrary"), vmem_limit_bytes=VMEM_LIMIT),
        name="attn_prompt",
    )(q_bf, q_bf, k_bf, vt_bf, *lams, subln)


def _attn_sample_kernel(q_ref, kc_ref, vc_ref, kn_ref, vn_ref, lq1_ref, lk1_ref, lq2_ref, lk2_ref,
                        sub_ref, o_ref, qq_ref, m_ref, l_ref, acc_ref, *, n_q, tk):
    t = pl.program_id(1)
    pairs = range(N_HEADS // 2)
    w2 = 2 * HEAD_W

    @pl.when(t == 0)
    def _():
        zero = jnp.zeros((HEAD_W, 2 * n_q), BF16)
        for j in pairs:
            qa = _stacked_query(q_ref[:, _head(2 * j)].astype(F32))
            qb = _stacked_query(q_ref[:, _head(2 * j + 1)].astype(F32))
            qq_ref[j] = jnp.concatenate([jnp.concatenate([qa, zero], axis=1),
                                         jnp.concatenate([zero, qb], axis=1)], axis=0)
        m_ref[...] = jnp.full(m_ref.shape, NEG_INF, F32)
        l_ref[...] = jnp.zeros(l_ref.shape, F32)
        acc_ref[...] = jnp.zeros(acc_ref.shape, F32)

    def update(k, v):
        s = [jnp.dot(k[j], qq_ref[j], preferred_element_type=F32) for j in pairs]
        p, alpha = [], []
        for j in pairs:
            m_prev = m_ref[j]
            m_new = jnp.maximum(m_prev, jnp.max(s[j], axis=0, keepdims=True))
            m_ref[j] = m_new
            a = jnp.exp2(m_prev - m_new)
            pj = jnp.exp2(s[j] - m_new)
            l_ref[j] = a * l_ref[j] + jnp.sum(pj, axis=0, keepdims=True)
            p.append(pj.astype(BF16))
            alpha.append(a)
        pv = [lax.dot_general(v[j], p[j], (((0,), (0,)), ((), ())), preferred_element_type=F32)
              for j in pairs]
        for j in pairs:
            acc_ref[j] = alpha[j] * acc_ref[j] + pv[j]

    def cache_pair(ref, j):
        return jnp.concatenate([ref[0, pl.ds(2 * j, tk, stride=N_HEADS), :],
                                ref[0, pl.ds(2 * j + 1, tk, stride=N_HEADS), :]], axis=1).astype(BF16)

    update([cache_pair(kc_ref, j) for j in pairs], [cache_pair(vc_ref, j) for j in pairs])

    @pl.when(t == pl.num_programs(1) - 1)
    def _():
        update([kn_ref[:, w2 * j:w2 * (j + 1)] for j in pairs], [vn_ref[:, w2 * j:w2 * (j + 1)] for j in pairs])
        lam = _lambda(lq1_ref, lk1_ref, lq2_ref, lk2_ref)
        for j in pairs:
            acc = acc_ref[j]
            l = l_ref[j]
            for i, h in enumerate((2 * j, 2 * j + 1)):
                o_ref[:, _head(h)] = _attn_finish(acc[_head(i), _head(i)], l[:, _head(i)], lam, sub_ref[...],
                                                  n_q).astype(BF16)


def _attn_sample(q_bf, cache_k, cache_v, kn_bf, vn_bf, lams, subln, row0, n_q, tk):
    n_b = cache_k.shape[0]
    past = cache_k.shape[1] // N_HEADS
    assert past % CHUNK == 0 and n_q <= CHUNK and past % tk == 0 and row0 % n_q == 0
    blk0 = row0 // n_q
    small = lambda shape: pl.BlockSpec(shape, lambda b, t: (0, 0))
    rows = pl.BlockSpec((n_q, ATT_WIDTH), lambda b, t: (b, 0))
    rows_all = pl.BlockSpec((n_q, ATT_WIDTH), lambda b, t: (blk0 + b, 0))
    cache = pl.BlockSpec((1, tk * N_HEADS, HEAD_W), lambda b, t: (b, t, 0))
    return pl.pallas_call(
        functools.partial(_attn_sample_kernel, n_q=n_q, tk=tk),
        grid=(n_b, past // tk),
        in_specs=[
            rows_all, cache, cache, rows_all, rows,
            small((1, ATT_DH)), small((1, ATT_DH)), small((1, ATT_DH)), small((1, ATT_DH)),
            small((1, HEAD_W)),
        ],
        out_specs=rows,
        out_shape=jax.ShapeDtypeStruct((n_b * n_q, ATT_WIDTH), BF16),
        scratch_shapes=[
            pltpu.VMEM((N_HEADS // 2, 2 * HEAD_W, 4 * n_q), BF16),
            pltpu.VMEM((N_HEADS // 2, 1, 4 * n_q), F32),
            pltpu.VMEM((N_HEADS // 2, 1, 4 * n_q), F32),
            pltpu.VMEM((N_HEADS // 2, 2 * HEAD_W, 4 * n_q), F32),
        ],
        compiler_params=pltpu.CompilerParams(
            dimension_semantics=("arbitrary", "arbitrary"), vmem_limit_bytes=VMEM_LIMIT),
        name="attn_sample",
    )(q_bf, cache_k, cache_v, kn_bf, vn_bf, *lams, subln)


def _outproj_kernel(x_ref, a_ref, b_ref, w_ref, o_ref, wb_ref):
    @pl.when(pl.program_id(0) == 0)
    def _():
        wb_ref[...] = w_ref[...].astype(BF16)

    o_ref[...] = (x_ref[...]
                  + jnp.dot(a_ref[...], wb_ref[:HG_WIDTH, :], preferred_element_type=F32)
                  + jnp.dot(b_ref[...], wb_ref[HG_WIDTH:, :], preferred_element_type=F32))


def _outproj(x, mix_hg, mix_at, w, tm):
    m = x.shape[0]
    row = lambda i: (i, 0)
    return pl.pallas_call(
        _outproj_kernel,
        grid=(m // tm,),
        in_specs=[
            pl.BlockSpec((tm, D_MODEL), row),
            pl.BlockSpec((tm, HG_WIDTH), row),
            pl.BlockSpec((tm, ATT_WIDTH), row),
            pl.BlockSpec((HG_WIDTH + ATT_WIDTH, D_MODEL), lambda i: (0, 0), pipeline_mode=pl.Buffered(1)),
        ],
        out_specs=pl.BlockSpec((tm, D_MODEL), row),
        out_shape=jax.ShapeDtypeStruct((m, D_MODEL), F32),
        scratch_shapes=[pltpu.VMEM((HG_WIDTH + ATT_WIDTH, D_MODEL), BF16)],
        compiler_params=pltpu.CompilerParams(
            dimension_semantics=("arbitrary",), vmem_limit_bytes=VMEM_LIMIT),
        name="outproj",
    )(x, mix_hg, mix_at, w)


def _mlp_kernel(x_ref, gain_ref, wu_ref, wd_ref, gfin_ref, o_ref, h_ref):
    j = pl.program_id(1)

    @pl.when(j == 0)
    def _():
        x = x_ref[...]
        h_ref[...] = _rmsnorm_rows(x, gain_ref[...]).astype(BF16)
        o_ref[...] = x

    u = jnp.dot(h_ref[...], wu_ref[...].astype(BF16), preferred_element_type=F32)
    u = jnp.square(jnp.maximum(u, 0.0)).astype(BF16)
    o_ref[...] += jnp.dot(u, wd_ref[...].astype(BF16), preferred_element_type=F32)

    @pl.when(j == pl.num_programs(1) - 1)
    def _():
        o_ref[...] = _rmsnorm_rows(o_ref[...], gfin_ref[...])


def _mlp(x, gain, wu, wd, gfin, tm, tf):
    m = x.shape[0]
    return pl.pallas_call(
        _mlp_kernel,
        grid=(m // tm, D_FF // tf),
        in_specs=[
            pl.BlockSpec((tm, D_MODEL), lambda i, j: (i, 0)),
            pl.BlockSpec((1, D_MODEL), lambda i, j: (0, 0)),
            pl.BlockSpec((D_MODEL, tf), lambda i, j: (0, j)),
            pl.BlockSpec((tf, D_MODEL), lambda i, j: (j, 0)),
            pl.BlockSpec((1, D_MODEL), lambda i, j: (0, 0)),
        ],
        out_specs=pl.BlockSpec((tm, D_MODEL), lambda i, j: (i, 0)),
        out_shape=jax.ShapeDtypeStruct((m, D_MODEL), F32),
        scratch_shapes=[pltpu.VMEM((tm, D_MODEL), BF16)],
        compiler_params=pltpu.CompilerParams(
            dimension_semantics=("arbitrary", "arbitrary"), vmem_limit_bytes=VMEM_LIMIT),
        name="mlp",
    )(x, gain, wu, wd, gfin)


def kernel(x_prompt, x_sample, cache_k, cache_v, state_hgrn, norm_attn, w_in, lower_bounds, hg_norm,
           lambda_q1, lambda_k1, lambda_q2, lambda_k2, subln, w_out, norm_mlp, w_up, w_down, norm_final):
    depth = w_in.shape[0]
    assert depth == 1
    n_pb, n_p, _ = x_prompt.shape
    n_sb, n_s, _ = x_sample.shape
    assert n_pb == 1
    past = cache_k.shape[2]

    w_in_bf = w_in[0].astype(BF16)
    lams = (lambda_q1, lambda_k1, lambda_q2, lambda_k2)
    gfin = norm_final.reshape(1, D_MODEL)

    xp = x_prompt.reshape(n_p, D_MODEL)
    xs = x_sample.reshape(n_sb * n_s, D_MODEL)

    gates, q_bf, k_bf, vt_bf, vs_bf, kf_p, vf_p, kf_s, vf_s = _inproj(xp, xs, norm_attn, w_in_bf, tm=256)

    s0_p = jnp.zeros((1, N_HEADS, HEAD_W, HEAD_W), F32)
    ohg_p, st_p = _gla(gates, lower_bounds, hg_norm, s0_p, row0=0, n_seq=1, seq_len=n_p, rows=512)
    ohg_s, st_s = _gla(gates, lower_bounds, hg_norm, state_hgrn[0], row0=n_p, n_seq=n_sb, seq_len=n_s, rows=n_s)

    oat_p = _attn_prompt(q_bf, k_bf, vt_bf, lams, subln, n=n_p, t=512, n_grp=2)
    ck = cache_k[0].reshape(n_sb, past * N_HEADS, HEAD_W)
    cv = cache_v[0].reshape(n_sb, past * N_HEADS, HEAD_W)
    oat_s = _attn_sample(q_bf, ck, cv, k_bf, vs_bf, lams, subln, row0=n_p, n_q=n_s, tk=1024)

    x1_p = _outproj(xp, ohg_p, oat_p, w_out[0], tm=512)
    x1_s = _outproj(xs, ohg_s, oat_s, w_out[0], tm=512)

    y_p = _mlp(x1_p, norm_mlp, w_up[0], w_down[0], gfin, tm=1024, tf=512)
    y_s = _mlp(x1_s, norm_mlp, w_up[0], w_down[0], gfin, tm=512, tf=512)

    return (
        y_p.reshape(n_pb, n_p, D_MODEL),
        y_s.reshape(n_sb, n_s, D_MODEL),
        kf_p.reshape(1, n_pb, n_p, N_HEADS, HEAD_W),
        vf_p.reshape(1, n_pb, n_p, N_HEADS, HEAD_W),
        st_p.reshape(1, n_pb, N_HEADS, HEAD_W, HEAD_W),
        kf_s.reshape(1, n_sb, n_s, N_HEADS, HEAD_W),
        vf_s.reshape(1, n_sb, n_s, N_HEADS, HEAD_W),
        st_s.reshape(1, n_sb, N_HEADS, HEAD_W, HEAD_W),
    )
```

```python
import functools
import math

import jax
import jax.numpy as jnp
from jax import lax
from jax.experimental import pallas as pl
from jax.experimental.pallas import tpu as pltpu

F32 = jnp.float32
BF16 = jnp.bfloat16

D_MODEL = 2048
HG_WIDTH = 1024
ATT_WIDTH = 1024
N_HEADS = 8
HEAD_W = 128
ATT_DH = 64
CHUNK = 64
SUB = 8
N_SUB = CHUNK // SUB
MIN_FACTORED_LB = 2.0 ** (-100.0 / SUB)
D_FF = 4 * D_MODEL
N_SEG = 7
EPS = 1e-6
NEG_INF = -1e30
LOG2E = 1.4426950408889634
Q_SCALE = ATT_DH ** -0.5 * LOG2E
LAM_INIT = 0.8 - 0.6 * math.exp(-0.3 * 0)
ONES_ROWS = 16
FINISH_COLS = 256
STAGE_COLS = 256
ROW_PAD = 128

VMEM_LIMIT = 56 * 1024 * 1024


def _rmsnorm_rows(x, gain):
    return x * lax.rsqrt(jnp.mean(x * x, axis=-1, keepdims=True) + EPS) * gain


def _sigmoid(x):
    return 1.0 / (1.0 + jnp.exp(-x))


def _head(h):
    return slice(h * HEAD_W, (h + 1) * HEAD_W)


def _inproj_kernel(xp_ref, xs_ref, gain_ref, w_ref, gates_ref, q_ref, kb_ref, vt_ref, vbs_ref,
                   kfp_ref, vfp_ref, kfs_ref, vfs_ref, *, n_p):
    def project(x_ref, kf_ref, vf_ref, vb_ref):
        tm = x_ref.shape[0]
        h = _rmsnorm_rows(x_ref[...], gain_ref[...]).astype(BF16)

        def segment(s):
            return jnp.dot(h, w_ref[:, s * HG_WIDTH:(s + 1) * HG_WIDTH], preferred_element_type=F32)

        k = segment(5)
        v = segment(6)
        kb_ref[...] = k.astype(BF16)
        vt_ref[...] = v.T.astype(BF16)
        if vb_ref is not None:
            vb_ref[...] = v.astype(BF16)
        for hd in range(N_HEADS):
            kf_ref[pl.ds(hd, tm, stride=N_HEADS), :] = k[:, _head(hd)]
            vf_ref[pl.ds(hd, tm, stride=N_HEADS), :] = v[:, _head(hd)]
        q_ref[...] = (segment(4) * Q_SCALE).astype(BF16)
        for s in range(4):
            gates_ref[s] = segment(s)

    is_prompt = pl.program_id(0) < n_p

    @pl.when(is_prompt)
    def _():
        project(xp_ref, kfp_ref, vfp_ref, None)

    @pl.when(jnp.logical_not(is_prompt))
    def _():
        project(xs_ref, kfs_ref, vfs_ref, vbs_ref)


def _inproj(xp, xs, gain, w_bf, tm):
    m_p, m_s = xp.shape[0], xs.shape[0]
    n_p, n_s = m_p // tm, m_s // tm
    m = m_p + m_s
    seg = HG_WIDTH
    row = lambda i: (i, 0)
    p_row = lambda i: (jnp.minimum(i, n_p - 1), 0)
    s_row = lambda i: (jnp.maximum(i - n_p, 0), 0)
    once = dict(pipeline_mode=pl.Buffered(1))
    return pl.pallas_call(
        functools.partial(_inproj_kernel, n_p=n_p),
        grid=(n_p + n_s,),
        in_specs=[
            pl.BlockSpec((tm, D_MODEL), p_row),
            pl.BlockSpec((tm, D_MODEL), s_row, **once),
            pl.BlockSpec((1, D_MODEL), lambda i: (0, 0)),
            pl.BlockSpec((D_MODEL, N_SEG * seg), lambda i: (0, 0), **once),
        ],
        out_specs=[
            pl.BlockSpec((4, tm, seg), lambda i: (0, i, 0)),
            pl.BlockSpec((tm, seg), row),
            pl.BlockSpec((tm, seg), row),
            pl.BlockSpec((seg, tm), lambda i: (0, i)),
            pl.BlockSpec((tm, seg), s_row, **once),
            pl.BlockSpec((tm * N_HEADS, HEAD_W), p_row),
            pl.BlockSpec((tm * N_HEADS, HEAD_W), p_row),
            pl.BlockSpec((tm * N_HEADS, HEAD_W), s_row, **once),
            pl.BlockSpec((tm * N_HEADS, HEAD_W), s_row, **once),
        ],
        out_shape=[
            jax.ShapeDtypeStruct((4, m, seg), F32),
            jax.ShapeDtypeStruct((m, seg), BF16),
            jax.ShapeDtypeStruct((m, seg), BF16),
            jax.ShapeDtypeStruct((seg, m), BF16),
            jax.ShapeDtypeStruct((m_s, seg), BF16),
            jax.ShapeDtypeStruct((m_p * N_HEADS, HEAD_W), F32),
            jax.ShapeDtypeStruct((m_p * N_HEADS, HEAD_W), F32),
            jax.ShapeDtypeStruct((m_s * N_HEADS, HEAD_W), F32),
            jax.ShapeDtypeStruct((m_s * N_HEADS, HEAD_W), F32),
        ],
        compiler_params=pltpu.CompilerParams(
            dimension_semantics=("arbitrary",), vmem_limit_bytes=VMEM_LIMIT),
        name="inproj",
    )(xp, xs, gain, w_bf)


def _gla_kernel(hq_ref, hf_ref, hi_ref, hg_ref, lbnd_ref, gn_ref, s0_ref, o_ref, sout_ref, st_ref,
                q_ref, b_ref, *, n_chunks):
    r = pl.program_id(1)

    @pl.when(r == 0)
    def _():
        st_ref[...] = jnp.concatenate([s0_ref[0, h].T for h in range(N_HEADS)], axis=1)

    lbs = lbnd_ref[...]
    e = jnp.exp(lbs - jnp.max(lbs, axis=0, keepdims=True))
    lb = e[0:1, :] / jnp.sum(e, axis=0, keepdims=True)
    gn = jnp.concatenate([gn_ref[...]] * N_HEADS, axis=1)

    ri = lax.broadcasted_iota(jnp.int32, (CHUNK, CHUNK), 0)
    ci = lax.broadcasted_iota(jnp.int32, (CHUNK, CHUNK), 1)
    tril = (ri >= ci).astype(F32)
    pr = lax.broadcasted_iota(jnp.int32, (2 * HEAD_W, 2 * HEAD_W), 0)
    pc = lax.broadcasted_iota(jnp.int32, (2 * HEAD_W, 2 * HEAD_W), 1)
    pair_ones = ((pr >> 7) == (pc >> 7)).astype(BF16)
    gr = lax.broadcasted_iota(jnp.int32, (CHUNK, CHUNK * SUB), 0)
    gc = lax.broadcasted_iota(jnp.int32, (CHUNK, CHUNK * SUB), 1)
    seg_sum = (((gc >> 3) == gr) & ((gc & (SUB - 1)) <= (gr & (SUB - 1)))).astype(BF16)
    n_off = (N_SUB - 1) * SUB
    n_key = SUB * (N_SUB * (N_SUB - 1) // 2)
    orow = lax.broadcasted_iota(jnp.int32, (n_off, n_key), 0) >> 3
    ocol = lax.broadcasted_iota(jnp.int32, (n_off, n_key), 1)
    ocol_seg = sum((ocol >= 4 * i * (i - 1)).astype(jnp.int32) for i in range(2, N_SUB))
    off_mask = orow == ocol_seg
    nt = (((1,), (1,)), ((), ()))
    tn = (((0,), (0,)), ((), ()))

    n_all = n_key + CHUNK
    frow = lax.broadcasted_iota(jnp.int32, (CHUNK, n_all), 0)
    fcol = lax.broadcasted_iota(jnp.int32, (CHUNK, n_all), 1)
    fseg = sum((fcol >= 4 * i * (i - 1)).astype(jnp.int32) for i in range(2, N_SUB))
    fsame = fcol - n_key
    all_mask = (((fcol < n_key) & (fseg + 1 == (frow >> 3)))
                | ((fcol >= n_key) & ((fsame >> 3) == (frow >> 3)) & ((fsame & (SUB - 1)) <= (frow & (SUB - 1)))))

    def chunk(c, carry, factored):
        r0 = pl.multiple_of(c * CHUNK, CHUNK)
        hq = hq_ref[0, pl.ds(r0, CHUNK), :]
        hf = hf_ref[0, pl.ds(r0, CHUNK), :]
        v = hi_ref[0, pl.ds(r0, CHUNK), :]
        hg = hg_ref[0, pl.ds(r0, CHUNK), :]

        f = lb + (1.0 - lb) * _sigmoid(hf)
        g = jnp.log(f) * LOG2E
        kk = 1.0 - f
        q = hq * _sigmoid(hq)
        b = jnp.dot(tril, g, precision=lax.Precision.HIGHEST, preferred_element_type=F32)
        v_bf = v.astype(BF16)

        st = st_ref[...]
        st_bf = st.astype(BF16)
        b_last = b[CHUNK - 1:CHUNK, :]
        q_dec = q * jnp.exp2(b)
        q_in = q_dec.astype(BF16)
        k_dec = (kk * jnp.exp2(b_last - b)).astype(BF16)
        o_inter = jnp.concatenate(
            [lax.dot_general(q_in[:, _head(h)], st_bf[:, _head(h)], nt, preferred_element_type=F32)
             for h in range(N_HEADS)], axis=1)
        upd = jnp.concatenate(
            [lax.dot_general(v_bf[:, _head(h)], k_dec[:, _head(h)], tn, preferred_element_type=F32)
             for h in range(N_HEADS)], axis=1)
        st_ref[...] = jnp.exp2(b_last) * st + upd

        qt, kh, vh = [], [], []
        for i in range(1, N_SUB):
            lo = i * SUB
            b_start = b[lo - 1:lo, :]
            qt.append(q[lo:lo + SUB] * jnp.exp2(b[lo:lo + SUB] - b_start))
            kh.append(kk[:lo] * jnp.exp2(b_start - b[:lo]))
            vh.append(v[:lo])

        if factored:
            b0 = jnp.concatenate(
                [jnp.zeros((SUB, HG_WIDTH), F32)]
                + [jnp.broadcast_to(b[i * SUB - 1:i * SUB, :], (SUB, HG_WIDTH)) for i in range(1, N_SUB)], axis=0)
            kd = kk * jnp.exp2(b0 - b)
            q_all = jnp.concatenate([q_dec[:SUB]] + qt, axis=0).astype(BF16)
            k_all = jnp.concatenate(kh + [kd], axis=0).astype(BF16)
            v_all = jnp.concatenate(vh + [v], axis=0).astype(BF16)
            a_all = [lax.dot_general(q_all[:, _head(h)], k_all[:, _head(h)], nt, preferred_element_type=F32)
                     for h in range(N_HEADS)]
            o_intra = jnp.concatenate(
                [jnp.dot(jnp.where(all_mask, a_all[h], 0.0).astype(BF16), v_all[:, _head(h)],
                         preferred_element_type=F32) for h in range(N_HEADS)], axis=1)
        else:
            for h in range(N_HEADS):
                q_ref[h] = q[:, _head(h)]
                b_ref[h] = b[:, _head(h)]
            qt = jnp.concatenate(qt, axis=0).astype(BF16)
            kh = jnp.concatenate(kh, axis=0).astype(BF16)
            vh = jnp.concatenate(vh, axis=0).astype(BF16)
            a_off = [lax.dot_general(qt[:, _head(h)], kh[:, _head(h)], nt, preferred_element_type=F32)
                     for h in range(N_HEADS)]

            a_rep = []
            for j in range(N_HEADS // 2):
                p_pair = []
                for h in (2 * j, 2 * j + 1):
                    kk_h = kk[:, _head(h)]
                    b_h = b[:, _head(h)]
                    rows = []
                    for r in range(CHUNK):
                        lo = r - r % SUB
                        q_row = q_ref[h, pl.ds(r, SUB, stride=0), :]
                        b_row = b_ref[h, pl.ds(r, SUB, stride=0), :]
                        rows.append((q_row * kk_h[lo:lo + SUB])
                                    * jnp.exp2(jnp.minimum(b_row - b_h[lo:lo + SUB], 0.0)))
                    p_pair.append(jnp.concatenate(rows, axis=0).astype(BF16))
                a_rep.append(jnp.dot(jnp.concatenate(p_pair, axis=1), pair_ones,
                                     preferred_element_type=F32))

            o_off = jnp.concatenate(
                [jnp.dot(jnp.where(off_mask, a_off[h], 0.0).astype(BF16), vh[:, _head(h)],
                         preferred_element_type=F32) for h in range(N_HEADS)], axis=1)
            o_diag = []
            for j in range(N_HEADS // 2):
                pair = slice(2 * HEAD_W * j, 2 * HEAD_W * (j + 1))
                v_rep = jnp.broadcast_to(v[:, pair].reshape(N_SUB, 1, SUB, 2 * HEAD_W),
                                         (N_SUB, SUB, SUB, 2 * HEAD_W))
                w = (a_rep[j].reshape(N_SUB, SUB, SUB, 2 * HEAD_W) * v_rep).reshape(CHUNK * SUB, 2 * HEAD_W)
                o_diag.append(jnp.dot(seg_sum, w.astype(BF16), preferred_element_type=F32))
            o_intra = jnp.concatenate(o_diag, axis=1) + jnp.concatenate(
                [jnp.zeros((SUB, HG_WIDTH), F32), o_off], axis=0)
        o = o_inter + o_intra

        y = jnp.concatenate(
            [o[:, _head(h)] * lax.rsqrt(jnp.mean(o[:, _head(h)] * o[:, _head(h)], axis=-1, keepdims=True) + EPS)
             for h in range(N_HEADS)], axis=1)
        y = y * gn * (hg * _sigmoid(hg))
        o_ref[pl.ds(r0, CHUNK), :] = y.astype(BF16)
        return carry

    factor_ok = jnp.min(lb) > MIN_FACTORED_LB

    @pl.when(factor_ok)
    def _():
        lax.fori_loop(0, n_chunks, functools.partial(chunk, factored=True), 0, unroll=min(n_chunks, 4))

    @pl.when(jnp.logical_not(factor_ok))
    def _():
        lax.fori_loop(0, n_chunks, functools.partial(chunk, factored=False), 0)

    @pl.when(r == pl.num_programs(1) - 1)
    def _():
        st = st_ref[...]
        for h in range(N_HEADS):
            sout_ref[0, h] = st[:, _head(h)].T


def _gla(gates, lower_bounds, hg_norm, s0, row0, n_seq, seq_len, rows):
    m = n_seq * seq_len
    nr = seq_len // rows
    blk0 = row0 // rows
    seg = lambda s: pl.BlockSpec((1, rows, HG_WIDTH), lambda b, r: (s, blk0 + b * nr + r, 0))
    state = pl.BlockSpec((1, N_HEADS, HEAD_W, HEAD_W), lambda b, r: (b, 0, 0, 0))
    return pl.pallas_call(
        functools.partial(_gla_kernel, n_chunks=rows // CHUNK),
        grid=(n_seq, nr),
        in_specs=[
            seg(0), seg(1), seg(2), seg(3),
            pl.BlockSpec((lower_bounds.shape[0], HG_WIDTH), lambda b, r: (0, 0)),
            pl.BlockSpec((1, HEAD_W), lambda b, r: (0, 0)),
            state,
        ],
        out_specs=[pl.BlockSpec((rows, HG_WIDTH), lambda b, r: (b * nr + r, 0)), state],
        out_shape=[
            jax.ShapeDtypeStruct((m, HG_WIDTH), BF16),
            jax.ShapeDtypeStruct((n_seq, N_HEADS, HEAD_W, HEAD_W), F32),
        ],
        scratch_shapes=[pltpu.VMEM((HEAD_W, HG_WIDTH), F32),
                        pltpu.VMEM((N_HEADS, CHUNK, HEAD_W), F32),
                        pltpu.VMEM((N_HEADS, CHUNK, HEAD_W), F32)],
        compiler_params=pltpu.CompilerParams(
            dimension_semantics=("arbitrary", "arbitrary"), vmem_limit_bytes=VMEM_LIMIT),
        name="hgrn2",
    )(gates, gates, gates, gates, lower_bounds, hg_norm, s0)


def _stacked_query(q):
    lane = lax.broadcasted_iota(jnp.int32, q.shape, 1)
    qbig = jnp.concatenate([jnp.where(lane < ATT_DH, q, 0.0), jnp.where(lane >= ATT_DH, q, 0.0)], axis=0)
    return qbig.T.astype(BF16)


def _lambda(lq1_ref, lk1_ref, lq2_ref, lk2_ref):
    s1 = jnp.sum(lq1_ref[...] * lk1_ref[...], axis=-1, keepdims=True)
    s2 = jnp.sum(lq2_ref[...] * lk2_ref[...], axis=-1, keepdims=True)
    return jnp.exp(s1) - jnp.exp(s2) + LAM_INIT


def _attn_finish(acc, l, lam, sub, n):
    o_both = (acc * (1.0 / l)).T
    o = o_both[:n] - lam * o_both[n:]
    return _rmsnorm_rows(o, sub) * (1.0 - LAM_INIT)


def _attn_prompt_kernel(q_ref, qn_ref, k_ref, vt_ref, lq1_ref, lk1_ref, lq2_ref, lk2_ref, sub_ref, o_ref,
                        qq_ref, s_ref, smax_ref, mask_ref, m_ref, acc_ref, *, t, n_grp):
    qi = pl.program_id(1)
    grp = range(n_grp)
    slot = qi % 2
    m_ref[...] = jnp.full(m_ref.shape, NEG_INF, F32)
    acc_ref[...] = jnp.zeros(acc_ref.shape, F32)
    ones_rows = jnp.ones((ONES_ROWS, t), BF16)

    def values_t(g, kv):
        k0 = pl.multiple_of(kv * t, t)
        return jnp.concatenate([vt_ref[_head(g), pl.ds(k0, t)], ones_rows], axis=0)

    def first_scores(g, qslot, cols):
        s = jnp.dot(k_ref[0:t, _head(g)], qq_ref[qslot, g, :, cols], preferred_element_type=F32)
        s_ref[g, 0, :, cols] = s
        smax_ref[g, 0, :, cols] = jnp.max(s, axis=0, keepdims=True)

    def stage(kv, cur, oth):
        k_next = pl.multiple_of((kv + 1) * t, t)
        vt = [values_t(g, kv) for g in grp]
        for c in range(0, 2 * t, STAGE_COLS):
            cols = slice(c, c + STAGE_COLS)
            for g in grp:
                s_next = jnp.dot(k_ref[pl.ds(k_next, t), _head(g)], qq_ref[slot, g, :, cols],
                                 preferred_element_type=F32)
                s_ref[g, oth, :, cols] = s_next
                smax_ref[g, oth, :, cols] = jnp.max(s_next, axis=0, keepdims=True)
                m_prev = m_ref[g, :, cols]
                m_new = jnp.maximum(m_prev, smax_ref[g, cur, :, cols])
                m_ref[g, :, cols] = m_new
                p = jnp.exp2(s_ref[g, cur, :, cols] - m_new).astype(BF16)
                acc_ref[g, :, cols] = (jnp.exp2(m_prev - m_new) * acc_ref[g, :, cols]
                                       + jnp.dot(vt[g], p, preferred_element_type=F32))

    @pl.when((pl.program_id(0) == 0) & (qi == 0))
    def _():
        kpos = lax.broadcasted_iota(jnp.int32, mask_ref.shape, 0)
        col = lax.broadcasted_iota(jnp.int32, mask_ref.shape, 1)
        qpos = jnp.where(col >= t, col - t, col)
        mask_ref[...] = jnp.where((kpos >> 6) <= (qpos >> 6), 0.0, NEG_INF)

    def finish(cur):
        for g in grp:
            qq_ref[1 - slot, g] = _stacked_query(qn_ref[:, _head(g)].astype(F32))
        vt = [values_t(g, qi) for g in grp]
        acc = [[] for _ in grp]
        for c in range(0, 2 * t, FINISH_COLS):
            cols = slice(c, c + FINISH_COLS)
            for g in grp:
                s = s_ref[g, cur, :, cols] + mask_ref[:, cols]
                m_prev = m_ref[g, :, cols]
                m_new = jnp.maximum(m_prev, jnp.max(s, axis=0, keepdims=True))
                p = jnp.exp2(s - m_new).astype(BF16)
                acc[g].append(jnp.exp2(m_prev - m_new) * acc_ref[g, :, cols]
                              + jnp.dot(vt[g], p, preferred_element_type=F32))
                first_scores(g, 1 - slot, cols)
        lam = _lambda(lq1_ref, lk1_ref, lq2_ref, lk2_ref)
        for g in grp:
            a = jnp.concatenate(acc[g], axis=1)
            o_ref[:, _head(g)] = _attn_finish(a[:HEAD_W], a[HEAD_W:HEAD_W + 1], lam, sub_ref[...],
                                              t).astype(BF16)

    @pl.when(qi == 0)
    def _():
        for g in grp:
            qq_ref[0, g] = _stacked_query(q_ref[:, _head(g)].astype(F32))
        for c in range(0, 2 * t, FINISH_COLS):
            for g in grp:
                first_scores(g, 0, slice(c, c + FINISH_COLS))

    def pair(j, carry):
        stage(2 * j, 0, 1)
        stage(2 * j + 1, 1, 0)
        return carry

    lax.fori_loop(0, qi // 2, pair, 0)

    @pl.when(qi % 2 == 1)
    def _():
        stage(qi - 1, 0, 1)
        finish(1)

    @pl.when(qi % 2 == 0)
    def _():
        finish(0)


def _attn_prompt(q_bf, k_bf, vt_bf, lams, subln, n, t, n_grp):
    w = n_grp * HEAD_W
    small = lambda shape: pl.BlockSpec(shape, lambda h, i: (0, 0))
    return pl.pallas_call(
        functools.partial(_attn_prompt_kernel, t=t, n_grp=n_grp),
        grid=(N_HEADS // n_grp, n // t),
        in_specs=[
            pl.BlockSpec((t, w), lambda h, i: (i, h)),
            pl.BlockSpec((t, w), lambda h, i: (jnp.minimum(i + 1, n // t - 1), h)),
            pl.BlockSpec((n, w), lambda h, i: (0, h)),
            pl.BlockSpec((w, n), lambda h, i: (h, 0)),
            small((1, ATT_DH)), small((1, ATT_DH)), small((1, ATT_DH)), small((1, ATT_DH)),
            small((1, HEAD_W)),
        ],
        out_specs=pl.BlockSpec((t, w), lambda h, i: (i, h)),
        out_shape=jax.ShapeDtypeStruct((n, ATT_WIDTH), BF16),
        scratch_shapes=[
            pltpu.VMEM((2, n_grp, HEAD_W, 2 * t), BF16),
            pltpu.VMEM((n_grp, 2, t, 2 * t + ROW_PAD), F32),
            pltpu.VMEM((n_grp, 2, 1, 2 * t), F32),
            pltpu.VMEM((t, 2 * t + ROW_PAD), F32),
            pltpu.VMEM((n_grp, 1, 2 * t), F32),
            pltpu.VMEM((n_grp, HEAD_W + ONES_ROWS, 2 * t + ROW_PAD), F32),
        ],
        compiler_params=pltpu.CompilerParams(
            dimension_semantics=("arbitrary", "arbitrary"), vmem_limit_bytes=VMEM_LIMIT),
        name="attn_prompt",
    )(q_bf, q_bf, k_bf, vt_bf, *lams, subln)


def _attn_sample_kernel(q_ref, kc_ref, vc_ref, kn_ref, vn_ref, lq1_ref, lk1_ref, lq2_ref, lk2_ref,
                        sub_ref, o_ref, qq_ref, m_ref, l_ref, acc_ref, *, n_q, tk):
    t = pl.program_id(1)
    pairs = range(N_HEADS // 2)
    w2 = 2 * HEAD_W

    @pl.when(t == 0)
    def _():
        zero = jnp.zeros((HEAD_W, 2 * n_q), BF16)
        for j in pairs:
            qa = _stacked_query(q_ref[:, _head(2 * j)].astype(F32))
            qb = _stacked_query(q_ref[:, _head(2 * j + 1)].astype(F32))
            qq_ref[j] = jnp.concatenate([jnp.concatenate([qa, zero], axis=1),
                                         jnp.concatenate([zero, qb], axis=1)], axis=0)
        m_ref[...] = jnp.full(m_ref.shape, NEG_INF, F32)
        l_ref[...] = jnp.zeros(l_ref.shape, F32)
        acc_ref[...] = jnp.zeros(acc_ref.shape, F32)

    def update(k, v):
        s = [jnp.dot(k[j], qq_ref[j], preferred_element_type=F32) for j in pairs]
        p, alpha = [], []
        for j in pairs:
            m_prev = m_ref[j]
            m_new = jnp.maximum(m_prev, jnp.max(s[j], axis=0, keepdims=True))
            m_ref[j] = m_new
            a = jnp.exp2(m_prev - m_new)
            pj = jnp.exp2(s[j] - m_new)
            l_ref[j] = a * l_ref[j] + jnp.sum(pj, axis=0, keepdims=True)
            p.append(pj.astype(BF16))
            alpha.append(a)
        pv = [lax.dot_general(v[j], p[j], (((0,), (0,)), ((), ())), preferred_element_type=F32)
              for j in pairs]
        for j in pairs:
            acc_ref[j] = alpha[j] * acc_ref[j] + pv[j]

    def cache_pair(ref, j):
        return jnp.concatenate([ref[0, pl.ds(2 * j, tk, stride=N_HEADS), :],
                                ref[0, pl.ds(2 * j + 1, tk, stride=N_HEADS), :]], axis=1).astype(BF16)

    update([cache_pair(kc_ref, j) for j in pairs], [cache_pair(vc_ref, j) for j in pairs])

    @pl.when(t == pl.num_programs(1) - 1)
    def _():
        update([kn_ref[:, w2 * j:w2 * (j + 1)] for j in pairs], [vn_ref[:, w2 * j:w2 * (j + 1)] for j in pairs])
        lam = _lambda(lq1_ref, lk1_ref, lq2_ref, lk2_ref)
        for j in pairs:
            acc = acc_ref[j]
            l = l_ref[j]
            for i, h in enumerate((2 * j, 2 * j + 1)):
                o_ref[:, _head(h)] = _attn_finish(acc[_head(i), _head(i)], l[:, _head(i)], lam, sub_ref[...],
                                                  n_q).astype(BF16)


def _attn_sample(q_bf, cache_k, cache_v, kn_bf, vn_bf, lams, subln, row0, n_q, tk):
    n_b = cache_k.shape[0]
    past = cache_k.shape[1] // N_HEADS
    assert past % CHUNK == 0 and n_q <= CHUNK and past % tk == 0 and row0 % n_q == 0
    blk0 = row0 // n_q
    small = lambda shape: pl.BlockSpec(shape, lambda b, t: (0, 0))
    rows = pl.BlockSpec((n_q, ATT_WIDTH), lambda b, t: (b, 0))
    rows_all = pl.BlockSpec((n_q, ATT_WIDTH), lambda b, t: (blk0 + b, 0))
    cache = pl.BlockSpec((1, tk * N_HEADS, HEAD_W), lambda b, t: (b, t, 0))
    return pl.pallas_call(
        functools.partial(_attn_sample_kernel, n_q=n_q, tk=tk),
        grid=(n_b, past // tk),
        in_specs=[
            rows_all, cache, cache, rows_all, rows,
            small((1, ATT_DH)), small((1, ATT_DH)), small((1, ATT_DH)), small((1, ATT_DH)),
            small((1, HEAD_W)),
        ],
        out_specs=rows,
        out_shape=jax.ShapeDtypeStruct((n_b * n_q, ATT_WIDTH), BF16),
        scratch_shapes=[
            pltpu.VMEM((N_HEADS // 2, 2 * HEAD_W, 4 * n_q), BF16),
            pltpu.VMEM((N_HEADS // 2, 1, 4 * n_q), F32),
            pltpu.VMEM((N_HEADS // 2, 1, 4 * n_q), F32),
            pltpu.VMEM((N_HEADS // 2, 2 * HEAD_W, 4 * n_q), F32),
        ],
        compiler_params=pltpu.CompilerParams(
            dimension_semantics=("arbitrary", "arbitrary"), vmem_limit_bytes=VMEM_LIMIT),
        name="attn_sample",
    )(q_bf, cache_k, cache_v, kn_bf, vn_bf, *lams, subln)


def _outproj_kernel(x_ref, a_ref, b_ref, w_ref, o_ref, wb_ref):
    @pl.when(pl.program_id(0) == 0)
    def _():
        wb_ref[...] = w_ref[...].astype(BF16)

    o_ref[...] = (x_ref[...]
                  + jnp.dot(a_ref[...], wb_ref[:HG_WIDTH, :], preferred_element_type=F32)
                  + jnp.dot(b_ref[...], wb_ref[HG_WIDTH:, :], preferred_element_type=F32))


def _outproj(x, mix_hg, mix_at, w, tm):
    m = x.shape[0]
    row = lambda i: (i, 0)
    return pl.pallas_call(
        _outproj_kernel,
        grid=(m // tm,),
        in_specs=[
            pl.BlockSpec((tm, D_MODEL), row),
            pl.BlockSpec((tm, HG_WIDTH), row),
            pl.BlockSpec((tm, ATT_WIDTH), row),
            pl.BlockSpec((HG_WIDTH + ATT_WIDTH, D_MODEL), lambda i: (0, 0), pipeline_mode=pl.Buffered(1)),
        ],
        out_specs=pl.BlockSpec((tm, D_MODEL), row),
        out_shape=jax.ShapeDtypeStruct((m, D_MODEL), F32),
        scratch_shapes=[pltpu.VMEM((HG_WIDTH + ATT_WIDTH, D_MODEL), BF16)],
        compiler_params=pltpu.CompilerParams(
            dimension_semantics=("arbitrary",), vmem_limit_bytes=VMEM_LIMIT),
        name="outproj",
    )(x, mix_hg, mix_at, w)


def _mlp_kernel(x_ref, gain_ref, wu_ref, wd_ref, gfin_ref, o_ref, h_ref):
    j = pl.program_id(1)

    @pl.when(j == 0)
    def _():
        x = x_ref[...]
        h_ref[...] = _rmsnorm_rows(x, gain_ref[...]).astype(BF16)
        o_ref[...] = x

    u = jnp.dot(h_ref[...], wu_ref[...].astype(BF16), preferred_element_type=F32)
    u = jnp.square(jnp.maximum(u, 0.0)).astype(BF16)
    o_ref[...] += jnp.dot(u, wd_ref[...].astype(BF16), preferred_element_type=F32)

    @pl.when(j == pl.num_programs(1) - 1)
    def _():
        o_ref[...] = _rmsnorm_rows(o_ref[...], gfin_ref[...])


def _mlp(x, gain, wu, wd, gfin, tm, tf):
    m = x.shape[0]
    return pl.pallas_call(
        _mlp_kernel,
        grid=(m // tm, D_FF // tf),
        in_specs=[
            pl.BlockSpec((tm, D_MODEL), lambda i, j: (i, 0)),
            pl.BlockSpec((1, D_MODEL), lambda i, j: (0, 0)),
            pl.BlockSpec((D_MODEL, tf), lambda i, j: (0, j)),
            pl.BlockSpec((tf, D_MODEL), lambda i, j: (j, 0)),
            pl.BlockSpec((1, D_MODEL), lambda i, j: (0, 0)),
        ],
        out_specs=pl.BlockSpec((tm, D_MODEL), lambda i, j: (i, 0)),
        out_shape=jax.ShapeDtypeStruct((m, D_MODEL), F32),
        scratch_shapes=[pltpu.VMEM((tm, D_MODEL), BF16)],
        compiler_params=pltpu.CompilerParams(
            dimension_semantics=("arbitrary", "arbitrary"), vmem_limit_bytes=VMEM_LIMIT),
        name="mlp",
    )(x, gain, wu, wd, gfin)


def kernel(x_prompt, x_sample, cache_k, cache_v, state_hgrn, norm_attn, w_in, lower_bounds, hg_norm,
           lambda_q1, lambda_k1, lambda_q2, lambda_k2, subln, w_out, norm_mlp, w_up, w_down, norm_final):
    depth = w_in.shape[0]
    assert depth == 1
    n_pb, n_p, _ = x_prompt.shape
    n_sb, n_s, _ = x_sample.shape
    assert n_pb == 1
    past = cache_k.shape[2]

    w_in_bf = w_in[0].astype(BF16)
    lams = (lambda_q1, lambda_k1, lambda_q2, lambda_k2)
    gfin = norm_final.reshape(1, D_MODEL)

    xp = x_prompt.reshape(n_p, D_MODEL)
    xs = x_sample.reshape(n_sb * n_s, D_MODEL)

    gates, q_bf, k_bf, vt_bf, vs_bf, kf_p, vf_p, kf_s, vf_s = _inproj(xp, xs, norm_attn, w_in_bf, tm=256)

    s0_p = jnp.zeros((1, N_HEADS, HEAD_W, HEAD_W), F32)
    ohg_p, st_p = _gla(gates, lower_bounds, hg_norm, s0_p, row0=0, n_seq=1, seq_len=n_p, rows=1024)
    ohg_s, st_s = _gla(gates, lower_bounds, hg_norm, state_hgrn[0], row0=n_p, n_seq=n_sb, seq_len=n_s, rows=n_s)

    oat_p = _attn_prompt(q_bf, k_bf, vt_bf, lams, subln, n=n_p, t=512, n_grp=2)
    ck = cache_k[0].reshape(n_sb, past * N_HEADS, HEAD_W)
    cv = cache_v[0].reshape(n_sb, past * N_HEADS, HEAD_W)
    oat_s = _attn_sample(q_bf, ck, cv, k_bf, vs_bf, lams, subln, row0=n_p, n_q=n_s, tk=1024)

    x1_p = _outproj(xp, ohg_p, oat_p, w_out[0], tm=512)
    x1_s = _outproj(xs, ohg_s, oat_s, w_out[0], tm=512)

    y_p = _mlp(x1_p, norm_mlp, w_up[0], w_down[0], gfin, tm=1024, tf=512)
    y_s = _mlp(x1_s, norm_mlp, w_up[0], w_down[0], gfin, tm=512, tf=512)

    return (
        y_p.reshape(n_pb, n_p, D_MODEL),
        y_s.reshape(n_sb, n_s, D_MODEL),
        kf_p.reshape(1, n_pb, n_p, N_HEADS, HEAD_W),
        vf_p.reshape(1, n_pb, n_p, N_HEADS, HEAD_W),
        st_p.reshape(1, n_pb, N_HEADS, HEAD_W, HEAD_W),
        kf_s.reshape(1, n_sb, n_s, N_HEADS, HEAD_W),
        vf_s.reshape(1, n_sb, n_s, N_HEADS, HEAD_W),
        st_s.reshape(1, n_sb, N_HEADS, HEAD_W, HEAD_W),
    )
```

```python
import functools
import math

import jax
import jax.numpy as jnp
from jax import lax
from jax.experimental import pallas as pl
from jax.experimental.pallas import tpu as pltpu

F32 = jnp.float32
BF16 = jnp.bfloat16

D_MODEL = 2048
HG_WIDTH = 1024
ATT_WIDTH = 1024
N_HEADS = 8
HEAD_W = 128
ATT_DH = 64
CHUNK = 64
SUB = 8
N_SUB = CHUNK // SUB
MIN_FACTORED_LB = 2.0 ** (-100.0 / SUB)
MIN_CHUNK_FACTORED_LB = 2.0 ** (-100.0 / CHUNK)
D_FF = 4 * D_MODEL
N_SEG = 7
EPS = 1e-6
NEG_INF = -1e30
LOG2E = 1.4426950408889634
Q_SCALE = ATT_DH ** -0.5 * LOG2E
LAM_INIT = 0.8 - 0.6 * math.exp(-0.3 * 0)
ONES_ROWS = 16
FINISH_COLS = 256
STAGE_COLS = 256

VMEM_LIMIT = 56 * 1024 * 1024


def _rmsnorm_rows(x, gain):
    return x * lax.rsqrt(jnp.mean(x * x, axis=-1, keepdims=True) + EPS) * gain


def _sigmoid(x):
    return 0.5 * jnp.tanh(0.5 * x) + 0.5


def _head(h):
    return slice(h * HEAD_W, (h + 1) * HEAD_W)


def _inproj_kernel(xp_ref, xs_ref, gain_ref, w_ref, gates_ref, q_ref, kb_ref, vt_ref, vbs_ref,
                   kfp_ref, vfp_ref, kfs_ref, vfs_ref, *, n_p):
    def project(x_ref, kf_ref, vf_ref, vb_ref):
        tm = x_ref.shape[0]
        h = _rmsnorm_rows(x_ref[...], gain_ref[...]).astype(BF16)

        def segment(s):
            return jnp.dot(h, w_ref[:, s * HG_WIDTH:(s + 1) * HG_WIDTH], preferred_element_type=F32)

        k = segment(5)
        v = segment(6)
        kb_ref[...] = k.astype(BF16)
        vt_ref[...] = v.T.astype(BF16)
        if vb_ref is not None:
            vb_ref[...] = v.astype(BF16)
        for hd in range(N_HEADS):
            kf_ref[pl.ds(hd, tm, stride=N_HEADS), :] = k[:, _head(hd)]
            vf_ref[pl.ds(hd, tm, stride=N_HEADS), :] = v[:, _head(hd)]
        q_ref[...] = (segment(4) * Q_SCALE).astype(BF16)
        for s in range(4):
            gates_ref[s] = segment(s)

    is_prompt = pl.program_id(0) < n_p

    @pl.when(is_prompt)
    def _():
        project(xp_ref, kfp_ref, vfp_ref, None)

    @pl.when(jnp.logical_not(is_prompt))
    def _():
        project(xs_ref, kfs_ref, vfs_ref, vbs_ref)


def _inproj(xp, xs, gain, w_bf, tm):
    m_p, m_s = xp.shape[0], xs.shape[0]
    n_p, n_s = m_p // tm, m_s // tm
    m = m_p + m_s
    seg = HG_WIDTH
    row = lambda i: (i, 0)
    p_row = lambda i: (jnp.minimum(i, n_p - 1), 0)
    s_row = lambda i: (jnp.maximum(i - n_p, 0), 0)
    once = dict(pipeline_mode=pl.Buffered(1))
    return pl.pallas_call(
        functools.partial(_inproj_kernel, n_p=n_p),
        grid=(n_p + n_s,),
        in_specs=[
            pl.BlockSpec((tm, D_MODEL), p_row),
            pl.BlockSpec((tm, D_MODEL), s_row, **once),
            pl.BlockSpec((1, D_MODEL), lambda i: (0, 0)),
            pl.BlockSpec((D_MODEL, N_SEG * seg), lambda i: (0, 0), **once),
        ],
        out_specs=[
            pl.BlockSpec((4, tm, seg), lambda i: (0, i, 0)),
            pl.BlockSpec((tm, seg), row),
            pl.BlockSpec((tm, seg), row),
            pl.BlockSpec((seg, tm), lambda i: (0, i)),
            pl.BlockSpec((tm, seg), s_row, **once),
            pl.BlockSpec((tm * N_HEADS, HEAD_W), p_row),
            pl.BlockSpec((tm * N_HEADS, HEAD_W), p_row),
            pl.BlockSpec((tm * N_HEADS, HEAD_W), s_row, **once),
            pl.BlockSpec((tm * N_HEADS, HEAD_W), s_row, **once),
        ],
        out_shape=[
            jax.ShapeDtypeStruct((4, m, seg), F32),
            jax.ShapeDtypeStruct((m, seg), BF16),
            jax.ShapeDtypeStruct((m, seg), BF16),
            jax.ShapeDtypeStruct((seg, m), BF16),
            jax.ShapeDtypeStruct((m_s, seg), BF16),
            jax.ShapeDtypeStruct((m_p * N_HEADS, HEAD_W), F32),
            jax.ShapeDtypeStruct((m_p * N_HEADS, HEAD_W), F32),
            jax.ShapeDtypeStruct((m_s * N_HEADS, HEAD_W), F32),
            jax.ShapeDtypeStruct((m_s * N_HEADS, HEAD_W), F32),
        ],
        compiler_params=pltpu.CompilerParams(
            dimension_semantics=("arbitrary",), vmem_limit_bytes=VMEM_LIMIT),
        name="inproj",
    )(xp, xs, gain, w_bf)


def _gla_kernel(hq_ref, hf_ref, hi_ref, hg_ref, lbnd_ref, gn_ref, s0_ref, o_ref, sout_ref, st_ref,
                q_ref, b_ref, *, n_chunks):
    r = pl.program_id(1)

    @pl.when(r == 0)
    def _():
        st_ref[...] = jnp.concatenate([s0_ref[0, h].T for h in range(N_HEADS)], axis=1)

    lbs = lbnd_ref[...]
    e = jnp.exp(lbs - jnp.max(lbs, axis=0, keepdims=True))
    lb = e[0:1, :] / jnp.sum(e, axis=0, keepdims=True)
    gn = jnp.concatenate([gn_ref[...]] * N_HEADS, axis=1)

    ri = lax.broadcasted_iota(jnp.int32, (CHUNK, CHUNK), 0)
    ci = lax.broadcasted_iota(jnp.int32, (CHUNK, CHUNK), 1)
    tril = (ri >= ci).astype(BF16)
    pr = lax.broadcasted_iota(jnp.int32, (2 * HEAD_W, 2 * HEAD_W), 0)
    pc = lax.broadcasted_iota(jnp.int32, (2 * HEAD_W, 2 * HEAD_W), 1)
    pair_ones = ((pr >> 7) == (pc >> 7)).astype(BF16)
    gr = lax.broadcasted_iota(jnp.int32, (CHUNK, CHUNK * SUB), 0)
    gc = lax.broadcasted_iota(jnp.int32, (CHUNK, CHUNK * SUB), 1)
    seg_sum = (((gc >> 3) == gr) & ((gc & (SUB - 1)) <= (gr & (SUB - 1)))).astype(BF16)
    n_off = (N_SUB - 1) * SUB
    n_key = SUB * (N_SUB * (N_SUB - 1) // 2)
    orow = lax.broadcasted_iota(jnp.int32, (n_off, n_key), 0) >> 3
    ocol = lax.broadcasted_iota(jnp.int32, (n_off, n_key), 1)
    ocol_seg = sum((ocol >= 4 * i * (i - 1)).astype(jnp.int32) for i in range(2, N_SUB))
    off_mask = orow == ocol_seg
    nt = (((1,), (1,)), ((), ()))
    tn = (((0,), (0,)), ((), ()))

    n_all = n_key + CHUNK
    frow = lax.broadcasted_iota(jnp.int32, (CHUNK, n_all), 0)
    fcol = lax.broadcasted_iota(jnp.int32, (CHUNK, n_all), 1)
    fseg = sum((fcol >= 4 * i * (i - 1)).astype(jnp.int32) for i in range(2, N_SUB))
    fsame = fcol - n_key
    all_mask = (((fcol < n_key) & (fseg + 1 == (frow >> 3)))
                | ((fcol >= n_key) & ((fsame >> 3) == (frow >> 3)) & ((fsame & (SUB - 1)) <= (frow & (SUB - 1)))))

    causal = ri >= ci

    def chunk(c, carry, mode):
        r0 = pl.multiple_of(c * CHUNK, CHUNK)
        hq = hq_ref[0, pl.ds(r0, CHUNK), :]
        hf = hf_ref[0, pl.ds(r0, CHUNK), :]
        v = hi_ref[0, pl.ds(r0, CHUNK), :]
        hg = hg_ref[0, pl.ds(r0, CHUNK), :]

        f = lb + (1.0 - lb) * _sigmoid(hf)
        g = jnp.log(f) * LOG2E
        kk = 1.0 - f
        q = hq * _sigmoid(hq)
        g_hi = g.astype(BF16)
        g_rest = g - g_hi.astype(F32)
        g_mid = g_rest.astype(BF16)
        g_lo = (g_rest - g_mid.astype(F32)).astype(BF16)
        b3 = jnp.dot(tril, jnp.concatenate([g_hi, g_mid, g_lo], axis=1), preferred_element_type=F32)
        b = b3[:, :HG_WIDTH] + b3[:, HG_WIDTH:2 * HG_WIDTH] + b3[:, 2 * HG_WIDTH:]
        v_bf = v.astype(BF16)

        st = st_ref[...]
        st_bf = st.astype(BF16)
        b_last = b[CHUNK - 1:CHUNK, :]
        q_dec = q * jnp.exp2(b)
        q_in = q_dec.astype(BF16)
        k_dec = (kk * jnp.exp2(b_last - b)).astype(BF16)
        o_inter = jnp.concatenate(
            [lax.dot_general(q_in[:, _head(h)], st_bf[:, _head(h)], nt, preferred_element_type=F32)
             for h in range(N_HEADS)], axis=1)
        upd = jnp.concatenate(
            [lax.dot_general(v_bf[:, _head(h)], k_dec[:, _head(h)], tn, preferred_element_type=F32)
             for h in range(N_HEADS)], axis=1)
        st_ref[...] = jnp.exp2(b_last) * st + upd

        qt, kh, vh = [], [], []
        for i in range(1, N_SUB) if mode != "chunk" else ():
            lo = i * SUB
            b_start = b[lo - 1:lo, :]
            qt.append(q[lo:lo + SUB] * jnp.exp2(b[lo:lo + SUB] - b_start))
            kh.append(kk[:lo] * jnp.exp2(b_start - b[:lo]))
            vh.append(v[:lo])

        if mode == "chunk":
            k_inv = (kk * jnp.exp2(-b)).astype(BF16)
            a_all = [lax.dot_general(q_in[:, _head(h)], k_inv[:, _head(h)], nt, preferred_element_type=F32)
                     for h in range(N_HEADS)]
            o_intra = jnp.concatenate(
                [jnp.dot(jnp.where(causal, a_all[h], 0.0).astype(BF16), v_bf[:, _head(h)],
                         preferred_element_type=F32) for h in range(N_HEADS)], axis=1)
        elif mode == "sub_block":
            b0 = jnp.concatenate(
                [jnp.zeros((SUB, HG_WIDTH), F32)]
                + [jnp.broadcast_to(b[i * SUB - 1:i * SUB, :], (SUB, HG_WIDTH)) for i in range(1, N_SUB)], axis=0)
            kd = kk * jnp.exp2(b0 - b)
            q_all = jnp.concatenate([q_dec[:SUB]] + qt, axis=0).astype(BF16)
            k_all = jnp.concatenate(kh + [kd], axis=0).astype(BF16)
            v_all = jnp.concatenate(vh + [v], axis=0).astype(BF16)
            a_all = [lax.dot_general(q_all[:, _head(h)], k_all[:, _head(h)], nt, preferred_element_type=F32)
                     for h in range(N_HEADS)]
            o_intra = jnp.concatenate(
                [jnp.dot(jnp.where(all_mask, a_all[h], 0.0).astype(BF16), v_all[:, _head(h)],
                         preferred_element_type=F32) for h in range(N_HEADS)], axis=1)
        else:
            for h in range(N_HEADS):
                q_ref[h] = q[:, _head(h)]
                b_ref[h] = b[:, _head(h)]
            qt = jnp.concatenate(qt, axis=0).astype(BF16)
            kh = jnp.concatenate(kh, axis=0).astype(BF16)
            vh = jnp.concatenate(vh, axis=0).astype(BF16)
            a_off = [lax.dot_general(qt[:, _head(h)], kh[:, _head(h)], nt, preferred_element_type=F32)
                     for h in range(N_HEADS)]

            a_rep = []
            for j in range(N_HEADS // 2):
                p_pair = []
                for h in (2 * j, 2 * j + 1):
                    kk_h = kk[:, _head(h)]
                    b_h = b[:, _head(h)]
                    rows = []
                    for r in range(CHUNK):
                        lo = r - r % SUB
                        q_row = q_ref[h, pl.ds(r, SUB, stride=0), :]
                        b_row = b_ref[h, pl.ds(r, SUB, stride=0), :]
                        rows.append((q_row * kk_h[lo:lo + SUB])
                                    * jnp.exp2(jnp.minimum(b_row - b_h[lo:lo + SUB], 0.0)))
                    p_pair.append(jnp.concatenate(rows, axis=0).astype(BF16))
                a_rep.append(jnp.dot(jnp.concatenate(p_pair, axis=1), pair_ones,
                                     preferred_element_type=F32))

            o_off = jnp.concatenate(
                [jnp.dot(jnp.where(off_mask, a_off[h], 0.0).astype(BF16), vh[:, _head(h)],
                         preferred_element_type=F32) for h in range(N_HEADS)], axis=1)
            o_diag = []
            for j in range(N_HEADS // 2):
                pair = slice(2 * HEAD_W * j, 2 * HEAD_W * (j + 1))
                v_rep = jnp.broadcast_to(v[:, pair].reshape(N_SUB, 1, SUB, 2 * HEAD_W),
                                         (N_SUB, SUB, SUB, 2 * HEAD_W))
                w = (a_rep[j].reshape(N_SUB, SUB, SUB, 2 * HEAD_W) * v_rep).reshape(CHUNK * SUB, 2 * HEAD_W)
                o_diag.append(jnp.dot(seg_sum, w.astype(BF16), preferred_element_type=F32))
            o_intra = jnp.concatenate(o_diag, axis=1) + jnp.concatenate(
                [jnp.zeros((SUB, HG_WIDTH), F32), o_off], axis=0)
        o = o_inter + o_intra

        y = jnp.concatenate(
            [o[:, _head(h)] * lax.rsqrt(jnp.mean(o[:, _head(h)] * o[:, _head(h)], axis=-1, keepdims=True) + EPS)
             for h in range(N_HEADS)], axis=1)
        y = y * gn * (hg * _sigmoid(hg))
        o_ref[pl.ds(r0, CHUNK), :] = y.astype(BF16)
        return carry

    lb_min = jnp.min(lb)
    unroll = min(n_chunks, 4)

    @pl.when(lb_min > MIN_CHUNK_FACTORED_LB)
    def _():
        lax.fori_loop(0, n_chunks, functools.partial(chunk, mode="chunk"), 0, unroll=unroll)

    @pl.when((lb_min > MIN_FACTORED_LB) & (lb_min <= MIN_CHUNK_FACTORED_LB))
    def _():
        lax.fori_loop(0, n_chunks, functools.partial(chunk, mode="sub_block"), 0, unroll=unroll)

    @pl.when(lb_min <= MIN_FACTORED_LB)
    def _():
        lax.fori_loop(0, n_chunks, functools.partial(chunk, mode="pairwise"), 0)

    @pl.when(r == pl.num_programs(1) - 1)
    def _():
        st = st_ref[...]
        for h in range(N_HEADS):
            sout_ref[0, h] = st[:, _head(h)].T


def _gla(gates, lower_bounds, hg_norm, s0, row0, n_seq, seq_len, rows):
    m = n_seq * seq_len
    nr = seq_len // rows
    blk0 = row0 // rows
    seg = lambda s: pl.BlockSpec((1, rows, HG_WIDTH), lambda b, r: (s, blk0 + b * nr + r, 0))
    state = pl.BlockSpec((1, N_HEADS, HEAD_W, HEAD_W), lambda b, r: (b, 0, 0, 0))
    return pl.pallas_call(
        functools.partial(_gla_kernel, n_chunks=rows // CHUNK),
        grid=(n_seq, nr),
        in_specs=[
            seg(0), seg(1), seg(2), seg(3),
            pl.BlockSpec((lower_bounds.shape[0], HG_WIDTH), lambda b, r: (0, 0)),
            pl.BlockSpec((1, HEAD_W), lambda b, r: (0, 0)),
            state,
        ],
        out_specs=[pl.BlockSpec((rows, HG_WIDTH), lambda b, r: (b * nr + r, 0)), state],
        out_shape=[
            jax.ShapeDtypeStruct((m, HG_WIDTH), BF16),
            jax.ShapeDtypeStruct((n_seq, N_HEADS, HEAD_W, HEAD_W), F32),
        ],
        scratch_shapes=[pltpu.VMEM((HEAD_W, HG_WIDTH), F32),
                        pltpu.VMEM((N_HEADS, CHUNK, HEAD_W), F32),
                        pltpu.VMEM((N_HEADS, CHUNK, HEAD_W), F32)],
        compiler_params=pltpu.CompilerParams(
            dimension_semantics=("arbitrary", "arbitrary"), vmem_limit_bytes=VMEM_LIMIT),
        name="hgrn2",
    )(gates, gates, gates, gates, lower_bounds, hg_norm, s0)


def _stacked_query(q):
    lane = lax.broadcasted_iota(jnp.int32, q.shape, 1)
    qbig = jnp.concatenate([jnp.where(lane < ATT_DH, q, 0.0), jnp.where(lane >= ATT_DH, q, 0.0)], axis=0)
    return qbig.T.astype(BF16)


def _lambda(lq1_ref, lk1_ref, lq2_ref, lk2_ref):
    s1 = jnp.sum(lq1_ref[...] * lk1_ref[...], axis=-1, keepdims=True)
    s2 = jnp.sum(lq2_ref[...] * lk2_ref[...], axis=-1, keepdims=True)
    return jnp.exp(s1) - jnp.exp(s2) + LAM_INIT


def _attn_finish(acc, l, lam, sub, n):
    o_both = (acc * (1.0 / l)).T
    o = o_both[:n] - lam * o_both[n:]
    return _rmsnorm_rows(o, sub) * (1.0 - LAM_INIT)


def _attn_prompt_kernel(q_ref, qn_ref, k_ref, vt_ref, lq1_ref, lk1_ref, lq2_ref, lk2_ref, sub_ref, o_ref,
                        qq_ref, s_ref, smax_ref, mask_ref, m_ref, acc_ref, *, t, n_grp):
    qi = pl.program_id(1)
    grp = range(n_grp)
    slot = qi % 2
    m_ref[...] = jnp.full(m_ref.shape, NEG_INF, F32)
    acc_ref[...] = jnp.zeros(acc_ref.shape, F32)
    ones_rows = jnp.ones((ONES_ROWS, t), BF16)

    def values_t(g, kv):
        k0 = pl.multiple_of(kv * t, t)
        return jnp.concatenate([vt_ref[_head(g), pl.ds(k0, t)], ones_rows], axis=0)

    def first_scores(g, qslot, cols):
        s = jnp.dot(k_ref[0:t, _head(g)], qq_ref[qslot, g, :, cols], preferred_element_type=F32)
        s_ref[g, 0, :, cols] = s
        smax_ref[g, 0, :, cols] = jnp.max(s, axis=0, keepdims=True)

    def stage(kv, cur, oth):
        k_next = pl.multiple_of((kv + 1) * t, t)
        vt = [values_t(g, kv) for g in grp]
        for c in range(0, 2 * t, STAGE_COLS):
            cols = slice(c, c + STAGE_COLS)
            for g in grp:
                s_next = jnp.dot(k_ref[pl.ds(k_next, t), _head(g)], qq_ref[slot, g, :, cols],
                                 preferred_element_type=F32)
                s_ref[g, oth, :, cols] = s_next
                smax_ref[g, oth, :, cols] = jnp.max(s_next, axis=0, keepdims=True)
                m_prev = m_ref[g, :, cols]
                m_new = jnp.maximum(m_prev, smax_ref[g, cur, :, cols])
                m_ref[g, :, cols] = m_new
                p = jnp.exp2(s_ref[g, cur, :, cols] - m_new).astype(BF16)
                acc_ref[g, :, cols] = (jnp.exp2(m_prev - m_new) * acc_ref[g, :, cols]
                                       + jnp.dot(vt[g], p, preferred_element_type=F32))

    @pl.when((pl.program_id(0) == 0) & (qi == 0))
    def _():
        kpos = lax.broadcasted_iota(jnp.int32, mask_ref.shape, 0)
        col = lax.broadcasted_iota(jnp.int32, mask_ref.shape, 1)
        qpos = jnp.where(col >= t, col - t, col)
        mask_ref[...] = jnp.where((kpos >> 6) <= (qpos >> 6), 0.0, NEG_INF)

    def finish(cur):
        for g in grp:
            qq_ref[1 - slot, g] = _stacked_query(qn_ref[:, _head(g)].astype(F32))
        vt = [values_t(g, qi) for g in grp]
        acc = [[] for _ in grp]
        for c in range(0, 2 * t, FINISH_COLS):
            cols = slice(c, c + FINISH_COLS)
            for g in grp:
                s = s_ref[g, cur, :, cols] + mask_ref[:, cols]
                m_prev = m_ref[g, :, cols]
                m_new = jnp.maximum(m_prev, jnp.max(s, axis=0, keepdims=True))
                p = jnp.exp2(s - m_new).astype(BF16)
                acc[g].append(jnp.exp2(m_prev - m_new) * acc_ref[g, :, cols]
                              + jnp.dot(vt[g], p, preferred_element_type=F32))
                first_scores(g, 1 - slot, cols)
        lam = _lambda(lq1_ref, lk1_ref, lq2_ref, lk2_ref)
        for g in grp:
            a = jnp.concatenate(acc[g], axis=1)
            o_ref[:, _head(g)] = _attn_finish(a[:HEAD_W], a[HEAD_W:HEAD_W + 1], lam, sub_ref[...],
                                              t).astype(BF16)

    @pl.when(qi == 0)
    def _():
        for g in grp:
            qq_ref[0, g] = _stacked_query(q_ref[:, _head(g)].astype(F32))
        for c in range(0, 2 * t, FINISH_COLS):
            for g in grp:
                first_scores(g, 0, slice(c, c + FINISH_COLS))

    def pair(j, carry):
        stage(2 * j, 0, 1)
        stage(2 * j + 1, 1, 0)
        return carry

    lax.fori_loop(0, qi // 2, pair, 0)

    @pl.when(qi % 2 == 1)
    def _():
        stage(qi - 1, 0, 1)
        finish(1)

    @pl.when(qi % 2 == 0)
    def _():
        finish(0)


def _attn_prompt(q_bf, k_bf, vt_bf, lams, subln, n, t, n_grp):
    w = n_grp * HEAD_W
    small = lambda shape: pl.BlockSpec(shape, lambda h, i: (0, 0))
    return pl.pallas_call(
        functools.partial(_attn_prompt_kernel, t=t, n_grp=n_grp),
        grid=(N_HEADS // n_grp, n // t),
        in_specs=[
            pl.BlockSpec((t, w), lambda h, i: (i, h)),
            pl.BlockSpec((t, w), lambda h, i: (jnp.minimum(i + 1, n // t - 1), h)),
            pl.BlockSpec((n, w), lambda h, i: (0, h)),
            pl.BlockSpec((w, n), lambda h, i: (h, 0)),
            small((1, ATT_DH)), small((1, ATT_DH)), small((1, ATT_DH)), small((1, ATT_DH)),
            small((1, HEAD_W)),
        ],
        out_specs=pl.BlockSpec((t, w), lambda h, i: (i, h)),
        out_shape=jax.ShapeDtypeStruct((n, ATT_WIDTH), BF16),
        scratch_shapes=[
            pltpu.VMEM((2, n_grp, HEAD_W, 2 * t), BF16),
            pltpu.VMEM((n_grp, 2, t, 2 * t), F32),
            pltpu.VMEM((n_grp, 2, 1, 2 * t), F32),
            pltpu.VMEM((t, 2 * t), F32),
            pltpu.VMEM((n_grp, 1, 2 * t), F32),
            pltpu.VMEM((n_grp, HEAD_W + ONES_ROWS, 2 * t), F32),
        ],
        compiler_params=pltpu.CompilerParams(
            dimension_semantics=("arbitrary", "arbitrary"), vmem_limit_bytes=VMEM_LIMIT),
        name="attn_prompt",
    )(q_bf, q_bf, k_bf, vt_bf, *lams, subln)


def _attn_sample_kernel(q_ref, kc_ref, vc_ref, kn_ref, vn_ref, lq1_ref, lk1_ref, lq2_ref, lk2_ref,
                        sub_ref, o_ref, qq_ref, m_ref, l_ref, acc_ref, *, n_q, tk):
    t = pl.program_id(1)
    pairs = range(N_HEADS // 2)
    w2 = 2 * HEAD_W

    @pl.when(t == 0)
    def _():
        zero = jnp.zeros((HEAD_W, 2 * n_q), BF16)
        for j in pairs:
            qa = _stacked_query(q_ref[:, _head(2 * j)].astype(F32))
            qb = _stacked_query(q_ref[:, _head(2 * j + 1)].astype(F32))
            qq_ref[j] = jnp.concatenate([jnp.concatenate([qa, zero], axis=1),
                                         jnp.concatenate([zero, qb], axis=1)], axis=0)
        m_ref[...] = jnp.full(m_ref.shape, NEG_INF, F32)
        l_ref[...] = jnp.zeros(l_ref.shape, F32)
        acc_ref[...] = jnp.zeros(acc_ref.shape, F32)

    def update(k, v):
        s = [jnp.dot(k[j], qq_ref[j], preferred_element_type=F32) for j in pairs]
        p, alpha = [], []
        for j in pairs:
            m_prev = m_ref[j]
            m_new = jnp.maximum(m_prev, jnp.max(s[j], axis=0, keepdims=True))
            m_ref[j] = m_new
            a = jnp.exp2(m_prev - m_new)
            pj = jnp.exp2(s[j] - m_new)
            l_ref[j] = a * l_ref[j] + jnp.sum(pj, axis=0, keepdims=True)
            p.append(pj.astype(BF16))
            alpha.append(a)
        pv = [lax.dot_general(v[j], p[j], (((0,), (0,)), ((), ())), preferred_element_type=F32)
              for j in pairs]
        for j in pairs:
            acc_ref[j] = alpha[j] * acc_ref[j] + pv[j]

    def cache_pair(ref, j):
        return jnp.concatenate([ref[0, pl.ds(2 * j, tk, stride=N_HEADS), :],
                                ref[0, pl.ds(2 * j + 1, tk, stride=N_HEADS), :]], axis=1).astype(BF16)

    update([cache_pair(kc_ref, j) for j in pairs], [cache_pair(vc_ref, j) for j in pairs])

    @pl.when(t == pl.num_programs(1) - 1)
    def _():
        update([kn_ref[:, w2 * j:w2 * (j + 1)] for j in pairs], [vn_ref[:, w2 * j:w2 * (j + 1)] for j in pairs])
        lam = _lambda(lq1_ref, lk1_ref, lq2_ref, lk2_ref)
        for j in pairs:
            acc = acc_ref[j]
            l = l_ref[j]
            for i, h in enumerate((2 * j, 2 * j + 1)):
                o_ref[:, _head(h)] = _attn_finish(acc[_head(i), _head(i)], l[:, _head(i)], lam, sub_ref[...],
                                                  n_q).astype(BF16)


def _attn_sample(q_bf, cache_k, cache_v, kn_bf, vn_bf, lams, subln, row0, n_q, tk):
    n_b = cache_k.shape[0]
    past = cache_k.shape[1] // N_HEADS
    assert past % CHUNK == 0 and n_q <= CHUNK and past % tk == 0 and row0 % n_q == 0
    blk0 = row0 // n_q
    small = lambda shape: pl.BlockSpec(shape, lambda b, t: (0, 0))
    rows = pl.BlockSpec((n_q, ATT_WIDTH), lambda b, t: (b, 0))
    rows_all = pl.BlockSpec((n_q, ATT_WIDTH), lambda b, t: (blk0 + b, 0))
    cache = pl.BlockSpec((1, tk * N_HEADS, HEAD_W), lambda b, t: (b, t, 0))
    return pl.pallas_call(
        functools.partial(_attn_sample_kernel, n_q=n_q, tk=tk),
        grid=(n_b, past // tk),
        in_specs=[
            rows_all, cache, cache, rows_all, rows,
            small((1, ATT_DH)), small((1, ATT_DH)), small((1, ATT_DH)), small((1, ATT_DH)),
            small((1, HEAD_W)),
        ],
        out_specs=rows,
        out_shape=jax.ShapeDtypeStruct((n_b * n_q, ATT_WIDTH), BF16),
        scratch_shapes=[
            pltpu.VMEM((N_HEADS // 2, 2 * HEAD_W, 4 * n_q), BF16),
            pltpu.VMEM((N_HEADS // 2, 1, 4 * n_q), F32),
            pltpu.VMEM((N_HEADS // 2, 1, 4 * n_q), F32),
            pltpu.VMEM((N_HEADS // 2, 2 * HEAD_W, 4 * n_q), F32),
        ],
        compiler_params=pltpu.CompilerParams(
            dimension_semantics=("arbitrary", "arbitrary"), vmem_limit_bytes=VMEM_LIMIT),
        name="attn_sample",
    )(q_bf, cache_k, cache_v, kn_bf, vn_bf, *lams, subln)


def _outproj_kernel(x_ref, a_ref, b_ref, w_ref, o_ref, wb_ref):
    @pl.when(pl.program_id(0) == 0)
    def _():
        wb_ref[...] = w_ref[...].astype(BF16)

    o_ref[...] = (x_ref[...]
                  + jnp.dot(a_ref[...], wb_ref[:HG_WIDTH, :], preferred_element_type=F32)
                  + jnp.dot(b_ref[...], wb_ref[HG_WIDTH:, :], preferred_element_type=F32))


def _outproj(x, mix_hg, mix_at, w, tm):
    m = x.shape[0]
    row = lambda i: (i, 0)
    return pl.pallas_call(
        _outproj_kernel,
        grid=(m // tm,),
        in_specs=[
            pl.BlockSpec((tm, D_MODEL), row),
            pl.BlockSpec((tm, HG_WIDTH), row),
            pl.BlockSpec((tm, ATT_WIDTH), row),
            pl.BlockSpec((HG_WIDTH + ATT_WIDTH, D_MODEL), lambda i: (0, 0), pipeline_mode=pl.Buffered(1)),
        ],
        out_specs=pl.BlockSpec((tm, D_MODEL), row),
        out_shape=jax.ShapeDtypeStruct((m, D_MODEL), F32),
        scratch_shapes=[pltpu.VMEM((HG_WIDTH + ATT_WIDTH, D_MODEL), BF16)],
        compiler_params=pltpu.CompilerParams(
            dimension_semantics=("arbitrary",), vmem_limit_bytes=VMEM_LIMIT),
        name="outproj",
    )(x, mix_hg, mix_at, w)


def _mlp_kernel(x_ref, gain_ref, wu_ref, wd_ref, gfin_ref, o_ref, h_ref):
    j = pl.program_id(1)

    @pl.when(j == 0)
    def _():
        x = x_ref[...]
        h_ref[...] = _rmsnorm_rows(x, gain_ref[...]).astype(BF16)
        o_ref[...] = x

    u = jnp.dot(h_ref[...], wu_ref[...].astype(BF16), preferred_element_type=F32)
    u = jnp.square(jnp.maximum(u, 0.0)).astype(BF16)
    o_ref[...] += jnp.dot(u, wd_ref[...].astype(BF16), preferred_element_type=F32)

    @pl.when(j == pl.num_programs(1) - 1)
    def _():
        o_ref[...] = _rmsnorm_rows(o_ref[...], gfin_ref[...])


def _mlp(x, gain, wu, wd, gfin, tm, tf):
    m = x.shape[0]
    return pl.pallas_call(
        _mlp_kernel,
        grid=(m // tm, D_FF // tf),
        in_specs=[
            pl.BlockSpec((tm, D_MODEL), lambda i, j: (i, 0)),
            pl.BlockSpec((1, D_MODEL), lambda i, j: (0, 0)),
            pl.BlockSpec((D_MODEL, tf), lambda i, j: (0, j)),
            pl.BlockSpec((tf, D_MODEL), lambda i, j: (j, 0)),
            pl.BlockSpec((1, D_MODEL), lambda i, j: (0, 0)),
        ],
        out_specs=pl.BlockSpec((tm, D_MODEL), lambda i, j: (i, 0)),
        out_shape=jax.ShapeDtypeStruct((m, D_MODEL), F32),
        scratch_shapes=[pltpu.VMEM((tm, D_MODEL), BF16)],
        compiler_params=pltpu.CompilerParams(
            dimension_semantics=("arbitrary", "arbitrary"), vmem_limit_bytes=VMEM_LIMIT),
        name="mlp",
    )(x, gain, wu, wd, gfin)


def kernel(x_prompt, x_sample, cache_k, cache_v, state_hgrn, norm_attn, w_in, lower_bounds, hg_norm,
           lambda_q1, lambda_k1, lambda_q2, lambda_k2, subln, w_out, norm_mlp, w_up, w_down, norm_final):
    depth = w_in.shape[0]
    assert depth == 1
    n_pb, n_p, _ = x_prompt.shape
    n_sb, n_s, _ = x_sample.shape
    assert n_pb == 1
    past = cache_k.shape[2]

    w_in_bf = w_in[0].astype(BF16)
    lams = (lambda_q1, lambda_k1, lambda_q2, lambda_k2)
    gfin = norm_final.reshape(1, D_MODEL)

    xp = x_prompt.reshape(n_p, D_MODEL)
    xs = x_sample.reshape(n_sb * n_s, D_MODEL)

    gates, q_bf, k_bf, vt_bf, vs_bf, kf_p, vf_p, kf_s, vf_s = _inproj(xp, xs, norm_attn, w_in_bf, tm=256)

    s0_p = jnp.zeros((1, N_HEADS, HEAD_W, HEAD_W), F32)
    ohg_p, st_p = _gla(gates, lower_bounds, hg_norm, s0_p, row0=0, n_seq=1, seq_len=n_p, rows=512)
    ohg_s, st_s = _gla(gates, lower_bounds, hg_norm, state_hgrn[0], row0=n_p, n_seq=n_sb, seq_len=n_s, rows=n_s)

    oat_p = _attn_prompt(q_bf, k_bf, vt_bf, lams, subln, n=n_p, t=512, n_grp=2)
    ck = cache_k[0].reshape(n_sb, past * N_HEADS, HEAD_W)
    cv = cache_v[0].reshape(n_sb, past * N_HEADS, HEAD_W)
    oat_s = _attn_sample(q_bf, ck, cv, k_bf, vs_bf, lams, subln, row0=n_p, n_q=n_s, tk=1024)

    x1_p = _outproj(xp, ohg_p, oat_p, w_out[0], tm=512)
    x1_s = _outproj(xs, ohg_s, oat_s, w_out[0], tm=512)

    y_p = _mlp(x1_p, norm_mlp, w_up[0], w_down[0], gfin, tm=1024, tf=512)
    y_s = _mlp(x1_s, norm_mlp, w_up[0], w_down[0], gfin, tm=512, tf=512)

    return (
        y_p.reshape(n_pb, n_p, D_MODEL),
        y_s.reshape(n_sb, n_s, D_MODEL),
        kf_p.reshape(1, n_pb, n_p, N_HEADS, HEAD_W),
        vf_p.reshape(1, n_pb, n_p, N_HEADS, HEAD_W),
        st_p.reshape(1, n_pb, N_HEADS, HEAD_W, HEAD_W),
        kf_s.reshape(1, n_sb, n_s, N_HEADS, HEAD_W),
        vf_s.reshape(1, n_sb, n_s, N_HEADS, HEAD_W),
        st_s.reshape(1, n_sb, N_HEADS, HEAD_W, HEAD_W),
    )
```

```python
import functools
import math

import jax
import jax.numpy as jnp
from jax import lax
from jax.experimental import pallas as pl
from jax.experimental.pallas import tpu as pltpu

F32 = jnp.float32
BF16 = jnp.bfloat16

D_MODEL = 2048
HG_WIDTH = 1024
ATT_WIDTH = 1024
N_HEADS = 8
HEAD_W = 128
ATT_DH = 64
CHUNK = 64
SUB = 8
N_SUB = CHUNK // SUB
MIN_FACTORED_LB = 2.0 ** (-100.0 / SUB)
MIN_CHUNK_FACTORED_LB = 2.0 ** (-100.0 / CHUNK)
D_FF = 4 * D_MODEL
N_SEG = 7
EPS = 1e-6
NEG_INF = -1e30
LOG2E = 1.4426950408889634
Q_SCALE = ATT_DH ** -0.5 * LOG2E
LAM_INIT = 0.8 - 0.6 * math.exp(-0.3 * 0)
ONES_ROWS = 16
FINISH_COLS = 256
STAGE_COLS = 256

VMEM_LIMIT = 56 * 1024 * 1024


def _rmsnorm_rows(x, gain):
    return x * lax.rsqrt(jnp.mean(x * x, axis=-1, keepdims=True) + EPS) * gain


def _sigmoid(x):
    return 0.5 * jnp.tanh(0.5 * x) + 0.5


def _head(h):
    return slice(h * HEAD_W, (h + 1) * HEAD_W)


def _inproj_kernel(xp_ref, xs_ref, gain_ref, w_ref, gates_ref, q_ref, kb_ref, vt_ref, vbs_ref,
                   kfp_ref, vfp_ref, kfs_ref, vfs_ref, *, n_p):
    def project(x_ref, kf_ref, vf_ref, vb_ref):
        tm = x_ref.shape[0]
        h = _rmsnorm_rows(x_ref[...], gain_ref[...]).astype(BF16)

        def segment(s):
            return jnp.dot(h, w_ref[:, s * HG_WIDTH:(s + 1) * HG_WIDTH], preferred_element_type=F32)

        k = segment(5)
        v = segment(6)
        kb_ref[...] = k.astype(BF16)
        vt_ref[...] = v.T.astype(BF16)
        if vb_ref is not None:
            vb_ref[...] = v.astype(BF16)
        for hd in range(N_HEADS):
            kf_ref[pl.ds(hd, tm, stride=N_HEADS), :] = k[:, _head(hd)]
            vf_ref[pl.ds(hd, tm, stride=N_HEADS), :] = v[:, _head(hd)]
        q_ref[...] = (segment(4) * Q_SCALE).astype(BF16)
        for s in range(4):
            gates_ref[s] = segment(s)

    is_prompt = pl.program_id(0) < n_p

    @pl.when(is_prompt)
    def _():
        project(xp_ref, kfp_ref, vfp_ref, None)

    @pl.when(jnp.logical_not(is_prompt))
    def _():
        project(xs_ref, kfs_ref, vfs_ref, vbs_ref)


def _inproj(xp, xs, gain, w_bf, tm):
    m_p, m_s = xp.shape[0], xs.shape[0]
    n_p, n_s = m_p // tm, m_s // tm
    m = m_p + m_s
    seg = HG_WIDTH
    row = lambda i: (i, 0)
    p_row = lambda i: (jnp.minimum(i, n_p - 1), 0)
    s_row = lambda i: (jnp.maximum(i - n_p, 0), 0)
    once = dict(pipeline_mode=pl.Buffered(1))
    return pl.pallas_call(
        functools.partial(_inproj_kernel, n_p=n_p),
        grid=(n_p + n_s,),
        in_specs=[
            pl.BlockSpec((tm, D_MODEL), p_row),
            pl.BlockSpec((tm, D_MODEL), s_row, **once),
            pl.BlockSpec((1, D_MODEL), lambda i: (0, 0)),
            pl.BlockSpec((D_MODEL, N_SEG * seg), lambda i: (0, 0), **once),
        ],
        out_specs=[
            pl.BlockSpec((4, tm, seg), lambda i: (0, i, 0)),
            pl.BlockSpec((tm, seg), row),
            pl.BlockSpec((tm, seg), row),
            pl.BlockSpec((seg, tm), lambda i: (0, i)),
            pl.BlockSpec((tm, seg), s_row, **once),
            pl.BlockSpec((tm * N_HEADS, HEAD_W), p_row),
            pl.BlockSpec((tm * N_HEADS, HEAD_W), p_row),
            pl.BlockSpec((tm * N_HEADS, HEAD_W), s_row, **once),
            pl.BlockSpec((tm * N_HEADS, HEAD_W), s_row, **once),
        ],
        out_shape=[
            jax.ShapeDtypeStruct((4, m, seg), F32),
            jax.ShapeDtypeStruct((m, seg), BF16),
            jax.ShapeDtypeStruct((m, seg), BF16),
            jax.ShapeDtypeStruct((seg, m), BF16),
            jax.ShapeDtypeStruct((m_s, seg), BF16),
            jax.ShapeDtypeStruct((m_p * N_HEADS, HEAD_W), F32),
            jax.ShapeDtypeStruct((m_p * N_HEADS, HEAD_W), F32),
            jax.ShapeDtypeStruct((m_s * N_HEADS, HEAD_W), F32),
            jax.ShapeDtypeStruct((m_s * N_HEADS, HEAD_W), F32),
        ],
        compiler_params=pltpu.CompilerParams(
            dimension_semantics=("arbitrary",), vmem_limit_bytes=VMEM_LIMIT),
        name="inproj",
    )(xp, xs, gain, w_bf)


def _gla_kernel(hq_ref, hf_ref, hi_ref, hg_ref, lbnd_ref, gn_ref, s0_ref, o_ref, sout_ref, st_ref,
                q_ref, b_ref, *, n_chunks):
    r = pl.program_id(1)

    @pl.when(r == 0)
    def _():
        st_ref[...] = jnp.concatenate([s0_ref[0, h].T for h in range(N_HEADS)], axis=1)

    lbs = lbnd_ref[...]
    e = jnp.exp(lbs - jnp.max(lbs, axis=0, keepdims=True))
    lb = e[0:1, :] / jnp.sum(e, axis=0, keepdims=True)
    gn = jnp.concatenate([gn_ref[...]] * N_HEADS, axis=1)

    ri = lax.broadcasted_iota(jnp.int32, (CHUNK, CHUNK), 0)
    ci = lax.broadcasted_iota(jnp.int32, (CHUNK, CHUNK), 1)
    tril = (ri >= ci).astype(BF16)
    pr = lax.broadcasted_iota(jnp.int32, (2 * HEAD_W, 2 * HEAD_W), 0)
    pc = lax.broadcasted_iota(jnp.int32, (2 * HEAD_W, 2 * HEAD_W), 1)
    pair_ones = ((pr >> 7) == (pc >> 7)).astype(BF16)
    gr = lax.broadcasted_iota(jnp.int32, (CHUNK, CHUNK * SUB), 0)
    gc = lax.broadcasted_iota(jnp.int32, (CHUNK, CHUNK * SUB), 1)
    seg_sum = (((gc >> 3) == gr) & ((gc & (SUB - 1)) <= (gr & (SUB - 1)))).astype(BF16)
    n_off = (N_SUB - 1) * SUB
    n_key = SUB * (N_SUB * (N_SUB - 1) // 2)
    orow = lax.broadcasted_iota(jnp.int32, (n_off, n_key), 0) >> 3
    ocol = lax.broadcasted_iota(jnp.int32, (n_off, n_key), 1)
    ocol_seg = sum((ocol >= 4 * i * (i - 1)).astype(jnp.int32) for i in range(2, N_SUB))
    off_mask = orow == ocol_seg
    nt = (((1,), (1,)), ((), ()))
    tn = (((0,), (0,)), ((), ()))

    n_all = n_key + CHUNK
    frow = lax.broadcasted_iota(jnp.int32, (CHUNK, n_all), 0)
    fcol = lax.broadcasted_iota(jnp.int32, (CHUNK, n_all), 1)
    fseg = sum((fcol >= 4 * i * (i - 1)).astype(jnp.int32) for i in range(2, N_SUB))
    fsame = fcol - n_key
    all_mask = (((fcol < n_key) & (fseg + 1 == (frow >> 3)))
                | ((fcol >= n_key) & ((fsame >> 3) == (frow >> 3)) & ((fsame & (SUB - 1)) <= (frow & (SUB - 1)))))

    causal = ri >= ci

    def chunk(c, carry, mode):
        r0 = pl.multiple_of(c * CHUNK, CHUNK)
        hq = hq_ref[0, pl.ds(r0, CHUNK), :]
        hf = hf_ref[0, pl.ds(r0, CHUNK), :]
        v = hi_ref[0, pl.ds(r0, CHUNK), :]
        hg = hg_ref[0, pl.ds(r0, CHUNK), :]

        f = lb + (1.0 - lb) * _sigmoid(hf)
        g = jnp.log(f) * LOG2E
        kk = 1.0 - f
        q = hq * _sigmoid(hq)
        g_hi = g.astype(BF16)
        g_rest = g - g_hi.astype(F32)
        g_mid = g_rest.astype(BF16)
        g_lo = (g_rest - g_mid.astype(F32)).astype(BF16)
        b3 = jnp.dot(tril, jnp.concatenate([g_hi, g_mid, g_lo], axis=1), preferred_element_type=F32)
        b = b3[:, :HG_WIDTH] + b3[:, HG_WIDTH:2 * HG_WIDTH] + b3[:, 2 * HG_WIDTH:]
        v_bf = v.astype(BF16)

        st = st_ref[...]
        st_bf = st.astype(BF16)
        b_last = b[CHUNK - 1:CHUNK, :]
        q_dec = q * jnp.exp2(b)
        q_in = q_dec.astype(BF16)
        k_dec = (kk * jnp.exp2(b_last - b)).astype(BF16)
        o_inter = jnp.concatenate(
            [lax.dot_general(q_in[:, _head(h)], st_bf[:, _head(h)], nt, preferred_element_type=F32)
             for h in range(N_HEADS)], axis=1)
        upd = jnp.concatenate(
            [lax.dot_general(v_bf[:, _head(h)], k_dec[:, _head(h)], tn, preferred_element_type=F32)
             for h in range(N_HEADS)], axis=1)
        st_ref[...] = jnp.exp2(b_last) * st + upd

        qt, kh, vh = [], [], []
        for i in range(1, N_SUB) if mode != "chunk" else ():
            lo = i * SUB
            b_start = b[lo - 1:lo, :]
            qt.append(q[lo:lo + SUB] * jnp.exp2(b[lo:lo + SUB] - b_start))
            kh.append(kk[:lo] * jnp.exp2(b_start - b[:lo]))
            vh.append(v[:lo])

        if mode == "chunk":
            k_inv = (kk * jnp.exp2(-b)).astype(BF16)
            a_all = [lax.dot_general(q_in[:, _head(h)], k_inv[:, _head(h)], nt, preferred_element_type=F32)
                     for h in range(N_HEADS)]
            o_intra = jnp.concatenate(
                [jnp.dot(jnp.where(causal, a_all[h], 0.0).astype(BF16), v_bf[:, _head(h)],
                         preferred_element_type=F32) for h in range(N_HEADS)], axis=1)
        elif mode == "sub_block":
            b0 = jnp.concatenate(
                [jnp.zeros((SUB, HG_WIDTH), F32)]
                + [jnp.broadcast_to(b[i * SUB - 1:i * SUB, :], (SUB, HG_WIDTH)) for i in range(1, N_SUB)], axis=0)
            kd = kk * jnp.exp2(b0 - b)
            q_all = jnp.concatenate([q_dec[:SUB]] + qt, axis=0).astype(BF16)
            k_all = jnp.concatenate(kh + [kd], axis=0).astype(BF16)
            v_all = jnp.concatenate(vh + [v], axis=0).astype(BF16)
            a_all = [lax.dot_general(q_all[:, _head(h)], k_all[:, _head(h)], nt, preferred_element_type=F32)
                     for h in range(N_HEADS)]
            o_intra = jnp.concatenate(
                [jnp.dot(jnp.where(all_mask, a_all[h], 0.0).astype(BF16), v_all[:, _head(h)],
                         preferred_element_type=F32) for h in range(N_HEADS)], axis=1)
        else:
            for h in range(N_HEADS):
                q_ref[h] = q[:, _head(h)]
                b_ref[h] = b[:, _head(h)]
            qt = jnp.concatenate(qt, axis=0).astype(BF16)
            kh = jnp.concatenate(kh, axis=0).astype(BF16)
            vh = jnp.concatenate(vh, axis=0).astype(BF16)
            a_off = [lax.dot_general(qt[:, _head(h)], kh[:, _head(h)], nt, preferred_element_type=F32)
                     for h in range(N_HEADS)]

            a_rep = []
            for j in range(N_HEADS // 2):
                p_pair = []
                for h in (2 * j, 2 * j + 1):
                    kk_h = kk[:, _head(h)]
                    b_h = b[:, _head(h)]
                    rows = []
                    for r in range(CHUNK):
                        lo = r - r % SUB
                        q_row = q_ref[h, pl.ds(r, SUB, stride=0), :]
                        b_row = b_ref[h, pl.ds(r, SUB, stride=0), :]
                        rows.append((q_row * kk_h[lo:lo + SUB])
                                    * jnp.exp2(jnp.minimum(b_row - b_h[lo:lo + SUB], 0.0)))
                    p_pair.append(jnp.concatenate(rows, axis=0).astype(BF16))
                a_rep.append(jnp.dot(jnp.concatenate(p_pair, axis=1), pair_ones,
                                     preferred_element_type=F32))

            o_off = jnp.concatenate(
                [jnp.dot(jnp.where(off_mask, a_off[h], 0.0).astype(BF16), vh[:, _head(h)],
                         preferred_element_type=F32) for h in range(N_HEADS)], axis=1)
            o_diag = []
            for j in range(N_HEADS // 2):
                pair = slice(2 * HEAD_W * j, 2 * HEAD_W * (j + 1))
                v_rep = jnp.broadcast_to(v[:, pair].reshape(N_SUB, 1, SUB, 2 * HEAD_W),
                                         (N_SUB, SUB, SUB, 2 * HEAD_W))
                w = (a_rep[j].reshape(N_SUB, SUB, SUB, 2 * HEAD_W) * v_rep).reshape(CHUNK * SUB, 2 * HEAD_W)
                o_diag.append(jnp.dot(seg_sum, w.astype(BF16), preferred_element_type=F32))
            o_intra = jnp.concatenate(o_diag, axis=1) + jnp.concatenate(
                [jnp.zeros((SUB, HG_WIDTH), F32), o_off], axis=0)
        o = o_inter + o_intra

        y = jnp.concatenate(
            [o[:, _head(h)] * lax.rsqrt(jnp.mean(o[:, _head(h)] * o[:, _head(h)], axis=-1, keepdims=True) + EPS)
             for h in range(N_HEADS)], axis=1)
        y = y * gn * (hg * _sigmoid(hg))
        o_ref[pl.ds(r0, CHUNK), :] = y.astype(BF16)
        return carry

    lb_min = jnp.min(lb)
    unroll = min(n_chunks, 4)

    @pl.when(lb_min > MIN_CHUNK_FACTORED_LB)
    def _():
        lax.fori_loop(0, n_chunks, functools.partial(chunk, mode="chunk"), 0, unroll=unroll)

    @pl.when((lb_min > MIN_FACTORED_LB) & (lb_min <= MIN_CHUNK_FACTORED_LB))
    def _():
        lax.fori_loop(0, n_chunks, functools.partial(chunk, mode="sub_block"), 0, unroll=unroll)

    @pl.when(lb_min <= MIN_FACTORED_LB)
    def _():
        lax.fori_loop(0, n_chunks, functools.partial(chunk, mode="pairwise"), 0)

    @pl.when(r == pl.num_programs(1) - 1)
    def _():
        st = st_ref[...]
        for h in range(N_HEADS):
            sout_ref[0, h] = st[:, _head(h)].T


def _gla(gates, lower_bounds, hg_norm, s0, row0, n_seq, seq_len, rows):
    m = n_seq * seq_len
    nr = seq_len // rows
    blk0 = row0 // rows
    seg = lambda s: pl.BlockSpec((1, rows, HG_WIDTH), lambda b, r: (s, blk0 + b * nr + r, 0))
    state = pl.BlockSpec((1, N_HEADS, HEAD_W, HEAD_W), lambda b, r: (b, 0, 0, 0))
    return pl.pallas_call(
        functools.partial(_gla_kernel, n_chunks=rows // CHUNK),
        grid=(n_seq, nr),
        in_specs=[
            seg(0), seg(1), seg(2), seg(3),
            pl.BlockSpec((lower_bounds.shape[0], HG_WIDTH), lambda b, r: (0, 0)),
            pl.BlockSpec((1, HEAD_W), lambda b, r: (0, 0)),
            state,
        ],
        out_specs=[pl.BlockSpec((rows, HG_WIDTH), lambda b, r: (b * nr + r, 0)), state],
        out_shape=[
            jax.ShapeDtypeStruct((m, HG_WIDTH), BF16),
            jax.ShapeDtypeStruct((n_seq, N_HEADS, HEAD_W, HEAD_W), F32),
        ],
        scratch_shapes=[pltpu.VMEM((HEAD_W, HG_WIDTH), F32),
                        pltpu.VMEM((N_HEADS, CHUNK, HEAD_W), F32),
                        pltpu.VMEM((N_HEADS, CHUNK, HEAD_W), F32)],
        compiler_params=pltpu.CompilerParams(
            dimension_semantics=("arbitrary", "arbitrary"), vmem_limit_bytes=VMEM_LIMIT),
        name="hgrn2",
    )(gates, gates, gates, gates, lower_bounds, hg_norm, s0)


def _stacked_query(q):
    lane = lax.broadcasted_iota(jnp.int32, q.shape, 1)
    qbig = jnp.concatenate([jnp.where(lane < ATT_DH, q, 0.0), jnp.where(lane >= ATT_DH, q, 0.0)], axis=0)
    return qbig.T.astype(BF16)


def _lambda(lq1_ref, lk1_ref, lq2_ref, lk2_ref):
    s1 = jnp.sum(lq1_ref[...] * lk1_ref[...], axis=-1, keepdims=True)
    s2 = jnp.sum(lq2_ref[...] * lk2_ref[...], axis=-1, keepdims=True)
    return jnp.exp(s1) - jnp.exp(s2) + LAM_INIT


def _attn_finish(acc, l, lam, sub, n):
    o_both = (acc * (1.0 / l)).T
    o = o_both[:n] - lam * o_both[n:]
    return _rmsnorm_rows(o, sub) * (1.0 - LAM_INIT)


def _attn_prompt_kernel(q_ref, qn_ref, k_ref, vt_ref, lq1_ref, lk1_ref, lq2_ref, lk2_ref, sub_ref, o_ref,
                        qq_ref, s_ref, smax_ref, mask_ref, m_ref, acc_ref, *, t, n_grp):
    qi = pl.program_id(1)
    grp = range(n_grp)
    slot = qi % 2
    m_ref[...] = jnp.full(m_ref.shape, NEG_INF, F32)
    acc_ref[...] = jnp.zeros(acc_ref.shape, F32)
    ones_rows = jnp.ones((ONES_ROWS, t), BF16)

    def values_t(g, kv):
        k0 = pl.multiple_of(kv * t, t)
        return jnp.concatenate([vt_ref[_head(g), pl.ds(k0, t)], ones_rows], axis=0)

    def first_scores(g, qslot, cols):
        s = jnp.dot(k_ref[0:t, _head(g)], qq_ref[qslot, g, :, cols], preferred_element_type=F32)
        s_ref[g, 0, :, cols] = s
        smax_ref[g, 0, :, cols] = jnp.max(s, axis=0, keepdims=True)

    def stage(kv, cur, oth):
        k_next = pl.multiple_of((kv + 1) * t, t)
        vt = [values_t(g, kv) for g in grp]
        for c in range(0, 2 * t, STAGE_COLS):
            cols = slice(c, c + STAGE_COLS)
            for g in grp:
                s_next = jnp.dot(k_ref[pl.ds(k_next, t), _head(g)], qq_ref[slot, g, :, cols],
                                 preferred_element_type=F32)
                s_ref[g, oth, :, cols] = s_next
                smax_ref[g, oth, :, cols] = jnp.max(s_next, axis=0, keepdims=True)
                m_prev = m_ref[g, :, cols]
                m_new = jnp.maximum(m_prev, smax_ref[g, cur, :, cols])
                m_ref[g, :, cols] = m_new
                p = jnp.exp2(s_ref[g, cur, :, cols] - m_new).astype(BF16)
                acc_ref[g, :, cols] = (jnp.exp2(m_prev - m_new) * acc_ref[g, :, cols]
                                       + jnp.dot(vt[g], p, preferred_element_type=F32))

    @pl.when((pl.program_id(0) == 0) & (qi == 0))
    def _():
        kpos = lax.broadcasted_iota(jnp.int32, mask_ref.shape, 0)
        col = lax.broadcasted_iota(jnp.int32, mask_ref.shape, 1)
        qpos = jnp.where(col >= t, col - t, col)
        mask_ref[...] = jnp.where((kpos >> 6) <= (qpos >> 6), 0.0, NEG_INF)

    def finish(cur):
        for g in grp:
            qq_ref[1 - slot, g] = _stacked_query(qn_ref[:, _head(g)].astype(F32))
        vt = [values_t(g, qi) for g in grp]
        acc = [[] for _ in grp]
        for c in range(0, 2 * t, FINISH_COLS):
            cols = slice(c, c + FINISH_COLS)
            for g in grp:
                s = s_ref[g, cur, :, cols] + mask_ref[:, cols]
                m_prev = m_ref[g, :, cols]
                m_new = jnp.maximum(m_prev, jnp.max(s, axis=0, keepdims=True))
                p = jnp.exp2(s - m_new).astype(BF16)
                acc[g].append(jnp.exp2(m_prev - m_new) * acc_ref[g, :, cols]
                              + jnp.dot(vt[g], p, preferred_element_type=F32))
                first_scores(g, 1 - slot, cols)
        lam = _lambda(lq1_ref, lk1_ref, lq2_ref, lk2_ref)
        for g in grp:
            a = jnp.concatenate(acc[g], axis=1)
            o_ref[:, _head(g)] = _attn_finish(a[:HEAD_W], a[HEAD_W:HEAD_W + 1], lam, sub_ref[...],
                                              t).astype(BF16)

    @pl.when(qi == 0)
    def _():
        for g in grp:
            qq_ref[0, g] = _stacked_query(q_ref[:, _head(g)].astype(F32))
        for c in range(0, 2 * t, FINISH_COLS):
            for g in grp:
                first_scores(g, 0, slice(c, c + FINISH_COLS))

    def pair(j, carry):
        stage(2 * j, 0, 1)
        stage(2 * j + 1, 1, 0)
        return carry

    lax.fori_loop(0, qi // 2, pair, 0)

    @pl.when(qi % 2 == 1)
    def _():
        stage(qi - 1, 0, 1)
        finish(1)

    @pl.when(qi % 2 == 0)
    def _():
        finish(0)


def _attn_prompt(q_bf, k_bf, vt_bf, lams, subln, n, t, n_grp):
    w = n_grp * HEAD_W
    small = lambda shape: pl.BlockSpec(shape, lambda h, i: (0, 0))
    return pl.pallas_call(
        functools.partial(_attn_prompt_kernel, t=t, n_grp=n_grp),
        grid=(N_HEADS // n_grp, n // t),
        in_specs=[
            pl.BlockSpec((t, w), lambda h, i: (i, h)),
            pl.BlockSpec((t, w), lambda h, i: (jnp.minimum(i + 1, n // t - 1), h)),
            pl.BlockSpec((n, w), lambda h, i: (0, h)),
            pl.BlockSpec((w, n), lambda h, i: (h, 0)),
            small((1, ATT_DH)), small((1, ATT_DH)), small((1, ATT_DH)), small((1, ATT_DH)),
            small((1, HEAD_W)),
        ],
        out_specs=pl.BlockSpec((t, w), lambda h, i: (i, h)),
        out_shape=jax.ShapeDtypeStruct((n, ATT_WIDTH), BF16),
        scratch_shapes=[
            pltpu.VMEM((2, n_grp, HEAD_W, 2 * t), BF16),
            pltpu.VMEM((n_grp, 2, t, 2 * t), F32),
            pltpu.VMEM((n_grp, 2, 1, 2 * t), F32),
            pltpu.VMEM((t, 2 * t), F32),
            pltpu.VMEM((n_grp, 1, 2 * t), F32),
            pltpu.VMEM((n_grp, HEAD_W + ONES_ROWS, 2 * t), F32),
        ],
        compiler_params=pltpu.CompilerParams(
            dimension_semantics=("arbitrary", "arbitrary"), vmem_limit_bytes=VMEM_LIMIT),
        name="attn_prompt",
    )(q_bf, q_bf, k_bf, vt_bf, *lams, subln)


def _attn_sample_kernel(q_ref, kc_ref, vc_ref, kn_ref, vn_ref, lq1_ref, lk1_ref, lq2_ref, lk2_ref,
                        sub_ref, o_ref, qq_ref, m_ref, l_ref, acc_ref, *, n_q, tk):
    t = pl.program_id(1)
    pairs = range(N_HEADS // 2)
    w2 = 2 * HEAD_W

    @pl.when(t == 0)
    def _():
        zero = jnp.zeros((HEAD_W, 2 * n_q), BF16)
        for j in pairs:
            qa = _stacked_query(q_ref[:, _head(2 * j)].astype(F32))
            qb = _stacked_query(q_ref[:, _head(2 * j + 1)].astype(F32))
            qq_ref[j] = jnp.concatenate([jnp.concatenate([qa, zero], axis=1),
                                         jnp.concatenate([zero, qb], axis=1)], axis=0)
        m_ref[...] = jnp.full(m_ref.shape, NEG_INF, F32)
        l_ref[...] = jnp.zeros(l_ref.shape, F32)
        acc_ref[...] = jnp.zeros(acc_ref.shape, F32)

    def update(k, v):
        s = [jnp.dot(k[j], qq_ref[j], preferred_element_type=F32) for j in pairs]
        p, alpha = [], []
        for j in pairs:
            m_prev = m_ref[j]
            m_new = jnp.maximum(m_prev, jnp.max(s[j], axis=0, keepdims=True))
            m_ref[j] = m_new
            a = jnp.exp2(m_prev - m_new)
            pj = jnp.exp2(s[j] - m_new)
            l_ref[j] = a * l_ref[j] + jnp.sum(pj, axis=0, keepdims=True)
            p.append(pj.astype(BF16))
            alpha.append(a)
        pv = [lax.dot_general(v[j], p[j], (((0,), (0,)), ((), ())), preferred_element_type=F32)
              for j in pairs]
        for j in pairs:
            acc_ref[j] = alpha[j] * acc_ref[j] + pv[j]

    def cache_pair(ref, j):
        return jnp.concatenate([ref[0, pl.ds(2 * j, tk, stride=N_HEADS), :],
                                ref[0, pl.ds(2 * j + 1, tk, stride=N_HEADS), :]], axis=1).astype(BF16)

    update([cache_pair(kc_ref, j) for j in pairs], [cache_pair(vc_ref, j) for j in pairs])

    @pl.when(t == pl.num_programs(1) - 1)
    def _():
        update([kn_ref[:, w2 * j:w2 * (j + 1)] for j in pairs], [vn_ref[:, w2 * j:w2 * (j + 1)] for j in pairs])
        lam = _lambda(lq1_ref, lk1_ref, lq2_ref, lk2_ref)
        for j in pairs:
            acc = acc_ref[j]
            l = l_ref[j]
            for i, h in enumerate((2 * j, 2 * j + 1)):
                o_ref[:, _head(h)] = _attn_finish(acc[_head(i), _head(i)], l[:, _head(i)], lam, sub_ref[...],
                                                  n_q).astype(BF16)


def _attn_sample(q_bf, cache_k, cache_v, kn_bf, vn_bf, lams, subln, row0, n_q, tk):
    n_b = cache_k.shape[0]
    past = cache_k.shape[1] // N_HEADS
    assert past % CHUNK == 0 and n_q <= CHUNK and past % tk == 0 and row0 % n_q == 0
    blk0 = row0 // n_q
    small = lambda shape: pl.BlockSpec(shape, lambda b, t: (0, 0))
    rows = pl.BlockSpec((n_q, ATT_WIDTH), lambda b, t: (b, 0))
    rows_all = pl.BlockSpec((n_q, ATT_WIDTH), lambda b, t: (blk0 + b, 0))
    cache = pl.BlockSpec((1, tk * N_HEADS, HEAD_W), lambda b, t: (b, t, 0))
    return pl.pallas_call(
        functools.partial(_attn_sample_kernel, n_q=n_q, tk=tk),
        grid=(n_b, past // tk),
        in_specs=[
            rows_all, cache, cache, rows_all, rows,
            small((1, ATT_DH)), small((1, ATT_DH)), small((1, ATT_DH)), small((1, ATT_DH)),
            small((1, HEAD_W)),
        ],
        out_specs=rows,
        out_shape=jax.ShapeDtypeStruct((n_b * n_q, ATT_WIDTH), BF16),
        scratch_shapes=[
            pltpu.VMEM((N_HEADS // 2, 2 * HEAD_W, 4 * n_q), BF16),
            pltpu.VMEM((N_HEADS // 2, 1, 4 * n_q), F32),
            pltpu.VMEM((N_HEADS // 2, 1, 4 * n_q), F32),
            pltpu.VMEM((N_HEADS // 2, 2 * HEAD_W, 4 * n_q), F32),
        ],
        compiler_params=pltpu.CompilerParams(
            dimension_semantics=("arbitrary", "arbitrary"), vmem_limit_bytes=VMEM_LIMIT),
        name="attn_sample",
    )(q_bf, cache_k, cache_v, kn_bf, vn_bf, *lams, subln)


def _outproj_kernel(x_ref, a_ref, b_ref, w_ref, o_ref, wb_ref):
    @pl.when(pl.program_id(0) == 0)
    def _():
        wb_ref[...] = w_ref[...].astype(BF16)

    o_ref[...] = (x_ref[...]
                  + jnp.dot(a_ref[...], wb_ref[:HG_WIDTH, :], preferred_element_type=F32)
                  + jnp.dot(b_ref[...], wb_ref[HG_WIDTH:, :], preferred_element_type=F32))


def _outproj(x, mix_hg, mix_at, w, tm):
    m = x.shape[0]
    row = lambda i: (i, 0)
    return pl.pallas_call(
        _outproj_kernel,
        grid=(m // tm,),
        in_specs=[
            pl.BlockSpec((tm, D_MODEL), row),
            pl.BlockSpec((tm, HG_WIDTH), row),
            pl.BlockSpec((tm, ATT_WIDTH), row),
            pl.BlockSpec((HG_WIDTH + ATT_WIDTH, D_MODEL), lambda i: (0, 0), pipeline_mode=pl.Buffered(1)),
        ],
        out_specs=pl.BlockSpec((tm, D_MODEL), row),
        out_shape=jax.ShapeDtypeStruct((m, D_MODEL), F32),
        scratch_shapes=[pltpu.VMEM((HG_WIDTH + ATT_WIDTH, D_MODEL), BF16)],
        compiler_params=pltpu.CompilerParams(
            dimension_semantics=("arbitrary",), vmem_limit_bytes=VMEM_LIMIT),
        name="outproj",
    )(x, mix_hg, mix_at, w)


def _mlp_kernel(x_ref, gain_ref, wu_ref, wd_ref, gfin_ref, o_ref, h_ref):
    j = pl.program_id(1)

    def delta():
        u = jnp.dot(h_ref[...], wu_ref[...].astype(BF16), preferred_element_type=F32)
        u = jnp.square(jnp.maximum(u, 0.0)).astype(BF16)
        return jnp.dot(u, wd_ref[...].astype(BF16), preferred_element_type=F32)

    @pl.when(j == 0)
    def _():
        h_ref[...] = _rmsnorm_rows(x_ref[...], gain_ref[...]).astype(BF16)
        o_ref[...] = x_ref[...] + delta()

    @pl.when(j > 0)
    def _():
        o_ref[...] += delta()

    @pl.when(j == pl.num_programs(1) - 1)
    def _():
        o_ref[...] = _rmsnorm_rows(o_ref[...], gfin_ref[...])


def _mlp(x, gain, wu, wd, gfin, tm, tf):
    m = x.shape[0]
    return pl.pallas_call(
        _mlp_kernel,
        grid=(m // tm, D_FF // tf),
        in_specs=[
            pl.BlockSpec((tm, D_MODEL), lambda i, j: (i, 0)),
            pl.BlockSpec((1, D_MODEL), lambda i, j: (0, 0)),
            pl.BlockSpec((D_MODEL, tf), lambda i, j: (0, j)),
            pl.BlockSpec((tf, D_MODEL), lambda i, j: (j, 0)),
            pl.BlockSpec((1, D_MODEL), lambda i, j: (0, 0)),
        ],
        out_specs=pl.BlockSpec((tm, D_MODEL), lambda i, j: (i, 0)),
        out_shape=jax.ShapeDtypeStruct((m, D_MODEL), F32),
        scratch_shapes=[pltpu.VMEM((tm, D_MODEL), BF16)],
        compiler_params=pltpu.CompilerParams(
            dimension_semantics=("arbitrary", "arbitrary"), vmem_limit_bytes=VMEM_LIMIT),
        name="mlp",
    )(x, gain, wu, wd, gfin)


def kernel(x_prompt, x_sample, cache_k, cache_v, state_hgrn, norm_attn, w_in, lower_bounds, hg_norm,
           lambda_q1, lambda_k1, lambda_q2, lambda_k2, subln, w_out, norm_mlp, w_up, w_down, norm_final):
    depth = w_in.shape[0]
    assert depth == 1
    n_pb, n_p, _ = x_prompt.shape
    n_sb, n_s, _ = x_sample.shape
    assert n_pb == 1
    past = cache_k.shape[2]

    w_in_bf = w_in[0].astype(BF16)
    lams = (lambda_q1, lambda_k1, lambda_q2, lambda_k2)
    gfin = norm_final.reshape(1, D_MODEL)

    xp = x_prompt.reshape(n_p, D_MODEL)
    xs = x_sample.reshape(n_sb * n_s, D_MODEL)

    gates, q_bf, k_bf, vt_bf, vs_bf, kf_p, vf_p, kf_s, vf_s = _inproj(xp, xs, norm_attn, w_in_bf, tm=256)

    s0_p = jnp.zeros((1, N_HEADS, HEAD_W, HEAD_W), F32)
    ohg_p, st_p = _gla(gates, lower_bounds, hg_norm, s0_p, row0=0, n_seq=1, seq_len=n_p, rows=512)
    ohg_s, st_s = _gla(gates, lower_bounds, hg_norm, state_hgrn[0], row0=n_p, n_seq=n_sb, seq_len=n_s, rows=n_s)

    oat_p = _attn_prompt(q_bf, k_bf, vt_bf, lams, subln, n=n_p, t=512, n_grp=2)
    ck = cache_k[0].reshape(n_sb, past * N_HEADS, HEAD_W)
    cv = cache_v[0].reshape(n_sb, past * N_HEADS, HEAD_W)
    oat_s = _attn_sample(q_bf, ck, cv, k_bf, vs_bf, lams, subln, row0=n_p, n_q=n_s, tk=2048)

    x1_p = _outproj(xp, ohg_p, oat_p, w_out[0], tm=512)
    x1_s = _outproj(xs, ohg_s, oat_s, w_out[0], tm=512)

    y_p = _mlp(x1_p, norm_mlp, w_up[0], w_down[0], gfin, tm=1024, tf=512)
    y_s = _mlp(x1_s, norm_mlp, w_up[0], w_down[0], gfin, tm=512, tf=1024)

    return (
        y_p.reshape(n_pb, n_p, D_MODEL),
        y_s.reshape(n_sb, n_s, D_MODEL),
        kf_p.reshape(1, n_pb, n_p, N_HEADS, HEAD_W),
        vf_p.reshape(1, n_pb, n_p, N_HEADS, HEAD_W),
        st_p.reshape(1, n_pb, N_HEADS, HEAD_W, HEAD_W),
        kf_s.reshape(1, n_sb, n_s, N_HEADS, HEAD_W),
        vf_s.reshape(1, n_sb, n_s, N_HEADS, HEAD_W),
        st_s.reshape(1, n_sb, N_HEADS, HEAD_W, HEAD_W),
    )
```

```python
import functools
import math

import jax
import jax.numpy as jnp
from jax import lax
from jax.experimental import pallas as pl
from jax.experimental.pallas import tpu as pltpu

F32 = jnp.float32
BF16 = jnp.bfloat16

D_MODEL = 2048
HG_WIDTH = 1024
ATT_WIDTH = 1024
N_HEADS = 8
HEAD_W = 128
ATT_DH = 64
CHUNK = 64
SUB = 8
N_SUB = CHUNK // SUB
MIN_FACTORED_LB = 2.0 ** (-100.0 / SUB)
MIN_CHUNK_FACTORED_LB = 2.0 ** (-100.0 / CHUNK)
D_FF = 4 * D_MODEL
N_SEG = 7
EPS = 1e-6
NEG_INF = -1e30
LOG2E = 1.4426950408889634
Q_SCALE = ATT_DH ** -0.5 * LOG2E
LAM_INIT = 0.8 - 0.6 * math.exp(-0.3 * 0)
ONES_ROWS = 16
FINISH_COLS = 256
STAGE_COLS = 256

VMEM_LIMIT = 56 * 1024 * 1024


def _rmsnorm_rows(x, gain):
    return x * lax.rsqrt(jnp.mean(x * x, axis=-1, keepdims=True) + EPS) * gain


def _sigmoid(x):
    return 0.5 * jnp.tanh(0.5 * x) + 0.5


def _head(h):
    return slice(h * HEAD_W, (h + 1) * HEAD_W)


def _inproj_kernel(xp_ref, xs_ref, gain_ref, w_ref, gates_ref, q_ref, kb_ref, vt_ref, vbs_ref,
                   kfp_ref, vfp_ref, kfs_ref, vfs_ref, *, n_p):
    def project(x_ref, kf_ref, vf_ref, vb_ref):
        tm = x_ref.shape[0]
        h = _rmsnorm_rows(x_ref[...], gain_ref[...]).astype(BF16)

        def segment(s):
            return jnp.dot(h, w_ref[:, s * HG_WIDTH:(s + 1) * HG_WIDTH], preferred_element_type=F32)

        k = segment(5)
        v = segment(6)
        kb_ref[...] = k.astype(BF16)
        vt_ref[...] = v.T.astype(BF16)
        if vb_ref is not None:
            vb_ref[...] = v.astype(BF16)
        for hd in range(N_HEADS):
            kf_ref[pl.ds(hd, tm, stride=N_HEADS), :] = k[:, _head(hd)]
            vf_ref[pl.ds(hd, tm, stride=N_HEADS), :] = v[:, _head(hd)]
        q_ref[...] = (segment(4) * Q_SCALE).astype(BF16)
        for s in range(4):
            gates_ref[s] = segment(s)

    is_prompt = pl.program_id(0) < n_p

    @pl.when(is_prompt)
    def _():
        project(xp_ref, kfp_ref, vfp_ref, None)

    @pl.when(jnp.logical_not(is_prompt))
    def _():
        project(xs_ref, kfs_ref, vfs_ref, vbs_ref)


def _inproj(xp, xs, gain, w_bf, tm):
    m_p, m_s = xp.shape[0], xs.shape[0]
    n_p, n_s = m_p // tm, m_s // tm
    m = m_p + m_s
    seg = HG_WIDTH
    row = lambda i: (i, 0)
    p_row = lambda i: (jnp.minimum(i, n_p - 1), 0)
    s_row = lambda i: (jnp.maximum(i - n_p, 0), 0)
    once = dict(pipeline_mode=pl.Buffered(1))
    return pl.pallas_call(
        functools.partial(_inproj_kernel, n_p=n_p),
        grid=(n_p + n_s,),
        in_specs=[
            pl.BlockSpec((tm, D_MODEL), p_row),
            pl.BlockSpec((tm, D_MODEL), s_row, **once),
            pl.BlockSpec((1, D_MODEL), lambda i: (0, 0)),
            pl.BlockSpec((D_MODEL, N_SEG * seg), lambda i: (0, 0), **once),
        ],
        out_specs=[
            pl.BlockSpec((4, tm, seg), lambda i: (0, i, 0)),
            pl.BlockSpec((tm, seg), row),
            pl.BlockSpec((tm, seg), row),
            pl.BlockSpec((seg, tm), lambda i: (0, i)),
            pl.BlockSpec((tm, seg), s_row, **once),
            pl.BlockSpec((tm * N_HEADS, HEAD_W), p_row),
            pl.BlockSpec((tm * N_HEADS, HEAD_W), p_row),
            pl.BlockSpec((tm * N_HEADS, HEAD_W), s_row, **once),
            pl.BlockSpec((tm * N_HEADS, HEAD_W), s_row, **once),
        ],
        out_shape=[
            jax.ShapeDtypeStruct((4, m, seg), F32),
            jax.ShapeDtypeStruct((m, seg), BF16),
            jax.ShapeDtypeStruct((m, seg), BF16),
            jax.ShapeDtypeStruct((seg, m), BF16),
            jax.ShapeDtypeStruct((m_s, seg), BF16),
            jax.ShapeDtypeStruct((m_p * N_HEADS, HEAD_W), F32),
            jax.ShapeDtypeStruct((m_p * N_HEADS, HEAD_W), F32),
            jax.ShapeDtypeStruct((m_s * N_HEADS, HEAD_W), F32),
            jax.ShapeDtypeStruct((m_s * N_HEADS, HEAD_W), F32),
        ],
        compiler_params=pltpu.CompilerParams(
            dimension_semantics=("arbitrary",), vmem_limit_bytes=VMEM_LIMIT),
        name="inproj",
    )(xp, xs, gain, w_bf)


def _gla_kernel(hq_ref, hf_ref, hi_ref, hg_ref, lbnd_ref, gn_ref, s0_ref, o_ref, sout_ref, st_ref,
                q_ref, b_ref, *, n_chunks, seqs_per_step):
    r = pl.program_id(1)

    def load_state(i):
        return jnp.concatenate([s0_ref[i, h].T for h in range(N_HEADS)], axis=1)

    if seqs_per_step == 1:
        @pl.when(r == 0)
        def _():
            st_ref[...] = load_state(0)

    lbs = lbnd_ref[...]
    e = jnp.exp(lbs - jnp.max(lbs, axis=0, keepdims=True))
    lb = e[0:1, :] / jnp.sum(e, axis=0, keepdims=True)
    gn = jnp.concatenate([gn_ref[...]] * N_HEADS, axis=1)

    ri = lax.broadcasted_iota(jnp.int32, (CHUNK, CHUNK), 0)
    ci = lax.broadcasted_iota(jnp.int32, (CHUNK, CHUNK), 1)
    tril = (ri >= ci).astype(BF16)
    pr = lax.broadcasted_iota(jnp.int32, (2 * HEAD_W, 2 * HEAD_W), 0)
    pc = lax.broadcasted_iota(jnp.int32, (2 * HEAD_W, 2 * HEAD_W), 1)
    pair_ones = ((pr >> 7) == (pc >> 7)).astype(BF16)
    gr = lax.broadcasted_iota(jnp.int32, (CHUNK, CHUNK * SUB), 0)
    gc = lax.broadcasted_iota(jnp.int32, (CHUNK, CHUNK * SUB), 1)
    seg_sum = (((gc >> 3) == gr) & ((gc & (SUB - 1)) <= (gr & (SUB - 1)))).astype(BF16)
    n_off = (N_SUB - 1) * SUB
    n_key = SUB * (N_SUB * (N_SUB - 1) // 2)
    orow = lax.broadcasted_iota(jnp.int32, (n_off, n_key), 0) >> 3
    ocol = lax.broadcasted_iota(jnp.int32, (n_off, n_key), 1)
    ocol_seg = sum((ocol >= 4 * i * (i - 1)).astype(jnp.int32) for i in range(2, N_SUB))
    off_mask = orow == ocol_seg
    nt = (((1,), (1,)), ((), ()))
    tn = (((0,), (0,)), ((), ()))

    n_all = n_key + CHUNK
    frow = lax.broadcasted_iota(jnp.int32, (CHUNK, n_all), 0)
    fcol = lax.broadcasted_iota(jnp.int32, (CHUNK, n_all), 1)
    fseg = sum((fcol >= 4 * i * (i - 1)).astype(jnp.int32) for i in range(2, N_SUB))
    fsame = fcol - n_key
    all_mask = (((fcol < n_key) & (fseg + 1 == (frow >> 3)))
                | ((fcol >= n_key) & ((fsame >> 3) == (frow >> 3)) & ((fsame & (SUB - 1)) <= (frow & (SUB - 1)))))

    causal = ri >= ci

    def chunk(c, carry, mode):
        r0 = pl.multiple_of(c * CHUNK, CHUNK)
        hq = hq_ref[0, pl.ds(r0, CHUNK), :]
        hf = hf_ref[0, pl.ds(r0, CHUNK), :]
        v = hi_ref[0, pl.ds(r0, CHUNK), :]
        hg = hg_ref[0, pl.ds(r0, CHUNK), :]

        f = lb + (1.0 - lb) * _sigmoid(hf)
        g = jnp.log(f) * LOG2E
        kk = 1.0 - f
        q = hq * _sigmoid(hq)
        g_hi = g.astype(BF16)
        g_rest = g - g_hi.astype(F32)
        g_mid = g_rest.astype(BF16)
        g_lo = (g_rest - g_mid.astype(F32)).astype(BF16)
        b3 = jnp.dot(tril, jnp.concatenate([g_hi, g_mid, g_lo], axis=1), preferred_element_type=F32)
        b = b3[:, :HG_WIDTH] + b3[:, HG_WIDTH:2 * HG_WIDTH] + b3[:, 2 * HG_WIDTH:]
        v_bf = v.astype(BF16)

        st = st_ref[...] if seqs_per_step == 1 else load_state(c)
        st_bf = st.astype(BF16)
        b_last = b[CHUNK - 1:CHUNK, :]
        q_dec = q * jnp.exp2(b)
        q_in = q_dec.astype(BF16)
        k_dec = (kk * jnp.exp2(b_last - b)).astype(BF16)
        o_inter = jnp.concatenate(
            [lax.dot_general(q_in[:, _head(h)], st_bf[:, _head(h)], nt, preferred_element_type=F32)
             for h in range(N_HEADS)], axis=1)
        upd = jnp.concatenate(
            [lax.dot_general(v_bf[:, _head(h)], k_dec[:, _head(h)], tn, preferred_element_type=F32)
             for h in range(N_HEADS)], axis=1)
        st_new = jnp.exp2(b_last) * st + upd
        if seqs_per_step == 1:
            st_ref[...] = st_new
        else:
            for h in range(N_HEADS):
                sout_ref[c, h] = st_new[:, _head(h)].T

        qt, kh, vh = [], [], []
        for i in range(1, N_SUB) if mode != "chunk" else ():
            lo = i * SUB
            b_start = b[lo - 1:lo, :]
            qt.append(q[lo:lo + SUB] * jnp.exp2(b[lo:lo + SUB] - b_start))
            kh.append(kk[:lo] * jnp.exp2(b_start - b[:lo]))
            vh.append(v[:lo])

        if mode == "chunk":
            k_inv = (kk * jnp.exp2(-b)).astype(BF16)
            a_all = [lax.dot_general(q_in[:, _head(h)], k_inv[:, _head(h)], nt, preferred_element_type=F32)
                     for h in range(N_HEADS)]
            o_intra = jnp.concatenate(
                [jnp.dot(jnp.where(causal, a_all[h], 0.0).astype(BF16), v_bf[:, _head(h)],
                         preferred_element_type=F32) for h in range(N_HEADS)], axis=1)
        elif mode == "sub_block":
            b0 = jnp.concatenate(
                [jnp.zeros((SUB, HG_WIDTH), F32)]
                + [jnp.broadcast_to(b[i * SUB - 1:i * SUB, :], (SUB, HG_WIDTH)) for i in range(1, N_SUB)], axis=0)
            kd = kk * jnp.exp2(b0 - b)
            q_all = jnp.concatenate([q_dec[:SUB]] + qt, axis=0).astype(BF16)
            k_all = jnp.concatenate(kh + [kd], axis=0).astype(BF16)
            v_all = jnp.concatenate(vh + [v], axis=0).astype(BF16)
            a_all = [lax.dot_general(q_all[:, _head(h)], k_all[:, _head(h)], nt, preferred_element_type=F32)
                     for h in range(N_HEADS)]
            o_intra = jnp.concatenate(
                [jnp.dot(jnp.where(all_mask, a_all[h], 0.0).astype(BF16), v_all[:, _head(h)],
                         preferred_element_type=F32) for h in range(N_HEADS)], axis=1)
        else:
            for h in range(N_HEADS):
                q_ref[h] = q[:, _head(h)]
                b_ref[h] = b[:, _head(h)]
            qt = jnp.concatenate(qt, axis=0).astype(BF16)
            kh = jnp.concatenate(kh, axis=0).astype(BF16)
            vh = jnp.concatenate(vh, axis=0).astype(BF16)
            a_off = [lax.dot_general(qt[:, _head(h)], kh[:, _head(h)], nt, preferred_element_type=F32)
                     for h in range(N_HEADS)]

            a_rep = []
            for j in range(N_HEADS // 2):
                p_pair = []
                for h in (2 * j, 2 * j + 1):
                    kk_h = kk[:, _head(h)]
                    b_h = b[:, _head(h)]
                    rows = []
                    for r in range(CHUNK):
                        lo = r - r % SUB
                        q_row = q_ref[h, pl.ds(r, SUB, stride=0), :]
                        b_row = b_ref[h, pl.ds(r, SUB, stride=0), :]
                        rows.append((q_row * kk_h[lo:lo + SUB])
                                    * jnp.exp2(jnp.minimum(b_row - b_h[lo:lo + SUB], 0.0)))
                    p_pair.append(jnp.concatenate(rows, axis=0).astype(BF16))
                a_rep.append(jnp.dot(jnp.concatenate(p_pair, axis=1), pair_ones,
                                     preferred_element_type=F32))

            o_off = jnp.concatenate(
                [jnp.dot(jnp.where(off_mask, a_off[h], 0.0).astype(BF16), vh[:, _head(h)],
                         preferred_element_type=F32) for h in range(N_HEADS)], axis=1)
            o_diag = []
            for j in range(N_HEADS // 2):
                pair = slice(2 * HEAD_W * j, 2 * HEAD_W * (j + 1))
                v_rep = jnp.broadcast_to(v[:, pair].reshape(N_SUB, 1, SUB, 2 * HEAD_W),
                                         (N_SUB, SUB, SUB, 2 * HEAD_W))
                w = (a_rep[j].reshape(N_SUB, SUB, SUB, 2 * HEAD_W) * v_rep).reshape(CHUNK * SUB, 2 * HEAD_W)
                o_diag.append(jnp.dot(seg_sum, w.astype(BF16), preferred_element_type=F32))
            o_intra = jnp.concatenate(o_diag, axis=1) + jnp.concatenate(
                [jnp.zeros((SUB, HG_WIDTH), F32), o_off], axis=0)
        o = o_inter + o_intra

        y = jnp.concatenate(
            [o[:, _head(h)] * lax.rsqrt(jnp.mean(o[:, _head(h)] * o[:, _head(h)], axis=-1, keepdims=True) + EPS)
             for h in range(N_HEADS)], axis=1)
        y = y * gn * (hg * _sigmoid(hg))
        o_ref[pl.ds(r0, CHUNK), :] = y.astype(BF16)
        return carry

    lb_min = jnp.min(lb)
    unroll = min(n_chunks, 4)

    @pl.when(lb_min > MIN_CHUNK_FACTORED_LB)
    def _():
        lax.fori_loop(0, n_chunks, functools.partial(chunk, mode="chunk"), 0, unroll=unroll)

    @pl.when((lb_min > MIN_FACTORED_LB) & (lb_min <= MIN_CHUNK_FACTORED_LB))
    def _():
        lax.fori_loop(0, n_chunks, functools.partial(chunk, mode="sub_block"), 0, unroll=unroll)

    @pl.when(lb_min <= MIN_FACTORED_LB)
    def _():
        lax.fori_loop(0, n_chunks, functools.partial(chunk, mode="pairwise"), 0)

    if seqs_per_step == 1:
        @pl.when(r == pl.num_programs(1) - 1)
        def _():
            st = st_ref[...]
            for h in range(N_HEADS):
                sout_ref[0, h] = st[:, _head(h)].T


def _gla(gates, lower_bounds, hg_norm, s0, row0, n_seq, seq_len, rows, seqs_per_step=1):
    m = n_seq * seq_len
    assert seqs_per_step == 1 or (seq_len == CHUNK and rows == seqs_per_step * CHUNK and n_seq % seqs_per_step == 0)
    nr = max(seq_len // rows, 1)
    blk0 = row0 // rows
    seg = lambda s: pl.BlockSpec((1, rows, HG_WIDTH), lambda b, r: (s, blk0 + b * nr + r, 0))
    state = pl.BlockSpec((seqs_per_step, N_HEADS, HEAD_W, HEAD_W), lambda b, r: (b, 0, 0, 0))
    return pl.pallas_call(
        functools.partial(_gla_kernel, n_chunks=rows // CHUNK, seqs_per_step=seqs_per_step),
        grid=(n_seq // seqs_per_step, nr),
        in_specs=[
            seg(0), seg(1), seg(2), seg(3),
            pl.BlockSpec((lower_bounds.shape[0], HG_WIDTH), lambda b, r: (0, 0)),
            pl.BlockSpec((1, HEAD_W), lambda b, r: (0, 0)),
            state,
        ],
        out_specs=[pl.BlockSpec((rows, HG_WIDTH), lambda b, r: (b * nr + r, 0)), state],
        out_shape=[
            jax.ShapeDtypeStruct((m, HG_WIDTH), BF16),
            jax.ShapeDtypeStruct((n_seq, N_HEADS, HEAD_W, HEAD_W), F32),
        ],
        scratch_shapes=[pltpu.VMEM((HEAD_W, HG_WIDTH), F32),
                        pltpu.VMEM((N_HEADS, CHUNK, HEAD_W), F32),
                        pltpu.VMEM((N_HEADS, CHUNK, HEAD_W), F32)],
        compiler_params=pltpu.CompilerParams(
            dimension_semantics=("arbitrary", "arbitrary"), vmem_limit_bytes=VMEM_LIMIT),
        name="hgrn2",
    )(gates, gates, gates, gates, lower_bounds, hg_norm, s0)


def _stacked_query(q):
    lane = lax.broadcasted_iota(jnp.int32, q.shape, 1)
    qbig = jnp.concatenate([jnp.where(lane < ATT_DH, q, 0.0), jnp.where(lane >= ATT_DH, q, 0.0)], axis=0)
    return qbig.T.astype(BF16)


def _lambda(lq1_ref, lk1_ref, lq2_ref, lk2_ref):
    s1 = jnp.sum(lq1_ref[...] * lk1_ref[...], axis=-1, keepdims=True)
    s2 = jnp.sum(lq2_ref[...] * lk2_ref[...], axis=-1, keepdims=True)
    return jnp.exp(s1) - jnp.exp(s2) + LAM_INIT


def _attn_finish(acc, l, lam, sub, n):
    o_both = (acc * (1.0 / l)).T
    o = o_both[:n] - lam * o_both[n:]
    return _rmsnorm_rows(o, sub) * (1.0 - LAM_INIT)


def _attn_prompt_kernel(q_ref, qn_ref, k_ref, vt_ref, lq1_ref, lk1_ref, lq2_ref, lk2_ref, sub_ref, o_ref,
                        qq_ref, s_ref, smax_ref, mask_ref, m_ref, acc_ref, *, t, n_grp):
    qi = pl.program_id(1)
    grp = range(n_grp)
    slot = qi % 2
    m_ref[...] = jnp.full(m_ref.shape, NEG_INF, F32)
    acc_ref[...] = jnp.zeros(acc_ref.shape, F32)
    ones_rows = jnp.ones((ONES_ROWS, t), BF16)

    def values_t(g, kv):
        k0 = pl.multiple_of(kv * t, t)
        return jnp.concatenate([vt_ref[_head(g), pl.ds(k0, t)], ones_rows], axis=0)

    def first_scores(g, qslot, cols):
        s = jnp.dot(k_ref[0:t, _head(g)], qq_ref[qslot, g, :, cols], preferred_element_type=F32)
        s_ref[g, 0, :, cols] = s
        smax_ref[g, 0, :, cols] = jnp.max(s, axis=0, keepdims=True)

    def stage(kv, cur, oth):
        k_next = pl.multiple_of((kv + 1) * t, t)
        vt = [values_t(g, kv) for g in grp]
        for c in range(0, 2 * t, STAGE_COLS):
            cols = slice(c, c + STAGE_COLS)
            for g in grp:
                s_next = jnp.dot(k_ref[pl.ds(k_next, t), _head(g)], qq_ref[slot, g, :, cols],
                                 preferred_element_type=F32)
                s_ref[g, oth, :, cols] = s_next
                smax_ref[g, oth, :, cols] = jnp.max(s_next, axis=0, keepdims=True)
                m_prev = m_ref[g, :, cols]
                m_new = jnp.maximum(m_prev, smax_ref[g, cur, :, cols])
                m_ref[g, :, cols] = m_new
                p = jnp.exp2(s_ref[g, cur, :, cols] - m_new).astype(BF16)
                acc_ref[g, :, cols] = (jnp.exp2(m_prev - m_new) * acc_ref[g, :, cols]
                                       + jnp.dot(vt[g], p, preferred_element_type=F32))

    @pl.when((pl.program_id(0) == 0) & (qi == 0))
    def _():
        kpos = lax.broadcasted_iota(jnp.int32, mask_ref.shape, 0)
        col = lax.broadcasted_iota(jnp.int32, mask_ref.shape, 1)
        qpos = jnp.where(col >= t, col - t, col)
        mask_ref[...] = jnp.where((kpos >> 6) <= (qpos >> 6), 0.0, NEG_INF)

    def finish(cur):
        for g in grp:
            qq_ref[1 - slot, g] = _stacked_query(qn_ref[:, _head(g)].astype(F32))
        vt = [values_t(g, qi) for g in grp]
        acc = [[] for _ in grp]
        for c in range(0, 2 * t, FINISH_COLS):
            cols = slice(c, c + FINISH_COLS)
            for g in grp:
                s = s_ref[g, cur, :, cols] + mask_ref[:, cols]
                m_prev = m_ref[g, :, cols]
                m_new = jnp.maximum(m_prev, jnp.max(s, axis=0, keepdims=True))
                p = jnp.exp2(s - m_new).astype(BF16)
                acc[g].append(jnp.exp2(m_prev - m_new) * acc_ref[g, :, cols]
                              + jnp.dot(vt[g], p, preferred_element_type=F32))
                first_scores(g, 1 - slot, cols)
        lam = _lambda(lq1_ref, lk1_ref, lq2_ref, lk2_ref)
        for g in grp:
            a = jnp.concatenate(acc[g], axis=1)
            o_ref[:, _head(g)] = _attn_finish(a[:HEAD_W], a[HEAD_W:HEAD_W + 1], lam, sub_ref[...],
                                              t).astype(BF16)

    @pl.when(qi == 0)
    def _():
        for g in grp:
            qq_ref[0, g] = _stacked_query(q_ref[:, _head(g)].astype(F32))
        for c in range(0, 2 * t, FINISH_COLS):
            for g in grp:
                first_scores(g, 0, slice(c, c + FINISH_COLS))

    def pair(j, carry):
        stage(2 * j, 0, 1)
        stage(2 * j + 1, 1, 0)
        return carry

    lax.fori_loop(0, qi // 2, pair, 0)

    @pl.when(qi % 2 == 1)
    def _():
        stage(qi - 1, 0, 1)
        finish(1)

    @pl.when(qi % 2 == 0)
    def _():
        finish(0)


def _attn_prompt(q_bf, k_bf, vt_bf, lams, subln, n, t, n_grp):
    w = n_grp * HEAD_W
    small = lambda shape: pl.BlockSpec(shape, lambda h, i: (0, 0))
    return pl.pallas_call(
        functools.partial(_attn_prompt_kernel, t=t, n_grp=n_grp),
        grid=(N_HEADS // n_grp, n // t),
        in_specs=[
            pl.BlockSpec((t, w), lambda h, i: (i, h)),
            pl.BlockSpec((t, w), lambda h, i: (jnp.minimum(i + 1, n // t - 1), h)),
            pl.BlockSpec((n, w), lambda h, i: (0, h)),
            pl.BlockSpec((w, n), lambda h, i: (h, 0)),
            small((1, ATT_DH)), small((1, ATT_DH)), small((1, ATT_DH)), small((1, ATT_DH)),
            small((1, HEAD_W)),
        ],
        out_specs=pl.BlockSpec((t, w), lambda h, i: (i, h)),
        out_shape=jax.ShapeDtypeStruct((n, ATT_WIDTH), BF16),
        scratch_shapes=[
            pltpu.VMEM((2, n_grp, HEAD_W, 2 * t), BF16),
            pltpu.VMEM((n_grp, 2, t, 2 * t), F32),
            pltpu.VMEM((n_grp, 2, 1, 2 * t), F32),
            pltpu.VMEM((t, 2 * t), F32),
            pltpu.VMEM((n_grp, 1, 2 * t), F32),
            pltpu.VMEM((n_grp, HEAD_W + ONES_ROWS, 2 * t), F32),
        ],
        compiler_params=pltpu.CompilerParams(
            dimension_semantics=("arbitrary", "arbitrary"), vmem_limit_bytes=VMEM_LIMIT),
        name="attn_prompt",
    )(q_bf, q_bf, k_bf, vt_bf, *lams, subln)


def _attn_sample_kernel(q_ref, kc_ref, vc_ref, kn_ref, vn_ref, lq1_ref, lk1_ref, lq2_ref, lk2_ref,
                        sub_ref, o_ref, qq_ref, m_ref, l_ref, acc_ref, *, n_q, tk):
    t = pl.program_id(1)
    pairs = range(N_HEADS // 2)
    w2 = 2 * HEAD_W

    @pl.when(t == 0)
    def _():
        zero = jnp.zeros((HEAD_W, 2 * n_q), BF16)
        for j in pairs:
            qa = _stacked_query(q_ref[:, _head(2 * j)].astype(F32))
            qb = _stacked_query(q_ref[:, _head(2 * j + 1)].astype(F32))
            qq_ref[j] = jnp.concatenate([jnp.concatenate([qa, zero], axis=1),
                                         jnp.concatenate([zero, qb], axis=1)], axis=0)
        m_ref[...] = jnp.full(m_ref.shape, NEG_INF, F32)
        l_ref[...] = jnp.zeros(l_ref.shape, F32)
        acc_ref[...] = jnp.zeros(acc_ref.shape, F32)

    def update(k, v):
        s = [jnp.dot(k[j], qq_ref[j], preferred_element_type=F32) for j in pairs]
        p, alpha = [], []
        for j in pairs:
            m_prev = m_ref[j]
            m_new = jnp.maximum(m_prev, jnp.max(s[j], axis=0, keepdims=True))
            m_ref[j] = m_new
            a = jnp.exp2(m_prev - m_new)
            pj = jnp.exp2(s[j] - m_new)
            l_ref[j] = a * l_ref[j] + jnp.sum(pj, axis=0, keepdims=True)
            p.append(pj.astype(BF16))
            alpha.append(a)
        pv = [lax.dot_general(v[j], p[j], (((0,), (0,)), ((), ())), preferred_element_type=F32)
              for j in pairs]
        for j in pairs:
            acc_ref[j] = alpha[j] * acc_ref[j] + pv[j]

    def cache_pair(ref, j):
        return jnp.concatenate([ref[0, pl.ds(2 * j, tk, stride=N_HEADS), :],
                                ref[0, pl.ds(2 * j + 1, tk, stride=N_HEADS), :]], axis=1).astype(BF16)

    update([cache_pair(kc_ref, j) for j in pairs], [cache_pair(vc_ref, j) for j in pairs])

    @pl.when(t == pl.num_programs(1) - 1)
    def _():
        update([kn_ref[:, w2 * j:w2 * (j + 1)] for j in pairs], [vn_ref[:, w2 * j:w2 * (j + 1)] for j in pairs])
        lam = _lambda(lq1_ref, lk1_ref, lq2_ref, lk2_ref)
        for j in pairs:
            acc = acc_ref[j]
            l = l_ref[j]
            for i, h in enumerate((2 * j, 2 * j + 1)):
                o_ref[:, _head(h)] = _attn_finish(acc[_head(i), _head(i)], l[:, _head(i)], lam, sub_ref[...],
                                                  n_q).astype(BF16)


def _attn_sample(q_bf, cache_k, cache_v, kn_bf, vn_bf, lams, subln, row0, n_q, tk):
    n_b = cache_k.shape[0]
    past = cache_k.shape[1] // N_HEADS
    assert past % CHUNK == 0 and n_q <= CHUNK and past % tk == 0 and row0 % n_q == 0
    blk0 = row0 // n_q
    small = lambda shape: pl.BlockSpec(shape, lambda b, t: (0, 0))
    rows = pl.BlockSpec((n_q, ATT_WIDTH), lambda b, t: (b, 0))
    rows_all = pl.BlockSpec((n_q, ATT_WIDTH), lambda b, t: (blk0 + b, 0))
    cache = pl.BlockSpec((1, tk * N_HEADS, HEAD_W), lambda b, t: (b, t, 0))
    return pl.pallas_call(
        functools.partial(_attn_sample_kernel, n_q=n_q, tk=tk),
        grid=(n_b, past // tk),
        in_specs=[
            rows_all, cache, cache, rows_all, rows,
            small((1, ATT_DH)), small((1, ATT_DH)), small((1, ATT_DH)), small((1, ATT_DH)),
            small((1, HEAD_W)),
        ],
        out_specs=rows,
        out_shape=jax.ShapeDtypeStruct((n_b * n_q, ATT_WIDTH), BF16),
        scratch_shapes=[
            pltpu.VMEM((N_HEADS // 2, 2 * HEAD_W, 4 * n_q), BF16),
            pltpu.VMEM((N_HEADS // 2, 1, 4 * n_q), F32),
            pltpu.VMEM((N_HEADS // 2, 1, 4 * n_q), F32),
            pltpu.VMEM((N_HEADS // 2, 2 * HEAD_W, 4 * n_q), F32),
        ],
        compiler_params=pltpu.CompilerParams(
            dimension_semantics=("arbitrary", "arbitrary"), vmem_limit_bytes=VMEM_LIMIT),
        name="attn_sample",
    )(q_bf, cache_k, cache_v, kn_bf, vn_bf, *lams, subln)


def _outproj_kernel(x_ref, a_ref, b_ref, w_ref, o_ref, wb_ref):
    @pl.when(pl.program_id(0) == 0)
    def _():
        wb_ref[...] = w_ref[...].astype(BF16)

    o_ref[...] = (x_ref[...]
                  + jnp.dot(a_ref[...], wb_ref[:HG_WIDTH, :], preferred_element_type=F32)
                  + jnp.dot(b_ref[...], wb_ref[HG_WIDTH:, :], preferred_element_type=F32))


def _outproj(x, mix_hg, mix_at, w, tm):
    m = x.shape[0]
    row = lambda i: (i, 0)
    return pl.pallas_call(
        _outproj_kernel,
        grid=(m // tm,),
        in_specs=[
            pl.BlockSpec((tm, D_MODEL), row),
            pl.BlockSpec((tm, HG_WIDTH), row),
            pl.BlockSpec((tm, ATT_WIDTH), row),
            pl.BlockSpec((HG_WIDTH + ATT_WIDTH, D_MODEL), lambda i: (0, 0), pipeline_mode=pl.Buffered(1)),
        ],
        out_specs=pl.BlockSpec((tm, D_MODEL), row),
        out_shape=jax.ShapeDtypeStruct((m, D_MODEL), F32),
        scratch_shapes=[pltpu.VMEM((HG_WIDTH + ATT_WIDTH, D_MODEL), BF16)],
        compiler_params=pltpu.CompilerParams(
            dimension_semantics=("arbitrary",), vmem_limit_bytes=VMEM_LIMIT),
        name="outproj",
    )(x, mix_hg, mix_at, w)


def _mlp_kernel(x_ref, gain_ref, wu_ref, wd_ref, gfin_ref, o_ref, h_ref):
    j = pl.program_id(1)

    @pl.when(j == 0)
    def _():
        x = x_ref[...]
        h_ref[...] = _rmsnorm_rows(x, gain_ref[...]).astype(BF16)
        o_ref[...] = x

    u = jnp.dot(h_ref[...], wu_ref[...].astype(BF16), preferred_element_type=F32)
    u = jnp.square(jnp.maximum(u, 0.0)).astype(BF16)
    o_ref[...] += jnp.dot(u, wd_ref[...].astype(BF16), preferred_element_type=F32)

    @pl.when(j == pl.num_programs(1) - 1)
    def _():
        o_ref[...] = _rmsnorm_rows(o_ref[...], gfin_ref[...])


def _mlp(x, gain, wu, wd, gfin, tm, tf):
    m = x.shape[0]
    return pl.pallas_call(
        _mlp_kernel,
        grid=(m // tm, D_FF // tf),
        in_specs=[
            pl.BlockSpec((tm, D_MODEL), lambda i, j: (i, 0)),
            pl.BlockSpec((1, D_MODEL), lambda i, j: (0, 0)),
            pl.BlockSpec((D_MODEL, tf), lambda i, j: (0, j)),
            pl.BlockSpec((tf, D_MODEL), lambda i, j: (j, 0)),
            pl.BlockSpec((1, D_MODEL), lambda i, j: (0, 0)),
        ],
        out_specs=pl.BlockSpec((tm, D_MODEL), lambda i, j: (i, 0)),
        out_shape=jax.ShapeDtypeStruct((m, D_MODEL), F32),
        scratch_shapes=[pltpu.VMEM((tm, D_MODEL), BF16)],
        compiler_params=pltpu.CompilerParams(
            dimension_semantics=("arbitrary", "arbitrary"), vmem_limit_bytes=VMEM_LIMIT),
        name="mlp",
    )(x, gain, wu, wd, gfin)


def kernel(x_prompt, x_sample, cache_k, cache_v, state_hgrn, norm_attn, w_in, lower_bounds, hg_norm,
           lambda_q1, lambda_k1, lambda_q2, lambda_k2, subln, w_out, norm_mlp, w_up, w_down, norm_final):
    depth = w_in.shape[0]
    assert depth == 1
    n_pb, n_p, _ = x_prompt.shape
    n_sb, n_s, _ = x_sample.shape
    assert n_pb == 1
    past = cache_k.shape[2]

    w_in_bf = w_in[0].astype(BF16)
    lams = (lambda_q1, lambda_k1, lambda_q2, lambda_k2)
    gfin = norm_final.reshape(1, D_MODEL)

    xp = x_prompt.reshape(n_p, D_MODEL)
    xs = x_sample.reshape(n_sb * n_s, D_MODEL)

    gates, q_bf, k_bf, vt_bf, vs_bf, kf_p, vf_p, kf_s, vf_s = _inproj(xp, xs, norm_attn, w_in_bf, tm=256)

    s0_p = jnp.zeros((1, N_HEADS, HEAD_W, HEAD_W), F32)
    ohg_p, st_p = _gla(gates, lower_bounds, hg_norm, s0_p, row0=0, n_seq=1, seq_len=n_p, rows=512)
    ohg_s, st_s = _gla(gates, lower_bounds, hg_norm, state_hgrn[0], row0=n_p, n_seq=n_sb, seq_len=n_s,
                       rows=4 * n_s, seqs_per_step=4)

    oat_p = _attn_prompt(q_bf, k_bf, vt_bf, lams, subln, n=n_p, t=512, n_grp=2)
    ck = cache_k[0].reshape(n_sb, past * N_HEADS, HEAD_W)
    cv = cache_v[0].reshape(n_sb, past * N_HEADS, HEAD_W)
    oat_s = _attn_sample(q_bf, ck, cv, k_bf, vs_bf, lams, subln, row0=n_p, n_q=n_s, tk=2048)

    x1_p = _outproj(xp, ohg_p, oat_p, w_out[0], tm=512)
    x1_s = _outproj(xs, ohg_s, oat_s, w_out[0], tm=512)

    y_p = _mlp(x1_p, norm_mlp, w_up[0], w_down[0], gfin, tm=1024, tf=512)
    y_s = _mlp(x1_s, norm_mlp, w_up[0], w_down[0], gfin, tm=512, tf=1024)

    return (
        y_p.reshape(n_pb, n_p, D_MODEL),
        y_s.reshape(n_sb, n_s, D_MODEL),
        kf_p.reshape(1, n_pb, n_p, N_HEADS, HEAD_W),
        vf_p.reshape(1, n_pb, n_p, N_HEADS, HEAD_W),
        st_p.reshape(1, n_pb, N_HEADS, HEAD_W, HEAD_W),
        kf_s.reshape(1, n_sb, n_s, N_HEADS, HEAD_W),
        vf_s.reshape(1, n_sb, n_s, N_HEADS, HEAD_W),
        st_s.reshape(1, n_sb, N_HEADS, HEAD_W, HEAD_W),
    )
```

```python
import functools
import math

import jax
import jax.numpy as jnp
from jax import lax
from jax.experimental import pallas as pl
from jax.experimental.pallas import tpu as pltpu

F32 = jnp.float32
BF16 = jnp.bfloat16

D_MODEL = 2048
HG_WIDTH = 1024
ATT_WIDTH = 1024
N_HEADS = 8
HEAD_W = 128
ATT_DH = 64
CHUNK = 64
SUB = 8
N_SUB = CHUNK // SUB
MIN_FACTORED_LB = 2.0 ** (-100.0 / SUB)
MIN_CHUNK_FACTORED_LB = 2.0 ** (-100.0 / CHUNK)
D_FF = 4 * D_MODEL
N_SEG = 7
EPS = 1e-6
NEG_INF = -1e30
LOG2E = 1.4426950408889634
Q_SCALE = ATT_DH ** -0.5 * LOG2E
LAM_INIT = 0.8 - 0.6 * math.exp(-0.3 * 0)
ONES_ROWS = 16
FINISH_COLS = 256
STAGE_COLS = 256

VMEM_LIMIT = 56 * 1024 * 1024


def _rmsnorm_rows(x, gain):
    return x * lax.rsqrt(jnp.mean(x * x, axis=-1, keepdims=True) + EPS) * gain


def _sigmoid(x):
    return 0.5 * jnp.tanh(0.5 * x) + 0.5


def _head(h):
    return slice(h * HEAD_W, (h + 1) * HEAD_W)


def _inproj_kernel(xp_ref, xs_ref, gain_ref, w_ref, gates_ref, q_ref, kb_ref, vt_ref, vbs_ref,
                   kfp_ref, vfp_ref, kfs_ref, vfs_ref, wb_ref, *, n_p, n_w):
    i = pl.program_id(0)

    w_rows = w_ref.shape[0]
    per_seg = D_MODEL // w_rows

    @pl.when(i < n_w)
    def _():
        j = jnp.minimum(i, n_w - 1)
        r0 = pl.multiple_of((j % per_seg) * w_rows, w_rows)
        wb_ref[j // per_seg, pl.ds(r0, w_rows), :] = w_ref[...].astype(BF16)

    def project(x_ref, kf_ref, vf_ref, vb_ref):
        tm = x_ref.shape[0]
        h = _rmsnorm_rows(x_ref[...], gain_ref[...]).astype(BF16)

        def segment(s):
            return jnp.dot(h, wb_ref[s], preferred_element_type=F32)

        k = segment(5)
        v = segment(6)
        kb_ref[...] = k.astype(BF16)
        vt_ref[...] = v.T.astype(BF16)
        if vb_ref is not None:
            vb_ref[...] = v.astype(BF16)
        for hd in range(N_HEADS):
            kf_ref[pl.ds(hd, tm, stride=N_HEADS), :] = k[:, _head(hd)]
            vf_ref[pl.ds(hd, tm, stride=N_HEADS), :] = v[:, _head(hd)]
        q_ref[...] = (segment(4) * Q_SCALE).astype(BF16)
        for s in range(4):
            gates_ref[s] = segment(s)

    @pl.when((i >= n_w) & (i < n_w + n_p))
    def _():
        project(xp_ref, kfp_ref, vfp_ref, None)

    @pl.when(i >= n_w + n_p)
    def _():
        project(xs_ref, kfs_ref, vfs_ref, vbs_ref)


def _inproj(xp, xs, gain, w, tm):
    m_p, m_s = xp.shape[0], xs.shape[0]
    n_p, n_s = m_p // tm, m_s // tm
    m = m_p + m_s
    seg = HG_WIDTH
    w_rows = 512
    per_seg = D_MODEL // w_rows
    n_w = N_SEG * per_seg

    def w_block(i):
        j = jnp.minimum(i, n_w - 1)
        return (j % per_seg, j // per_seg)

    tile = lambda i: jnp.maximum(i - n_w, 0)
    row = lambda i: (tile(i), 0)
    p_row = lambda i: (jnp.minimum(tile(i), n_p - 1), 0)
    s_row = lambda i: (jnp.maximum(tile(i) - n_p, 0), 0)
    once = dict(pipeline_mode=pl.Buffered(1))
    return pl.pallas_call(
        functools.partial(_inproj_kernel, n_p=n_p, n_w=n_w),
        grid=(n_w + n_p + n_s,),
        in_specs=[
            pl.BlockSpec((tm, D_MODEL), p_row),
            pl.BlockSpec((tm, D_MODEL), s_row, **once),
            pl.BlockSpec((1, D_MODEL), lambda i: (0, 0)),
            pl.BlockSpec((w_rows, seg), w_block),
        ],
        out_specs=[
            pl.BlockSpec((4, tm, seg), lambda i: (0, tile(i), 0)),
            pl.BlockSpec((tm, seg), row),
            pl.BlockSpec((tm, seg), row),
            pl.BlockSpec((seg, tm), lambda i: (0, tile(i))),
            pl.BlockSpec((tm, seg), s_row, **once),
            pl.BlockSpec((tm * N_HEADS, HEAD_W), p_row),
            pl.BlockSpec((tm * N_HEADS, HEAD_W), p_row),
            pl.BlockSpec((tm * N_HEADS, HEAD_W), s_row, **once),
            pl.BlockSpec((tm * N_HEADS, HEAD_W), s_row, **once),
        ],
        out_shape=[
            jax.ShapeDtypeStruct((4, m, seg), F32),
            jax.ShapeDtypeStruct((m, seg), BF16),
            jax.ShapeDtypeStruct((m, seg), BF16),
            jax.ShapeDtypeStruct((seg, m), BF16),
            jax.ShapeDtypeStruct((m_s, seg), BF16),
            jax.ShapeDtypeStruct((m_p * N_HEADS, HEAD_W), F32),
            jax.ShapeDtypeStruct((m_p * N_HEADS, HEAD_W), F32),
            jax.ShapeDtypeStruct((m_s * N_HEADS, HEAD_W), F32),
            jax.ShapeDtypeStruct((m_s * N_HEADS, HEAD_W), F32),
        ],
        scratch_shapes=[pltpu.VMEM((N_SEG, D_MODEL, seg), BF16)],
        compiler_params=pltpu.CompilerParams(
            dimension_semantics=("arbitrary",), vmem_limit_bytes=60 * 1024 * 1024),
        name="inproj",
    )(xp, xs, gain, w)


def _gla_kernel(hq_ref, hf_ref, hi_ref, hg_ref, lbnd_ref, gn_ref, s0_ref, o_ref, sout_ref, st_ref,
                q_ref, b_ref, *, n_chunks):
    r = pl.program_id(1)

    @pl.when(r == 0)
    def _():
        st_ref[...] = jnp.concatenate([s0_ref[0, h].T for h in range(N_HEADS)], axis=1)

    lbs = lbnd_ref[...]
    e = jnp.exp(lbs - jnp.max(lbs, axis=0, keepdims=True))
    lb = e[0:1, :] / jnp.sum(e, axis=0, keepdims=True)
    gn = jnp.concatenate([gn_ref[...]] * N_HEADS, axis=1)

    ri = lax.broadcasted_iota(jnp.int32, (CHUNK, CHUNK), 0)
    ci = lax.broadcasted_iota(jnp.int32, (CHUNK, CHUNK), 1)
    tril = (ri >= ci).astype(BF16)
    pr = lax.broadcasted_iota(jnp.int32, (2 * HEAD_W, 2 * HEAD_W), 0)
    pc = lax.broadcasted_iota(jnp.int32, (2 * HEAD_W, 2 * HEAD_W), 1)
    pair_ones = ((pr >> 7) == (pc >> 7)).astype(BF16)
    gr = lax.broadcasted_iota(jnp.int32, (CHUNK, CHUNK * SUB), 0)
    gc = lax.broadcasted_iota(jnp.int32, (CHUNK, CHUNK * SUB), 1)
    seg_sum = (((gc >> 3) == gr) & ((gc & (SUB - 1)) <= (gr & (SUB - 1)))).astype(BF16)
    n_off = (N_SUB - 1) * SUB
    n_key = SUB * (N_SUB * (N_SUB - 1) // 2)
    orow = lax.broadcasted_iota(jnp.int32, (n_off, n_key), 0) >> 3
    ocol = lax.broadcasted_iota(jnp.int32, (n_off, n_key), 1)
    ocol_seg = sum((ocol >= 4 * i * (i - 1)).astype(jnp.int32) for i in range(2, N_SUB))
    off_mask = orow == ocol_seg
    nt = (((1,), (1,)), ((), ()))
    tn = (((0,), (0,)), ((), ()))

    n_all = n_key + CHUNK
    frow = lax.broadcasted_iota(jnp.int32, (CHUNK, n_all), 0)
    fcol = lax.broadcasted_iota(jnp.int32, (CHUNK, n_all), 1)
    fseg = sum((fcol >= 4 * i * (i - 1)).astype(jnp.int32) for i in range(2, N_SUB))
    fsame = fcol - n_key
    all_mask = (((fcol < n_key) & (fseg + 1 == (frow >> 3)))
                | ((fcol >= n_key) & ((fsame >> 3) == (frow >> 3)) & ((fsame & (SUB - 1)) <= (frow & (SUB - 1)))))

    causal = ri >= ci

    def chunk(c, carry, mode):
        r0 = pl.multiple_of(c * CHUNK, CHUNK)
        hq = hq_ref[0, pl.ds(r0, CHUNK), :]
        hf = hf_ref[0, pl.ds(r0, CHUNK), :]
        v = hi_ref[0, pl.ds(r0, CHUNK), :]
        hg = hg_ref[0, pl.ds(r0, CHUNK), :]

        f = lb + (1.0 - lb) * _sigmoid(hf)
        g = jnp.log(f) * LOG2E
        kk = 1.0 - f
        q = hq * _sigmoid(hq)
        g_hi = g.astype(BF16)
        g_rest = g - g_hi.astype(F32)
        g_mid = g_rest.astype(BF16)
        g_lo = (g_rest - g_mid.astype(F32)).astype(BF16)
        b3 = jnp.dot(tril, jnp.concatenate([g_hi, g_mid, g_lo], axis=1), preferred_element_type=F32)
        b = b3[:, :HG_WIDTH] + b3[:, HG_WIDTH:2 * HG_WIDTH] + b3[:, 2 * HG_WIDTH:]
        v_bf = v.astype(BF16)

        st = st_ref[...]
        st_bf = st.astype(BF16)
        b_last = b[CHUNK - 1:CHUNK, :]
        q_dec = q * jnp.exp2(b)
        q_in = q_dec.astype(BF16)
        k_dec = (kk * jnp.exp2(b_last - b)).astype(BF16)
        o_inter = jnp.concatenate(
            [lax.dot_general(q_in[:, _head(h)], st_bf[:, _head(h)], nt, preferred_element_type=F32)
             for h in range(N_HEADS)], axis=1)
        upd = jnp.concatenate(
            [lax.dot_general(v_bf[:, _head(h)], k_dec[:, _head(h)], tn, preferred_element_type=F32)
             for h in range(N_HEADS)], axis=1)
        st_ref[...] = jnp.exp2(b_last) * st + upd

        qt, kh, vh = [], [], []
        for i in range(1, N_SUB) if mode != "chunk" else ():
            lo = i * SUB
            b_start = b[lo - 1:lo, :]
            qt.append(q[lo:lo + SUB] * jnp.exp2(b[lo:lo + SUB] - b_start))
            kh.append(kk[:lo] * jnp.exp2(b_start - b[:lo]))
            vh.append(v[:lo])

        if mode == "chunk":
            k_inv = (kk * jnp.exp2(-b)).astype(BF16)
            a_all = [lax.dot_general(q_in[:, _head(h)], k_inv[:, _head(h)], nt, preferred_element_type=F32)
                     for h in range(N_HEADS)]
            o_intra = jnp.concatenate(
                [jnp.dot(jnp.where(causal, a_all[h], 0.0).astype(BF16), v_bf[:, _head(h)],
                         preferred_element_type=F32) for h in range(N_HEADS)], axis=1)
        elif mode == "sub_block":
            b0 = jnp.concatenate(
                [jnp.zeros((SUB, HG_WIDTH), F32)]
                + [jnp.broadcast_to(b[i * SUB - 1:i * SUB, :], (SUB, HG_WIDTH)) for i in range(1, N_SUB)], axis=0)
            kd = kk * jnp.exp2(b0 - b)
            q_all = jnp.concatenate([q_dec[:SUB]] + qt, axis=0).astype(BF16)
            k_all = jnp.concatenate(kh + [kd], axis=0).astype(BF16)
            v_all = jnp.concatenate(vh + [v], axis=0).astype(BF16)
            a_all = [lax.dot_general(q_all[:, _head(h)], k_all[:, _head(h)], nt, preferred_element_type=F32)
                     for h in range(N_HEADS)]
            o_intra = jnp.concatenate(
                [jnp.dot(jnp.where(all_mask, a_all[h], 0.0).astype(BF16), v_all[:, _head(h)],
                         preferred_element_type=F32) for h in range(N_HEADS)], axis=1)
        else:
            for h in range(N_HEADS):
                q_ref[h] = q[:, _head(h)]
                b_ref[h] = b[:, _head(h)]
            qt = jnp.concatenate(qt, axis=0).astype(BF16)
            kh = jnp.concatenate(kh, axis=0).astype(BF16)
            vh = jnp.concatenate(vh, axis=0).astype(BF16)
            a_off = [lax.dot_general(qt[:, _head(h)], kh[:, _head(h)], nt, preferred_element_type=F32)
                     for h in range(N_HEADS)]

            a_rep = []
            for j in range(N_HEADS // 2):
                p_pair = []
                for h in (2 * j, 2 * j + 1):
                    kk_h = kk[:, _head(h)]
                    b_h = b[:, _head(h)]
                    rows = []
                    for r in range(CHUNK):
                        lo = r - r % SUB
                        q_row = q_ref[h, pl.ds(r, SUB, stride=0), :]
                        b_row = b_ref[h, pl.ds(r, SUB, stride=0), :]
                        rows.append((q_row * kk_h[lo:lo + SUB])
                                    * jnp.exp2(jnp.minimum(b_row - b_h[lo:lo + SUB], 0.0)))
                    p_pair.append(jnp.concatenate(rows, axis=0).astype(BF16))
                a_rep.append(jnp.dot(jnp.concatenate(p_pair, axis=1), pair_ones,
                                     preferred_element_type=F32))

            o_off = jnp.concatenate(
                [jnp.dot(jnp.where(off_mask, a_off[h], 0.0).astype(BF16), vh[:, _head(h)],
                         preferred_element_type=F32) for h in range(N_HEADS)], axis=1)
            o_diag = []
            for j in range(N_HEADS // 2):
                pair = slice(2 * HEAD_W * j, 2 * HEAD_W * (j + 1))
                v_rep = jnp.broadcast_to(v[:, pair].reshape(N_SUB, 1, SUB, 2 * HEAD_W),
                                         (N_SUB, SUB, SUB, 2 * HEAD_W))
                w = (a_rep[j].reshape(N_SUB, SUB, SUB, 2 * HEAD_W) * v_rep).reshape(CHUNK * SUB, 2 * HEAD_W)
                o_diag.append(jnp.dot(seg_sum, w.astype(BF16), preferred_element_type=F32))
            o_intra = jnp.concatenate(o_diag, axis=1) + jnp.concatenate(
                [jnp.zeros((SUB, HG_WIDTH), F32), o_off], axis=0)
        o = o_inter + o_intra

        y = jnp.concatenate(
            [o[:, _head(h)] * lax.rsqrt(jnp.mean(o[:, _head(h)] * o[:, _head(h)], axis=-1, keepdims=True) + EPS)
             for h in range(N_HEADS)], axis=1)
        y = y * gn * (hg * _sigmoid(hg))
        o_ref[pl.ds(r0, CHUNK), :] = y.astype(BF16)
        return carry

    lb_min = jnp.min(lb)
    unroll = min(n_chunks, 4)

    @pl.when(lb_min > MIN_CHUNK_FACTORED_LB)
    def _():
        lax.fori_loop(0, n_chunks, functools.partial(chunk, mode="chunk"), 0, unroll=unroll)

    @pl.when((lb_min > MIN_FACTORED_LB) & (lb_min <= MIN_CHUNK_FACTORED_LB))
    def _():
        lax.fori_loop(0, n_chunks, functools.partial(chunk, mode="sub_block"), 0, unroll=unroll)

    @pl.when(lb_min <= MIN_FACTORED_LB)
    def _():
        lax.fori_loop(0, n_chunks, functools.partial(chunk, mode="pairwise"), 0)

    @pl.when(r == pl.num_programs(1) - 1)
    def _():
        st = st_ref[...]
        for h in range(N_HEADS):
            sout_ref[0, h] = st[:, _head(h)].T


def _gla(gates, lower_bounds, hg_norm, s0, row0, n_seq, seq_len, rows):
    m = n_seq * seq_len
    nr = seq_len // rows
    blk0 = row0 // rows
    seg = lambda s: pl.BlockSpec((1, rows, HG_WIDTH), lambda b, r: (s, blk0 + b * nr + r, 0))
    state = pl.BlockSpec((1, N_HEADS, HEAD_W, HEAD_W), lambda b, r: (b, 0, 0, 0))
    return pl.pallas_call(
        functools.partial(_gla_kernel, n_chunks=rows // CHUNK),
        grid=(n_seq, nr),
        in_specs=[
            seg(0), seg(1), seg(2), seg(3),
            pl.BlockSpec((lower_bounds.shape[0], HG_WIDTH), lambda b, r: (0, 0)),
            pl.BlockSpec((1, HEAD_W), lambda b, r: (0, 0)),
            state,
        ],
        out_specs=[pl.BlockSpec((rows, HG_WIDTH), lambda b, r: (b * nr + r, 0)), state],
        out_shape=[
            jax.ShapeDtypeStruct((m, HG_WIDTH), BF16),
            jax.ShapeDtypeStruct((n_seq, N_HEADS, HEAD_W, HEAD_W), F32),
        ],
        scratch_shapes=[pltpu.VMEM((HEAD_W, HG_WIDTH), F32),
                        pltpu.VMEM((N_HEADS, CHUNK, HEAD_W), F32),
                        pltpu.VMEM((N_HEADS, CHUNK, HEAD_W), F32)],
        compiler_params=pltpu.CompilerParams(
            dimension_semantics=("arbitrary", "arbitrary"), vmem_limit_bytes=VMEM_LIMIT),
        name="hgrn2",
    )(gates, gates, gates, gates, lower_bounds, hg_norm, s0)


def _stacked_query(q):
    lane = lax.broadcasted_iota(jnp.int32, q.shape, 1)
    qbig = jnp.concatenate([jnp.where(lane < ATT_DH, q, 0.0), jnp.where(lane >= ATT_DH, q, 0.0)], axis=0)
    return qbig.T.astype(BF16)


def _lambda(lq1_ref, lk1_ref, lq2_ref, lk2_ref):
    s1 = jnp.sum(lq1_ref[...] * lk1_ref[...], axis=-1, keepdims=True)
    s2 = jnp.sum(lq2_ref[...] * lk2_ref[...], axis=-1, keepdims=True)
    return jnp.exp(s1) - jnp.exp(s2) + LAM_INIT


def _attn_finish(acc, l, lam, sub, n):
    o_both = (acc * (1.0 / l)).T
    o = o_both[:n] - lam * o_both[n:]
    return _rmsnorm_rows(o, sub) * (1.0 - LAM_INIT)


def _attn_prompt_kernel(q_ref, qn_ref, k_ref, vt_ref, lq1_ref, lk1_ref, lq2_ref, lk2_ref, sub_ref, o_ref,
                        qq_ref, s_ref, smax_ref, mask_ref, m_ref, acc_ref, *, t, n_grp):
    qi = pl.program_id(1)
    grp = range(n_grp)
    slot = qi % 2
    m_ref[...] = jnp.full(m_ref.shape, NEG_INF, F32)
    acc_ref[...] = jnp.zeros(acc_ref.shape, F32)
    ones_rows = jnp.ones((ONES_ROWS, t), BF16)

    def values_t(g, kv):
        k0 = pl.multiple_of(kv * t, t)
        return jnp.concatenate([vt_ref[_head(g), pl.ds(k0, t)], ones_rows], axis=0)

    def first_scores(g, qslot, cols):
        s = jnp.dot(k_ref[0:t, _head(g)], qq_ref[qslot, g, :, cols], preferred_element_type=F32)
        s_ref[g, 0, :, cols] = s
        smax_ref[g, 0, :, cols] = jnp.max(s, axis=0, keepdims=True)

    def stage(kv, cur, oth):
        k_next = pl.multiple_of((kv + 1) * t, t)
        vt = [values_t(g, kv) for g in grp]
        for c in range(0, 2 * t, STAGE_COLS):
            cols = slice(c, c + STAGE_COLS)
            for g in grp:
                s_next = jnp.dot(k_ref[pl.ds(k_next, t), _head(g)], qq_ref[slot, g, :, cols],
                                 preferred_element_type=F32)
                s_ref[g, oth, :, cols] = s_next
                smax_ref[g, oth, :, cols] = jnp.max(s_next, axis=0, keepdims=True)
                m_prev = m_ref[g, :, cols]
                m_new = jnp.maximum(m_prev, smax_ref[g, cur, :, cols])
                m_ref[g, :, cols] = m_new
                p = jnp.exp2(s_ref[g, cur, :, cols] - m_new).astype(BF16)
                acc_ref[g, :, cols] = (jnp.exp2(m_prev - m_new) * acc_ref[g, :, cols]
                                       + jnp.dot(vt[g], p, preferred_element_type=F32))

    @pl.when((pl.program_id(0) == 0) & (qi == 0))
    def _():
        kpos = lax.broadcasted_iota(jnp.int32, mask_ref.shape, 0)
        col = lax.broadcasted_iota(jnp.int32, mask_ref.shape, 1)
        qpos = jnp.where(col >= t, col - t, col)
        mask_ref[...] = jnp.where((kpos >> 6) <= (qpos >> 6), 0.0, NEG_INF)

    def finish(cur):
        for g in grp:
            qq_ref[1 - slot, g] = _stacked_query(qn_ref[:, _head(g)].astype(F32))
        vt = [values_t(g, qi) for g in grp]
        acc = [[] for _ in grp]
        for c in range(0, 2 * t, FINISH_COLS):
            cols = slice(c, c + FINISH_COLS)
            for g in grp:
                s = s_ref[g, cur, :, cols] + mask_ref[:, cols]
                m_prev = m_ref[g, :, cols]
                m_new = jnp.maximum(m_prev, jnp.max(s, axis=0, keepdims=True))
                p = jnp.exp2(s - m_new).astype(BF16)
                acc[g].append(jnp.exp2(m_prev - m_new) * acc_ref[g, :, cols]
                              + jnp.dot(vt[g], p, preferred_element_type=F32))
                first_scores(g, 1 - slot, cols)
        lam = _lambda(lq1_ref, lk1_ref, lq2_ref, lk2_ref)
        for g in grp:
            a = jnp.concatenate(acc[g], axis=1)
            o_ref[:, _head(g)] = _attn_finish(a[:HEAD_W], a[HEAD_W:HEAD_W + 1], lam, sub_ref[...],
                                              t).astype(BF16)

    @pl.when(qi == 0)
    def _():
        for g in grp:
            qq_ref[0, g] = _stacked_query(q_ref[:, _head(g)].astype(F32))
        for c in range(0, 2 * t, FINISH_COLS):
            for g in grp:
                first_scores(g, 0, slice(c, c + FINISH_COLS))

    def pair(j, carry):
        stage(2 * j, 0, 1)
        stage(2 * j + 1, 1, 0)
        return carry

    lax.fori_loop(0, qi // 2, pair, 0)

    @pl.when(qi % 2 == 1)
    def _():
        stage(qi - 1, 0, 1)
        finish(1)

    @pl.when(qi % 2 == 0)
    def _():
        finish(0)


def _attn_prompt(q_bf, k_bf, vt_bf, lams, subln, n, t, n_grp):
    w = n_grp * HEAD_W
    small = lambda shape: pl.BlockSpec(shape, lambda h, i: (0, 0))
    return pl.pallas_call(
        functools.partial(_attn_prompt_kernel, t=t, n_grp=n_grp),
        grid=(N_HEADS // n_grp, n // t),
        in_specs=[
            pl.BlockSpec((t, w), lambda h, i: (i, h)),
            pl.BlockSpec((t, w), lambda h, i: (jnp.minimum(i + 1, n // t - 1), h)),
            pl.BlockSpec((n, w), lambda h, i: (0, h)),
            pl.BlockSpec((w, n), lambda h, i: (h, 0)),
            small((1, ATT_DH)), small((1, ATT_DH)), small((1, ATT_DH)), small((1, ATT_DH)),
            small((1, HEAD_W)),
        ],
        out_specs=pl.BlockSpec((t, w), lambda h, i: (i, h)),
        out_shape=jax.ShapeDtypeStruct((n, ATT_WIDTH), BF16),
        scratch_shapes=[
            pltpu.VMEM((2, n_grp, HEAD_W, 2 * t), BF16),
            pltpu.VMEM((n_grp, 2, t, 2 * t), F32),
            pltpu.VMEM((n_grp, 2, 1, 2 * t), F32),
            pltpu.VMEM((t, 2 * t), F32),
            pltpu.VMEM((n_grp, 1, 2 * t), F32),
            pltpu.VMEM((n_grp, HEAD_W + ONES_ROWS, 2 * t), F32),
        ],
        compiler_params=pltpu.CompilerParams(
            dimension_semantics=("arbitrary", "arbitrary"), vmem_limit_bytes=VMEM_LIMIT),
        name="attn_prompt",
    )(q_bf, q_bf, k_bf, vt_bf, *lams, subln)


def _attn_sample_kernel(q_ref, kc_ref, vc_ref, kn_ref, vn_ref, lq1_ref, lk1_ref, lq2_ref, lk2_ref,
                        sub_ref, o_ref, qq_ref, m_ref, l_ref, acc_ref, *, n_q, tk):
    t = pl.program_id(1)
    pairs = range(N_HEADS // 2)
    w2 = 2 * HEAD_W

    @pl.when(t == 0)
    def _():
        zero = jnp.zeros((HEAD_W, 2 * n_q), BF16)
        for j in pairs:
            qa = _stacked_query(q_ref[:, _head(2 * j)].astype(F32))
            qb = _stacked_query(q_ref[:, _head(2 * j + 1)].astype(F32))
            qq_ref[j] = jnp.concatenate([jnp.concatenate([qa, zero], axis=1),
                                         jnp.concatenate([zero, qb], axis=1)], axis=0)
        m_ref[...] = jnp.full(m_ref.shape, NEG_INF, F32)
        l_ref[...] = jnp.zeros(l_ref.shape, F32)
        acc_ref[...] = jnp.zeros(acc_ref.shape, F32)

    def update(k, v):
        s = [jnp.dot(k[j], qq_ref[j], preferred_element_type=F32) for j in pairs]
        p, alpha = [], []
        for j in pairs:
            m_prev = m_ref[j]
            m_new = jnp.maximum(m_prev, jnp.max(s[j], axis=0, keepdims=True))
            m_ref[j] = m_new
            a = jnp.exp2(m_prev - m_new)
            pj = jnp.exp2(s[j] - m_new)
            l_ref[j] = a * l_ref[j] + jnp.sum(pj, axis=0, keepdims=True)
            p.append(pj.astype(BF16))
            alpha.append(a)
        pv = [lax.dot_general(v[j], p[j], (((0,), (0,)), ((), ())), preferred_element_type=F32)
              for j in pairs]
        for j in pairs:
            acc_ref[j] = alpha[j] * acc_ref[j] + pv[j]

    def cache_pair(ref, j):
        return jnp.concatenate([ref[0, pl.ds(2 * j, tk, stride=N_HEADS), :],
                                ref[0, pl.ds(2 * j + 1, tk, stride=N_HEADS), :]], axis=1).astype(BF16)

    update([cache_pair(kc_ref, j) for j in pairs], [cache_pair(vc_ref, j) for j in pairs])

    @pl.when(t == pl.num_programs(1) - 1)
    def _():
        update([kn_ref[:, w2 * j:w2 * (j + 1)] for j in pairs], [vn_ref[:, w2 * j:w2 * (j + 1)] for j in pairs])
        lam = _lambda(lq1_ref, lk1_ref, lq2_ref, lk2_ref)
        for j in pairs:
            acc = acc_ref[j]
            l = l_ref[j]
            for i, h in enumerate((2 * j, 2 * j + 1)):
                o_ref[:, _head(h)] = _attn_finish(acc[_head(i), _head(i)], l[:, _head(i)], lam, sub_ref[...],
                                                  n_q).astype(BF16)


def _attn_sample(q_bf, cache_k, cache_v, kn_bf, vn_bf, lams, subln, row0, n_q, tk):
    n_b = cache_k.shape[0]
    past = cache_k.shape[1] // N_HEADS
    assert past % CHUNK == 0 and n_q <= CHUNK and past % tk == 0 and row0 % n_q == 0
    blk0 = row0 // n_q
    small = lambda shape: pl.BlockSpec(shape, lambda b, t: (0, 0))
    rows = pl.BlockSpec((n_q, ATT_WIDTH), lambda b, t: (b, 0))
    rows_all = pl.BlockSpec((n_q, ATT_WIDTH), lambda b, t: (blk0 + b, 0))
    cache = pl.BlockSpec((1, tk * N_HEADS, HEAD_W), lambda b, t: (b, t, 0))
    return pl.pallas_call(
        functools.partial(_attn_sample_kernel, n_q=n_q, tk=tk),
        grid=(n_b, past // tk),
        in_specs=[
            rows_all, cache, cache, rows_all, rows,
            small((1, ATT_DH)), small((1, ATT_DH)), small((1, ATT_DH)), small((1, ATT_DH)),
            small((1, HEAD_W)),
        ],
        out_specs=rows,
        out_shape=jax.ShapeDtypeStruct((n_b * n_q, ATT_WIDTH), BF16),
        scratch_shapes=[
            pltpu.VMEM((N_HEADS // 2, 2 * HEAD_W, 4 * n_q), BF16),
            pltpu.VMEM((N_HEADS // 2, 1, 4 * n_q), F32),
            pltpu.VMEM((N_HEADS // 2, 1, 4 * n_q), F32),
            pltpu.VMEM((N_HEADS // 2, 2 * HEAD_W, 4 * n_q), F32),
        ],
        compiler_params=pltpu.CompilerParams(
            dimension_semantics=("arbitrary", "arbitrary"), vmem_limit_bytes=VMEM_LIMIT),
        name="attn_sample",
    )(q_bf, cache_k, cache_v, kn_bf, vn_bf, *lams, subln)


def _outproj_kernel(x_ref, a_ref, b_ref, w_ref, o_ref, wb_ref):
    @pl.when(pl.program_id(0) == 0)
    def _():
        wb_ref[...] = w_ref[...].astype(BF16)

    o_ref[...] = (x_ref[...]
                  + jnp.dot(a_ref[...], wb_ref[:HG_WIDTH, :], preferred_element_type=F32)
                  + jnp.dot(b_ref[...], wb_ref[HG_WIDTH:, :], preferred_element_type=F32))


def _outproj(x, mix_hg, mix_at, w, tm):
    m = x.shape[0]
    row = lambda i: (i, 0)
    return pl.pallas_call(
        _outproj_kernel,
        grid=(m // tm,),
        in_specs=[
            pl.BlockSpec((tm, D_MODEL), row),
            pl.BlockSpec((tm, HG_WIDTH), row),
            pl.BlockSpec((tm, ATT_WIDTH), row),
            pl.BlockSpec((HG_WIDTH + ATT_WIDTH, D_MODEL), lambda i: (0, 0), pipeline_mode=pl.Buffered(1)),
        ],
        out_specs=pl.BlockSpec((tm, D_MODEL), row),
        out_shape=jax.ShapeDtypeStruct((m, D_MODEL), F32),
        scratch_shapes=[pltpu.VMEM((HG_WIDTH + ATT_WIDTH, D_MODEL), BF16)],
        compiler_params=pltpu.CompilerParams(
            dimension_semantics=("arbitrary",), vmem_limit_bytes=VMEM_LIMIT),
        name="outproj",
    )(x, mix_hg, mix_at, w)


def _mlp_kernel(x_ref, gain_ref, wu_ref, wd_ref, gfin_ref, o_ref, h_ref):
    j = pl.program_id(1)

    @pl.when(j == 0)
    def _():
        x = x_ref[...]
        h_ref[...] = _rmsnorm_rows(x, gain_ref[...]).astype(BF16)
        o_ref[...] = x

    u = jnp.dot(h_ref[...], wu_ref[...].astype(BF16), preferred_element_type=F32)
    u = jnp.square(jnp.maximum(u, 0.0)).astype(BF16)
    o_ref[...] += jnp.dot(u, wd_ref[...].astype(BF16), preferred_element_type=F32)

    @pl.when(j == pl.num_programs(1) - 1)
    def _():
        o_ref[...] = _rmsnorm_rows(o_ref[...], gfin_ref[...])


def _mlp(x, gain, wu, wd, gfin, tm, tf):
    m = x.shape[0]
    return pl.pallas_call(
        _mlp_kernel,
        grid=(m // tm, D_FF // tf),
        in_specs=[
            pl.BlockSpec((tm, D_MODEL), lambda i, j: (i, 0)),
            pl.BlockSpec((1, D_MODEL), lambda i, j: (0, 0)),
            pl.BlockSpec((D_MODEL, tf), lambda i, j: (0, j)),
            pl.BlockSpec((tf, D_MODEL), lambda i, j: (j, 0)),
            pl.BlockSpec((1, D_MODEL), lambda i, j: (0, 0)),
        ],
        out_specs=pl.BlockSpec((tm, D_MODEL), lambda i, j: (i, 0)),
        out_shape=jax.ShapeDtypeStruct((m, D_MODEL), F32),
        scratch_shapes=[pltpu.VMEM((tm, D_MODEL), BF16)],
        compiler_params=pltpu.CompilerParams(
            dimension_semantics=("arbitrary", "arbitrary"), vmem_limit_bytes=VMEM_LIMIT),
        name="mlp",
    )(x, gain, wu, wd, gfin)


def kernel(x_prompt, x_sample, cache_k, cache_v, state_hgrn, norm_attn, w_in, lower_bounds, hg_norm,
           lambda_q1, lambda_k1, lambda_q2, lambda_k2, subln, w_out, norm_mlp, w_up, w_down, norm_final):
    depth = w_in.shape[0]
    assert depth == 1
    n_pb, n_p, _ = x_prompt.shape
    n_sb, n_s, _ = x_sample.shape
    assert n_pb == 1
    past = cache_k.shape[2]

    lams = (lambda_q1, lambda_k1, lambda_q2, lambda_k2)
    gfin = norm_final.reshape(1, D_MODEL)

    xp = x_prompt.reshape(n_p, D_MODEL)
    xs = x_sample.reshape(n_sb * n_s, D_MODEL)

    gates, q_bf, k_bf, vt_bf, vs_bf, kf_p, vf_p, kf_s, vf_s = _inproj(xp, xs, norm_attn, w_in[0], tm=256)

    s0_p = jnp.zeros((1, N_HEADS, HEAD_W, HEAD_W), F32)
    ohg_p, st_p = _gla(gates, lower_bounds, hg_norm, s0_p, row0=0, n_seq=1, seq_len=n_p, rows=512)
    ohg_s, st_s = _gla(gates, lower_bounds, hg_norm, state_hgrn[0], row0=n_p, n_seq=n_sb, seq_len=n_s, rows=n_s)

    oat_p = _attn_prompt(q_bf, k_bf, vt_bf, lams, subln, n=n_p, t=512, n_grp=2)
    ck = cache_k[0].reshape(n_sb, past * N_HEADS, HEAD_W)
    cv = cache_v[0].reshape(n_sb, past * N_HEADS, HEAD_W)
    oat_s = _attn_sample(q_bf, ck, cv, k_bf, vs_bf, lams, subln, row0=n_p, n_q=n_s, tk=2048)

    x1_p = _outproj(xp, ohg_p, oat_p, w_out[0], tm=512)
    x1_s = _outproj(xs, ohg_s, oat_s, w_out[0], tm=512)

    y_p = _mlp(x1_p, norm_mlp, w_up[0], w_down[0], gfin, tm=1024, tf=512)
    y_s = _mlp(x1_s, norm_mlp, w_up[0], w_down[0], gfin, tm=512, tf=1024)

    return (
        y_p.reshape(n_pb, n_p, D_MODEL),
        y_s.reshape(n_sb, n_s, D_MODEL),
        kf_p.reshape(1, n_pb, n_p, N_HEADS, HEAD_W),
        vf_p.reshape(1, n_pb, n_p, N_HEADS, HEAD_W),
        st_p.reshape(1, n_pb, N_HEADS, HEAD_W, HEAD_W),
        kf_s.reshape(1, n_sb, n_s, N_HEADS, HEAD_W),
        vf_s.reshape(1, n_sb, n_s, N_HEADS, HEAD_W),
        st_s.reshape(1, n_sb, N_HEADS, HEAD_W, HEAD_W),
    )
```

```python
import functools
import math

import jax
import jax.numpy as jnp
from jax import lax
from jax.experimental import pallas as pl
from jax.experimental.pallas import tpu as pltpu

F32 = jnp.float32
BF16 = jnp.bfloat16

D_MODEL = 2048
HG_WIDTH = 1024
ATT_WIDTH = 1024
N_HEADS = 8
HEAD_W = 128
ATT_DH = 64
CHUNK = 64
SUB = 8
N_SUB = CHUNK // SUB
MIN_FACTORED_LB = 2.0 ** (-100.0 / SUB)
MIN_CHUNK_FACTORED_LB = 2.0 ** (-100.0 / CHUNK)
D_FF = 4 * D_MODEL
N_SEG = 7
EPS = 1e-6
NEG_INF = -1e30
LOG2E = 1.4426950408889634
Q_SCALE = ATT_DH ** -0.5 * LOG2E
LAM_INIT = 0.8 - 0.6 * math.exp(-0.3 * 0)
ONES_ROWS = 16
FINISH_COLS = 256
STAGE_COLS = 256

VMEM_LIMIT = 56 * 1024 * 1024


def _rmsnorm_rows(x, gain):
    return x * lax.rsqrt(jnp.mean(x * x, axis=-1, keepdims=True) + EPS) * gain


def _sigmoid(x):
    return 0.5 * jnp.tanh(0.5 * x) + 0.5


def _head(h):
    return slice(h * HEAD_W, (h + 1) * HEAD_W)


def _inproj_kernel(xp_ref, xs_ref, gain_ref, w_ref, gates_ref, q_ref, kb_ref, vt_ref, vbs_ref,
                   kfp_ref, vfp_ref, kfs_ref, vfs_ref, wb_ref, *, n_p, n_w):
    i = pl.program_id(0)

    w_rows = w_ref.shape[0]
    per_seg = D_MODEL // w_rows

    @pl.when(i < n_w)
    def _():
        j = jnp.minimum(i, n_w - 1)
        r0 = pl.multiple_of((j % per_seg) * w_rows, w_rows)
        wb_ref[j // per_seg, pl.ds(r0, w_rows), :] = w_ref[...].astype(BF16)

    def project(x_ref, kf_ref, vf_ref, vb_ref):
        tm = x_ref.shape[0]
        h = _rmsnorm_rows(x_ref[...], gain_ref[...]).astype(BF16)

        def segment(s):
            return jnp.dot(h, wb_ref[s], preferred_element_type=F32)

        k = segment(5)
        v = segment(6)
        kb_ref[...] = k.astype(BF16)
        vt_ref[...] = v.T.astype(BF16)
        if vb_ref is not None:
            vb_ref[...] = v.astype(BF16)
        for hd in range(N_HEADS):
            kf_ref[pl.ds(hd, tm, stride=N_HEADS), :] = k[:, _head(hd)]
            vf_ref[pl.ds(hd, tm, stride=N_HEADS), :] = v[:, _head(hd)]
        q_ref[...] = (segment(4) * Q_SCALE).astype(BF16)
        for s in range(4):
            gates_ref[s] = segment(s)

    @pl.when((i >= n_w) & (i < n_w + n_p))
    def _():
        project(xp_ref, kfp_ref, vfp_ref, None)

    @pl.when(i >= n_w + n_p)
    def _():
        project(xs_ref, kfs_ref, vfs_ref, vbs_ref)


def _inproj(xp, xs, gain, w, tm):
    m_p, m_s = xp.shape[0], xs.shape[0]
    n_p, n_s = m_p // tm, m_s // tm
    m = m_p + m_s
    seg = HG_WIDTH
    w_rows = 512
    per_seg = D_MODEL // w_rows
    n_w = N_SEG * per_seg

    def w_block(i):
        j = jnp.minimum(i, n_w - 1)
        return (j % per_seg, j // per_seg)

    tile = lambda i: jnp.maximum(i - n_w, 0)
    row = lambda i: (tile(i), 0)
    p_row = lambda i: (jnp.minimum(tile(i), n_p - 1), 0)
    s_row = lambda i: (jnp.maximum(tile(i) - n_p, 0), 0)
    once = dict(pipeline_mode=pl.Buffered(1))
    return pl.pallas_call(
        functools.partial(_inproj_kernel, n_p=n_p, n_w=n_w),
        grid=(n_w + n_p + n_s,),
        in_specs=[
            pl.BlockSpec((tm, D_MODEL), p_row),
            pl.BlockSpec((tm, D_MODEL), s_row, **once),
            pl.BlockSpec((1, D_MODEL), lambda i: (0, 0)),
            pl.BlockSpec((w_rows, seg), w_block),
        ],
        out_specs=[
            pl.BlockSpec((4, tm, seg), lambda i: (0, tile(i), 0)),
            pl.BlockSpec((tm, seg), row),
            pl.BlockSpec((tm, seg), row),
            pl.BlockSpec((seg, tm), lambda i: (0, tile(i))),
            pl.BlockSpec((tm, seg), s_row, **once),
            pl.BlockSpec((tm * N_HEADS, HEAD_W), p_row),
            pl.BlockSpec((tm * N_HEADS, HEAD_W), p_row),
            pl.BlockSpec((tm * N_HEADS, HEAD_W), s_row, **once),
            pl.BlockSpec((tm * N_HEADS, HEAD_W), s_row, **once),
        ],
        out_shape=[
            jax.ShapeDtypeStruct((4, m, seg), F32),
            jax.ShapeDtypeStruct((m, seg), BF16),
            jax.ShapeDtypeStruct((m, seg), BF16),
            jax.ShapeDtypeStruct((seg, m), BF16),
            jax.ShapeDtypeStruct((m_s, seg), BF16),
            jax.ShapeDtypeStruct((m_p * N_HEADS, HEAD_W), F32),
            jax.ShapeDtypeStruct((m_p * N_HEADS, HEAD_W), F32),
            jax.ShapeDtypeStruct((m_s * N_HEADS, HEAD_W), F32),
            jax.ShapeDtypeStruct((m_s * N_HEADS, HEAD_W), F32),
        ],
        scratch_shapes=[pltpu.VMEM((N_SEG, D_MODEL, seg), BF16)],
        compiler_params=pltpu.CompilerParams(
            dimension_semantics=("arbitrary",), vmem_limit_bytes=60 * 1024 * 1024),
        name="inproj",
    )(xp, xs, gain, w)


def _gla_kernel(hq_ref, hf_ref, hi_ref, hg_ref, lbnd_ref, gn_ref, s0_ref, o_ref, sout_ref, st_ref,
                q_ref, b_ref, *, n_chunks, seqs_per_step):
    r = pl.program_id(1)

    def load_state(i):
        return jnp.concatenate([s0_ref[i, h].T for h in range(N_HEADS)], axis=1)

    if seqs_per_step == 1:
        @pl.when(r == 0)
        def _():
            st_ref[...] = load_state(0)

    lbs = lbnd_ref[...]
    e = jnp.exp(lbs - jnp.max(lbs, axis=0, keepdims=True))
    lb = e[0:1, :] / jnp.sum(e, axis=0, keepdims=True)
    gn = jnp.concatenate([gn_ref[...]] * N_HEADS, axis=1)

    ri = lax.broadcasted_iota(jnp.int32, (CHUNK, CHUNK), 0)
    ci = lax.broadcasted_iota(jnp.int32, (CHUNK, CHUNK), 1)
    tril = (ri >= ci).astype(BF16)
    pr = lax.broadcasted_iota(jnp.int32, (2 * HEAD_W, 2 * HEAD_W), 0)
    pc = lax.broadcasted_iota(jnp.int32, (2 * HEAD_W, 2 * HEAD_W), 1)
    pair_ones = ((pr >> 7) == (pc >> 7)).astype(BF16)
    gr = lax.broadcasted_iota(jnp.int32, (CHUNK, CHUNK * SUB), 0)
    gc = lax.broadcasted_iota(jnp.int32, (CHUNK, CHUNK * SUB), 1)
    seg_sum = (((gc >> 3) == gr) & ((gc & (SUB - 1)) <= (gr & (SUB - 1)))).astype(BF16)
    n_off = (N_SUB - 1) * SUB
    n_key = SUB * (N_SUB * (N_SUB - 1) // 2)
    orow = lax.broadcasted_iota(jnp.int32, (n_off, n_key), 0) >> 3
    ocol = lax.broadcasted_iota(jnp.int32, (n_off, n_key), 1)
    ocol_seg = sum((ocol >= 4 * i * (i - 1)).astype(jnp.int32) for i in range(2, N_SUB))
    off_mask = orow == ocol_seg
    nt = (((1,), (1,)), ((), ()))
    tn = (((0,), (0,)), ((), ()))

    n_all = n_key + CHUNK
    frow = lax.broadcasted_iota(jnp.int32, (CHUNK, n_all), 0)
    fcol = lax.broadcasted_iota(jnp.int32, (CHUNK, n_all), 1)
    fseg = sum((fcol >= 4 * i * (i - 1)).astype(jnp.int32) for i in range(2, N_SUB))
    fsame = fcol - n_key
    all_mask = (((fcol < n_key) & (fseg + 1 == (frow >> 3)))
                | ((fcol >= n_key) & ((fsame >> 3) == (frow >> 3)) & ((fsame & (SUB - 1)) <= (frow & (SUB - 1)))))

    causal = ri >= ci

    def chunk(c, carry, mode):
        r0 = pl.multiple_of(c * CHUNK, CHUNK)
        hq = hq_ref[0, pl.ds(r0, CHUNK), :]
        hf = hf_ref[0, pl.ds(r0, CHUNK), :]
        v = hi_ref[0, pl.ds(r0, CHUNK), :]
        hg = hg_ref[0, pl.ds(r0, CHUNK), :]

        f = lb + (1.0 - lb) * _sigmoid(hf)
        g = jnp.log(f) * LOG2E
        kk = 1.0 - f
        q = hq * _sigmoid(hq)
        g_hi = g.astype(BF16)
        g_rest = g - g_hi.astype(F32)
        g_mid = g_rest.astype(BF16)
        g_lo = (g_rest - g_mid.astype(F32)).astype(BF16)
        b3 = jnp.dot(tril, jnp.concatenate([g_hi, g_mid, g_lo], axis=1), preferred_element_type=F32)
        b = b3[:, :HG_WIDTH] + b3[:, HG_WIDTH:2 * HG_WIDTH] + b3[:, 2 * HG_WIDTH:]
        v_bf = v.astype(BF16)

        st = st_ref[...] if seqs_per_step == 1 else load_state(c)
        st_bf = st.astype(BF16)
        b_last = b[CHUNK - 1:CHUNK, :]
        q_dec = q * jnp.exp2(b)
        q_in = q_dec.astype(BF16)
        k_dec = (kk * jnp.exp2(b_last - b)).astype(BF16)
        o_inter = jnp.concatenate(
            [lax.dot_general(q_in[:, _head(h)], st_bf[:, _head(h)], nt, preferred_element_type=F32)
             for h in range(N_HEADS)], axis=1)
        upd = jnp.concatenate(
            [lax.dot_general(v_bf[:, _head(h)], k_dec[:, _head(h)], tn, preferred_element_type=F32)
             for h in range(N_HEADS)], axis=1)
        st_new = jnp.exp2(b_last) * st + upd
        if seqs_per_step == 1:
            st_ref[...] = st_new
        else:
            for h in range(N_HEADS):
                sout_ref[c, h] = st_new[:, _head(h)].T

        qt, kh, vh = [], [], []
        for i in range(1, N_SUB) if mode != "chunk" else ():
            lo = i * SUB
            b_start = b[lo - 1:lo, :]
            qt.append(q[lo:lo + SUB] * jnp.exp2(b[lo:lo + SUB] - b_start))
            kh.append(kk[:lo] * jnp.exp2(b_start - b[:lo]))
            vh.append(v[:lo])

        if mode == "chunk":
            k_inv = (kk * jnp.exp2(-b)).astype(BF16)
            a_all = [lax.dot_general(q_in[:, _head(h)], k_inv[:, _head(h)], nt, preferred_element_type=F32)
                     for h in range(N_HEADS)]
            o_intra = jnp.concatenate(
                [jnp.dot(jnp.where(causal, a_all[h], 0.0).astype(BF16), v_bf[:, _head(h)],
                         preferred_element_type=F32) for h in range(N_HEADS)], axis=1)
        elif mode == "sub_block":
            b0 = jnp.concatenate(
                [jnp.zeros((SUB, HG_WIDTH), F32)]
                + [jnp.broadcast_to(b[i * SUB - 1:i * SUB, :], (SUB, HG_WIDTH)) for i in range(1, N_SUB)], axis=0)
            kd = kk * jnp.exp2(b0 - b)
            q_all = jnp.concatenate([q_dec[:SUB]] + qt, axis=0).astype(BF16)
            k_all = jnp.concatenate(kh + [kd], axis=0).astype(BF16)
            v_all = jnp.concatenate(vh + [v], axis=0).astype(BF16)
            a_all = [lax.dot_general(q_all[:, _head(h)], k_all[:, _head(h)], nt, preferred_element_type=F32)
                     for h in range(N_HEADS)]
            o_intra = jnp.concatenate(
                [jnp.dot(jnp.where(all_mask, a_all[h], 0.0).astype(BF16), v_all[:, _head(h)],
                         preferred_element_type=F32) for h in range(N_HEADS)], axis=1)
        else:
            for h in range(N_HEADS):
                q_ref[h] = q[:, _head(h)]
                b_ref[h] = b[:, _head(h)]
            qt = jnp.concatenate(qt, axis=0).astype(BF16)
            kh = jnp.concatenate(kh, axis=0).astype(BF16)
            vh = jnp.concatenate(vh, axis=0).astype(BF16)
            a_off = [lax.dot_general(qt[:, _head(h)], kh[:, _head(h)], nt, preferred_element_type=F32)
                     for h in range(N_HEADS)]

            a_rep = []
            for j in range(N_HEADS // 2):
                p_pair = []
                for h in (2 * j, 2 * j + 1):
                    kk_h = kk[:, _head(h)]
                    b_h = b[:, _head(h)]
                    rows = []
                    for r in range(CHUNK):
                        lo = r - r % SUB
                        q_row = q_ref[h, pl.ds(r, SUB, stride=0), :]
                        b_row = b_ref[h, pl.ds(r, SUB, stride=0), :]
                        rows.append((q_row * kk_h[lo:lo + SUB])
                                    * jnp.exp2(jnp.minimum(b_row - b_h[lo:lo + SUB], 0.0)))
                    p_pair.append(jnp.concatenate(rows, axis=0).astype(BF16))
                a_rep.append(jnp.dot(jnp.concatenate(p_pair, axis=1), pair_ones,
                                     preferred_element_type=F32))

            o_off = jnp.concatenate(
                [jnp.dot(jnp.where(off_mask, a_off[h], 0.0).astype(BF16), vh[:, _head(h)],
                         preferred_element_type=F32) for h in range(N_HEADS)], axis=1)
            o_diag = []
            for j in range(N_HEADS // 2):
                pair = slice(2 * HEAD_W * j, 2 * HEAD_W * (j + 1))
                v_rep = jnp.broadcast_to(v[:, pair].reshape(N_SUB, 1, SUB, 2 * HEAD_W),
                                         (N_SUB, SUB, SUB, 2 * HEAD_W))
                w = (a_rep[j].reshape(N_SUB, SUB, SUB, 2 * HEAD_W) * v_rep).reshape(CHUNK * SUB, 2 * HEAD_W)
                o_diag.append(jnp.dot(seg_sum, w.astype(BF16), preferred_element_type=F32))
            o_intra = jnp.concatenate(o_diag, axis=1) + jnp.concatenate(
                [jnp.zeros((SUB, HG_WIDTH), F32), o_off], axis=0)
        o = o_inter + o_intra

        y = jnp.concatenate(
            [o[:, _head(h)] * lax.rsqrt(jnp.mean(o[:, _head(h)] * o[:, _head(h)], axis=-1, keepdims=True) + EPS)
             for h in range(N_HEADS)], axis=1)
        y = y * gn * (hg * _sigmoid(hg))
        o_ref[pl.ds(r0, CHUNK), :] = y.astype(BF16)
        return carry

    lb_min = jnp.min(lb)
    unroll = min(n_chunks, 4)

    @pl.when(lb_min > MIN_CHUNK_FACTORED_LB)
    def _():
        lax.fori_loop(0, n_chunks, functools.partial(chunk, mode="chunk"), 0, unroll=unroll)

    @pl.when((lb_min > MIN_FACTORED_LB) & (lb_min <= MIN_CHUNK_FACTORED_LB))
    def _():
        lax.fori_loop(0, n_chunks, functools.partial(chunk, mode="sub_block"), 0, unroll=unroll)

    @pl.when(lb_min <= MIN_FACTORED_LB)
    def _():
        lax.fori_loop(0, n_chunks, functools.partial(chunk, mode="pairwise"), 0)

    if seqs_per_step == 1:
        @pl.when(r == pl.num_programs(1) - 1)
        def _():
            st = st_ref[...]
            for h in range(N_HEADS):
                sout_ref[0, h] = st[:, _head(h)].T


def _gla(gates, lower_bounds, hg_norm, s0, row0, n_seq, seq_len, rows, seqs_per_step=1):
    m = n_seq * seq_len
    assert seqs_per_step == 1 or (seq_len == CHUNK and rows == seqs_per_step * CHUNK and n_seq % seqs_per_step == 0)
    nr = max(seq_len // rows, 1)
    blk0 = row0 // rows
    seg = lambda s: pl.BlockSpec((1, rows, HG_WIDTH), lambda b, r: (s, blk0 + b * nr + r, 0))
    state = pl.BlockSpec((seqs_per_step, N_HEADS, HEAD_W, HEAD_W), lambda b, r: (b, 0, 0, 0))
    return pl.pallas_call(
        functools.partial(_gla_kernel, n_chunks=rows // CHUNK, seqs_per_step=seqs_per_step),
        grid=(n_seq // seqs_per_step, nr),
        in_specs=[
            seg(0), seg(1), seg(2), seg(3),
            pl.BlockSpec((lower_bounds.shape[0], HG_WIDTH), lambda b, r: (0, 0)),
            pl.BlockSpec((1, HEAD_W), lambda b, r: (0, 0)),
            state,
        ],
        out_specs=[pl.BlockSpec((rows, HG_WIDTH), lambda b, r: (b * nr + r, 0)), state],
        out_shape=[
            jax.ShapeDtypeStruct((m, HG_WIDTH), BF16),
            jax.ShapeDtypeStruct((n_seq, N_HEADS, HEAD_W, HEAD_W), F32),
        ],
        scratch_shapes=[pltpu.VMEM((HEAD_W, HG_WIDTH), F32),
                        pltpu.VMEM((N_HEADS, CHUNK, HEAD_W), F32),
                        pltpu.VMEM((N_HEADS, CHUNK, HEAD_W), F32)],
        compiler_params=pltpu.CompilerParams(
            dimension_semantics=("arbitrary", "arbitrary"), vmem_limit_bytes=VMEM_LIMIT),
        name="hgrn2",
    )(gates, gates, gates, gates, lower_bounds, hg_norm, s0)


def _stacked_query(q):
    lane = lax.broadcasted_iota(jnp.int32, q.shape, 1)
    qbig = jnp.concatenate([jnp.where(lane < ATT_DH, q, 0.0), jnp.where(lane >= ATT_DH, q, 0.0)], axis=0)
    return qbig.T.astype(BF16)


def _lambda(lq1_ref, lk1_ref, lq2_ref, lk2_ref):
    s1 = jnp.sum(lq1_ref[...] * lk1_ref[...], axis=-1, keepdims=True)
    s2 = jnp.sum(lq2_ref[...] * lk2_ref[...], axis=-1, keepdims=True)
    return jnp.exp(s1) - jnp.exp(s2) + LAM_INIT


def _attn_finish(acc, l, lam, sub, n):
    o_both = (acc * (1.0 / l)).T
    o = o_both[:n] - lam * o_both[n:]
    return _rmsnorm_rows(o, sub) * (1.0 - LAM_INIT)


def _attn_prompt_kernel(q_ref, qn_ref, k_ref, vt_ref, lq1_ref, lk1_ref, lq2_ref, lk2_ref, sub_ref, o_ref,
                        qq_ref, s_ref, smax_ref, mask_ref, m_ref, acc_ref, *, t, n_grp):
    qi = pl.program_id(1)
    grp = range(n_grp)
    slot = qi % 2
    m_ref[...] = jnp.full(m_ref.shape, NEG_INF, F32)
    acc_ref[...] = jnp.zeros(acc_ref.shape, F32)
    ones_rows = jnp.ones((ONES_ROWS, t), BF16)

    def values_t(g, kv):
        k0 = pl.multiple_of(kv * t, t)
        return jnp.concatenate([vt_ref[_head(g), pl.ds(k0, t)], ones_rows], axis=0)

    def first_scores(g, qslot, cols):
        s = jnp.dot(k_ref[0:t, _head(g)], qq_ref[qslot, g, :, cols], preferred_element_type=F32)
        s_ref[g, 0, :, cols] = s
        smax_ref[g, 0, :, cols] = jnp.max(s, axis=0, keepdims=True)

    def stage(kv, cur, oth):
        k_next = pl.multiple_of((kv + 1) * t, t)
        vt = [values_t(g, kv) for g in grp]
        for c in range(0, 2 * t, STAGE_COLS):
            cols = slice(c, c + STAGE_COLS)
            for g in grp:
                s_next = jnp.dot(k_ref[pl.ds(k_next, t), _head(g)], qq_ref[slot, g, :, cols],
                                 preferred_element_type=F32)
                s_ref[g, oth, :, cols] = s_next
                smax_ref[g, oth, :, cols] = jnp.max(s_next, axis=0, keepdims=True)
                m_prev = m_ref[g, :, cols]
                m_new = jnp.maximum(m_prev, smax_ref[g, cur, :, cols])
                m_ref[g, :, cols] = m_new
                p = jnp.exp2(s_ref[g, cur, :, cols] - m_new).astype(BF16)
                acc_ref[g, :, cols] = (jnp.exp2(m_prev - m_new) * acc_ref[g, :, cols]
                                       + jnp.dot(vt[g], p, preferred_element_type=F32))

    @pl.when((pl.program_id(0) == 0) & (qi == 0))
    def _():
        kpos = lax.broadcasted_iota(jnp.int32, mask_ref.shape, 0)
        col = lax.broadcasted_iota(jnp.int32, mask_ref.shape, 1)
        qpos = jnp.where(col >= t, col - t, col)
        mask_ref[...] = jnp.where((kpos >> 6) <= (qpos >> 6), 0.0, NEG_INF)

    def finish(cur):
        for g in grp:
            qq_ref[1 - slot, g] = _stacked_query(qn_ref[:, _head(g)].astype(F32))
        vt = [values_t(g, qi) for g in grp]
        acc = [[] for _ in grp]
        for c in range(0, 2 * t, FINISH_COLS):
            cols = slice(c, c + FINISH_COLS)
            for g in grp:
                s = s_ref[g, cur, :, cols] + mask_ref[:, cols]
                m_prev = m_ref[g, :, cols]
                m_new = jnp.maximum(m_prev, jnp.max(s, axis=0, keepdims=True))
                p = jnp.exp2(s - m_new).astype(BF16)
                acc[g].append(jnp.exp2(m_prev - m_new) * acc_ref[g, :, cols]
                              + jnp.dot(vt[g], p, preferred_element_type=F32))
                first_scores(g, 1 - slot, cols)
        lam = _lambda(lq1_ref, lk1_ref, lq2_ref, lk2_ref)
        for g in grp:
            a = jnp.concatenate(acc[g], axis=1)
            o_ref[:, _head(g)] = _attn_finish(a[:HEAD_W], a[HEAD_W:HEAD_W + 1], lam, sub_ref[...],
                                              t).astype(BF16)

    @pl.when(qi == 0)
    def _():
        for g in grp:
            qq_ref[0, g] = _stacked_query(q_ref[:, _head(g)].astype(F32))
        for c in range(0, 2 * t, FINISH_COLS):
            for g in grp:
                first_scores(g, 0, slice(c, c + FINISH_COLS))

    def pair(j, carry):
        stage(2 * j, 0, 1)
        stage(2 * j + 1, 1, 0)
        return carry

    lax.fori_loop(0, qi // 2, pair, 0)

    @pl.when(qi % 2 == 1)
    def _():
        stage(qi - 1, 0, 1)
        finish(1)

    @pl.when(qi % 2 == 0)
    def _():
        finish(0)


def _attn_prompt(q_bf, k_bf, vt_bf, lams, subln, n, t, n_grp):
    w = n_grp * HEAD_W
    small = lambda shape: pl.BlockSpec(shape, lambda h, i: (0, 0))
    return pl.pallas_call(
        functools.partial(_attn_prompt_kernel, t=t, n_grp=n_grp),
        grid=(N_HEADS // n_grp, n // t),
        in_specs=[
            pl.BlockSpec((t, w), lambda h, i: (i, h)),
            pl.BlockSpec((t, w), lambda h, i: (jnp.minimum(i + 1, n // t - 1), h)),
            pl.BlockSpec((n, w), lambda h, i: (0, h)),
            pl.BlockSpec((w, n), lambda h, i: (h, 0)),
            small((1, ATT_DH)), small((1, ATT_DH)), small((1, ATT_DH)), small((1, ATT_DH)),
            small((1, HEAD_W)),
        ],
        out_specs=pl.BlockSpec((t, w), lambda h, i: (i, h)),
        out_shape=jax.ShapeDtypeStruct((n, ATT_WIDTH), BF16),
        scratch_shapes=[
            pltpu.VMEM((2, n_grp, HEAD_W, 2 * t), BF16),
            pltpu.VMEM((n_grp, 2, t, 2 * t), F32),
            pltpu.VMEM((n_grp, 2, 1, 2 * t), F32),
            pltpu.VMEM((t, 2 * t), F32),
            pltpu.VMEM((n_grp, 1, 2 * t), F32),
            pltpu.VMEM((n_grp, HEAD_W + ONES_ROWS, 2 * t), F32),
        ],
        compiler_params=pltpu.CompilerParams(
            dimension_semantics=("arbitrary", "arbitrary"), vmem_limit_bytes=VMEM_LIMIT),
        name="attn_prompt",
    )(q_bf, q_bf, k_bf, vt_bf, *lams, subln)


def _attn_sample_kernel(q_ref, kc_ref, vc_ref, kn_ref, vn_ref, lq1_ref, lk1_ref, lq2_ref, lk2_ref,
                        sub_ref, o_ref, qq_ref, m_ref, l_ref, acc_ref, *, n_q, tk):
    t = pl.program_id(1)
    pairs = range(N_HEADS // 2)
    w2 = 2 * HEAD_W

    @pl.when(t == 0)
    def _():
        zero = jnp.zeros((HEAD_W, 2 * n_q), BF16)
        for j in pairs:
            qa = _stacked_query(q_ref[:, _head(2 * j)].astype(F32))
            qb = _stacked_query(q_ref[:, _head(2 * j + 1)].astype(F32))
            qq_ref[j] = jnp.concatenate([jnp.concatenate([qa, zero], axis=1),
                                         jnp.concatenate([zero, qb], axis=1)], axis=0)
        m_ref[...] = jnp.full(m_ref.shape, NEG_INF, F32)
        l_ref[...] = jnp.zeros(l_ref.shape, F32)
        acc_ref[...] = jnp.zeros(acc_ref.shape, F32)

    def update(k, v):
        s = [jnp.dot(k[j], qq_ref[j], preferred_element_type=F32) for j in pairs]
        p, alpha = [], []
        for j in pairs:
            m_prev = m_ref[j]
            m_new = jnp.maximum(m_prev, jnp.max(s[j], axis=0, keepdims=True))
            m_ref[j] = m_new
            a = jnp.exp2(m_prev - m_new)
            pj = jnp.exp2(s[j] - m_new)
            l_ref[j] = a * l_ref[j] + jnp.sum(pj, axis=0, keepdims=True)
            p.append(pj.astype(BF16))
            alpha.append(a)
        pv = [lax.dot_general(v[j], p[j], (((0,), (0,)), ((), ())), preferred_element_type=F32)
              for j in pairs]
        for j in pairs:
            acc_ref[j] = alpha[j] * acc_ref[j] + pv[j]

    def cache_pair(ref, j):
        return jnp.concatenate([ref[0, pl.ds(2 * j, tk, stride=N_HEADS), :],
                                ref[0, pl.ds(2 * j + 1, tk, stride=N_HEADS), :]], axis=1).astype(BF16)

    update([cache_pair(kc_ref, j) for j in pairs], [cache_pair(vc_ref, j) for j in pairs])

    @pl.when(t == pl.num_programs(1) - 1)
    def _():
        update([kn_ref[:, w2 * j:w2 * (j + 1)] for j in pairs], [vn_ref[:, w2 * j:w2 * (j + 1)] for j in pairs])
        lam = _lambda(lq1_ref, lk1_ref, lq2_ref, lk2_ref)
        for j in pairs:
            acc = acc_ref[j]
            l = l_ref[j]
            for i, h in enumerate((2 * j, 2 * j + 1)):
                o_ref[:, _head(h)] = _attn_finish(acc[_head(i), _head(i)], l[:, _head(i)], lam, sub_ref[...],
                                                  n_q).astype(BF16)


def _attn_sample(q_bf, cache_k, cache_v, kn_bf, vn_bf, lams, subln, row0, n_q, tk):
    n_b = cache_k.shape[0]
    past = cache_k.shape[1] // N_HEADS
    assert past % CHUNK == 0 and n_q <= CHUNK and past % tk == 0 and row0 % n_q == 0
    blk0 = row0 // n_q
    small = lambda shape: pl.BlockSpec(shape, lambda b, t: (0, 0))
    rows = pl.BlockSpec((n_q, ATT_WIDTH), lambda b, t: (b, 0))
    rows_all = pl.BlockSpec((n_q, ATT_WIDTH), lambda b, t: (blk0 + b, 0))
    cache = pl.BlockSpec((1, tk * N_HEADS, HEAD_W), lambda b, t: (b, t, 0))
    return pl.pallas_call(
        functools.partial(_attn_sample_kernel, n_q=n_q, tk=tk),
        grid=(n_b, past // tk),
        in_specs=[
            rows_all, cache, cache, rows_all, rows,
            small((1, ATT_DH)), small((1, ATT_DH)), small((1, ATT_DH)), small((1, ATT_DH)),
            small((1, HEAD_W)),
        ],
        out_specs=rows,
        out_shape=jax.ShapeDtypeStruct((n_b * n_q, ATT_WIDTH), BF16),
        scratch_shapes=[
            pltpu.VMEM((N_HEADS // 2, 2 * HEAD_W, 4 * n_q), BF16),
            pltpu.VMEM((N_HEADS // 2, 1, 4 * n_q), F32),
            pltpu.VMEM((N_HEADS // 2, 1, 4 * n_q), F32),
            pltpu.VMEM((N_HEADS // 2, 2 * HEAD_W, 4 * n_q), F32),
        ],
        compiler_params=pltpu.CompilerParams(
            dimension_semantics=("arbitrary", "arbitrary"), vmem_limit_bytes=VMEM_LIMIT),
        name="attn_sample",
    )(q_bf, cache_k, cache_v, kn_bf, vn_bf, *lams, subln)


def _outproj_kernel(x_ref, a_ref, b_ref, w_ref, o_ref, wb_ref):
    @pl.when(pl.program_id(0) == 0)
    def _():
        wb_ref[...] = w_ref[...].astype(BF16)

    o_ref[...] = (x_ref[...]
                  + jnp.dot(a_ref[...], wb_ref[:HG_WIDTH, :], preferred_element_type=F32)
                  + jnp.dot(b_ref[...], wb_ref[HG_WIDTH:, :], preferred_element_type=F32))


def _outproj(x, mix_hg, mix_at, w, tm):
    m = x.shape[0]
    row = lambda i: (i, 0)
    return pl.pallas_call(
        _outproj_kernel,
        grid=(m // tm,),
        in_specs=[
            pl.BlockSpec((tm, D_MODEL), row),
            pl.BlockSpec((tm, HG_WIDTH), row),
            pl.BlockSpec((tm, ATT_WIDTH), row),
            pl.BlockSpec((HG_WIDTH + ATT_WIDTH, D_MODEL), lambda i: (0, 0), pipeline_mode=pl.Buffered(1)),
        ],
        out_specs=pl.BlockSpec((tm, D_MODEL), row),
        out_shape=jax.ShapeDtypeStruct((m, D_MODEL), F32),
        scratch_shapes=[pltpu.VMEM((HG_WIDTH + ATT_WIDTH, D_MODEL), BF16)],
        compiler_params=pltpu.CompilerParams(
            dimension_semantics=("arbitrary",), vmem_limit_bytes=VMEM_LIMIT),
        name="outproj",
    )(x, mix_hg, mix_at, w)


def _mlp_kernel(x_ref, gain_ref, wu_ref, wd_ref, gfin_ref, o_ref, h_ref):
    j = pl.program_id(1)

    @pl.when(j == 0)
    def _():
        x = x_ref[...]
        h_ref[...] = _rmsnorm_rows(x, gain_ref[...]).astype(BF16)
        o_ref[...] = x

    u = jnp.dot(h_ref[...], wu_ref[...].astype(BF16), preferred_element_type=F32)
    u = jnp.square(jnp.maximum(u, 0.0)).astype(BF16)
    o_ref[...] += jnp.dot(u, wd_ref[...].astype(BF16), preferred_element_type=F32)

    @pl.when(j == pl.num_programs(1) - 1)
    def _():
        o_ref[...] = _rmsnorm_rows(o_ref[...], gfin_ref[...])


def _mlp(x, gain, wu, wd, gfin, tm, tf):
    m = x.shape[0]
    return pl.pallas_call(
        _mlp_kernel,
        grid=(m // tm, D_FF // tf),
        in_specs=[
            pl.BlockSpec((tm, D_MODEL), lambda i, j: (i, 0)),
            pl.BlockSpec((1, D_MODEL), lambda i, j: (0, 0)),
            pl.BlockSpec((D_MODEL, tf), lambda i, j: (0, j)),
            pl.BlockSpec((tf, D_MODEL), lambda i, j: (j, 0)),
            pl.BlockSpec((1, D_MODEL), lambda i, j: (0, 0)),
        ],
        out_specs=pl.BlockSpec((tm, D_MODEL), lambda i, j: (i, 0)),
        out_shape=jax.ShapeDtypeStruct((m, D_MODEL), F32),
        scratch_shapes=[pltpu.VMEM((tm, D_MODEL), BF16)],
        compiler_params=pltpu.CompilerParams(
            dimension_semantics=("arbitrary", "arbitrary"), vmem_limit_bytes=VMEM_LIMIT),
        name="mlp",
    )(x, gain, wu, wd, gfin)


def kernel(x_prompt, x_sample, cache_k, cache_v, state_hgrn, norm_attn, w_in, lower_bounds, hg_norm,
           lambda_q1, lambda_k1, lambda_q2, lambda_k2, subln, w_out, norm_mlp, w_up, w_down, norm_final):
    depth = w_in.shape[0]
    assert depth == 1
    n_pb, n_p, _ = x_prompt.shape
    n_sb, n_s, _ = x_sample.shape
    assert n_pb == 1
    past = cache_k.shape[2]

    lams = (lambda_q1, lambda_k1, lambda_q2, lambda_k2)
    gfin = norm_final.reshape(1, D_MODEL)

    xp = x_prompt.reshape(n_p, D_MODEL)
    xs = x_sample.reshape(n_sb * n_s, D_MODEL)

    gates, q_bf, k_bf, vt_bf, vs_bf, kf_p, vf_p, kf_s, vf_s = _inproj(xp, xs, norm_attn, w_in[0], tm=256)

    s0_p = jnp.zeros((1, N_HEADS, HEAD_W, HEAD_W), F32)
    ohg_p, st_p = _gla(gates, lower_bounds, hg_norm, s0_p, row0=0, n_seq=1, seq_len=n_p, rows=512)
    ohg_s, st_s = _gla(gates, lower_bounds, hg_norm, state_hgrn[0], row0=n_p, n_seq=n_sb, seq_len=n_s,
                       rows=4 * n_s, seqs_per_step=4)

    oat_p = _attn_prompt(q_bf, k_bf, vt_bf, lams, subln, n=n_p, t=512, n_grp=2)
    ck = cache_k[0].reshape(n_sb, past * N_HEADS, HEAD_W)
    cv = cache_v[0].reshape(n_sb, past * N_HEADS, HEAD_W)
    oat_s = _attn_sample(q_bf, ck, cv, k_bf, vs_bf, lams, subln, row0=n_p, n_q=n_s, tk=2048)

    x1_p = _outproj(xp, ohg_p, oat_p, w_out[0], tm=512)
    x1_s = _outproj(xs, ohg_s, oat_s, w_out[0], tm=512)

    y_p = _mlp(x1_p, norm_mlp, w_up[0], w_down[0], gfin, tm=1024, tf=512)
    y_s = _mlp(x1_s, norm_mlp, w_up[0], w_down[0], gfin, tm=512, tf=1024)

    return (
        y_p.reshape(n_pb, n_p, D_MODEL),
        y_s.reshape(n_sb, n_s, D_MODEL),
        kf_p.reshape(1, n_pb, n_p, N_HEADS, HEAD_W),
        vf_p.reshape(1, n_pb, n_p, N_HEADS, HEAD_W),
        st_p.reshape(1, n_pb, N_HEADS, HEAD_W, HEAD_W),
        kf_s.reshape(1, n_sb, n_s, N_HEADS, HEAD_W),
        vf_s.reshape(1, n_sb, n_s, N_HEADS, HEAD_W),
        st_s.reshape(1, n_sb, N_HEADS, HEAD_W, HEAD_W),
    )
```
